```python
import jax, jax.numpy as jnp
from jax import lax
import numpy as np

D_MODEL = 2048
BATCH = 4
SEQ = 4096
DEPTH = 1

MIX_WIDTH = D_MODEL
MLA_WIDTH = D_MODEL // 2
HGRN_WIDTH = MIX_WIDTH - MLA_WIDTH
MLA_HEADS = 8
MLA_V_DIM = MLA_WIDTH // MLA_HEADS
MLA_NOPE_DIM = 128
MLA_ROPE_DIM = 64
MLA_Q_RANK = D_MODEL // 4
MLA_KV_RANK = D_MODEL // 8
ROPE_BASE = 10000.0
Q_BLOCK = 128
HGRN_HEADS = 8
HGRN_KEY_DIM = 128
HGRN_VAL_DIM = HGRN_WIDTH // HGRN_HEADS
HGRN_QK_WIDTH = HGRN_HEADS * HGRN_KEY_DIM
HGRN_CHUNK = 64
D_FF = 4 * D_MODEL
NORM_EPS = 1e-6
N_MOD = 6
IN_WIDTH = MLA_Q_RANK + MLA_KV_RANK + MLA_ROPE_DIM + 3 * HGRN_QK_WIDTH + 2 * HGRN_WIDTH

kernel_name = "hymba_mla_hgrn2_adaln_encoder_layer"


def _in_proj_offsets():
    widths = (MLA_Q_RANK, MLA_KV_RANK, MLA_ROPE_DIM, HGRN_QK_WIDTH, HGRN_QK_WIDTH,
              HGRN_QK_WIDTH, HGRN_WIDTH, HGRN_WIDTH)
    return [int(o) for o in np.cumsum(widths)[:-1]]


def _rmsnorm(x, g):
    xf = x.astype(jnp.float32)
    y = xf * lax.rsqrt(jnp.mean(xf * xf, axis=-1, keepdims=True) + NORM_EPS)
    return (y * g.astype(jnp.float32)).astype(x.dtype)


def _rope_angles(positions):
    half = MLA_ROPE_DIM // 2
    inv_freq = ROPE_BASE ** (-jnp.arange(half, dtype=jnp.float32) / half)
    ang = positions.astype(jnp.float32)[..., None] * inv_freq
    return jnp.cos(ang), jnp.sin(ang)


def _apply_rope(x, cos, sin):
    x1, x2 = jnp.split(x.astype(jnp.float32), 2, axis=-1)
    return jnp.concatenate([x1 * cos - x2 * sin, x1 * sin + x2 * cos], axis=-1).astype(x.dtype)


def _mla(q_lat, kv_lat, k_rope, cos, sin, q_norm_g, kv_norm_g, w_uq, w_ukv):
    B, S, _ = q_lat.shape
    H = MLA_HEADS
    q = (_rmsnorm(q_lat, q_norm_g) @ w_uq).reshape(B, S, H, MLA_NOPE_DIM + MLA_ROPE_DIM)
    q_nope = q[..., :MLA_NOPE_DIM]
    q_rope = _apply_rope(q[..., MLA_NOPE_DIM:], cos[:, :, None, :], sin[:, :, None, :])
    kv = (_rmsnorm(kv_lat, kv_norm_g) @ w_ukv).reshape(B, S, H, MLA_NOPE_DIM + MLA_V_DIM)
    k_nope, v = kv[..., :MLA_NOPE_DIM], kv[..., MLA_NOPE_DIM:]
    k_rope = _apply_rope(k_rope, cos, sin)
    scale = (MLA_NOPE_DIM + MLA_ROPE_DIM) ** -0.5
    nb = S // Q_BLOCK

    def to_blocks(t):
        return t.reshape(B, nb, Q_BLOCK, *t.shape[2:]).swapaxes(0, 1)

    def attend(blk):
        qn, qr = blk
        s = (jnp.einsum('bqhd,bkhd->bhqk', qn, k_nope)
             + jnp.einsum('bqhr,bkr->bhqk', qr, k_rope))
        p = jax.nn.softmax(s.astype(jnp.float32) * scale, axis=-1).astype(v.dtype)
        return jnp.einsum('bhqk,bkhd->bqhd', p, v)

    o = lax.map(attend, (to_blocks(q_nope), to_blocks(q_rope)))
    return o.swapaxes(0, 1).reshape(B, S, H * MLA_V_DIM)


def _gla_chunkwise(q, k, v, log_f):
    B, S, H, Dk = q.shape
    Dv = v.shape[-1]
    n = S // HGRN_CHUNK

    def to_chunks(t):
        return t.reshape(B, n, HGRN_CHUNK, H, t.shape[-1]).transpose(1, 0, 3, 2, 4)

    qc, kc, vc = to_chunks(q), to_chunks(k), to_chunks(v)
    bc = jnp.cumsum(to_chunks(log_f), axis=3)
    mask = jnp.tril(jnp.ones((HGRN_CHUNK, HGRN_CHUNK), dtype=bool))[None, None, :, :, None]

    def step(state, inp):
        q_, k_, v_, b_ = inp
        rel = jnp.where(mask, b_[:, :, :, None, :] - b_[:, :, None, :, :], -jnp.inf)
        a = jnp.einsum('bhtd,bhsd,bhtsd->bhts', q_, k_, jnp.exp(rel))
        o = (jnp.einsum('bhts,bhse->bhte', a, v_)
             + jnp.einsum('bhtd,bhde->bhte', q_ * jnp.exp(b_), state))
        b_last = b_[:, :, -1:, :]
        state = (state * jnp.exp(b_last)[:, :, 0, :, None]
                 + jnp.einsum('bhsd,bhse->bhde', k_ * jnp.exp(b_last - b_), v_))
        return state, o

    state0 = jnp.zeros((B, H, Dk, Dv), jnp.float32)
    _, o = lax.scan(step, state0, (qc, kc, vc, bc))
    return o.transpose(1, 0, 3, 2, 4).reshape(B, S, H, Dv)


def _hgrn2_direction(q, f_logit, v, lb):
    f = lb + (1.0 - lb) * jax.nn.sigmoid(f_logit.astype(jnp.float32))
    return _gla_chunkwise(q.astype(jnp.float32), 1.0 - f, v.astype(jnp.float32), jnp.log(f))


def _lower_bound(lb_logits, layer):
    p = jax.nn.softmax(lb_logits.astype(jnp.float32), axis=0)
    return jnp.cumsum(p, axis=0)[layer].reshape(HGRN_HEADS, HGRN_KEY_DIM)


def _hgrn2_bidir(hq, f_fwd, f_bwd, hi, hg, lb_fwd, lb_bwd, norm_g):
    B, S, _ = hq.shape
    q = hq.reshape(B, S, HGRN_HEADS, HGRN_KEY_DIM)
    ff = f_fwd.reshape(B, S, HGRN_HEADS, HGRN_KEY_DIM)
    fb = f_bwd.reshape(B, S, HGRN_HEADS, HGRN_KEY_DIM)
    v = hi.reshape(B, S, HGRN_HEADS, HGRN_VAL_DIM)
    o_fwd = _hgrn2_direction(q, ff, v, lb_fwd)
    o_bwd = jnp.flip(_hgrn2_direction(jnp.flip(q, 1), jnp.flip(fb, 1), jnp.flip(v, 1), lb_bwd), 1)
    o = _rmsnorm(o_fwd + o_bwd, norm_g).reshape(B, S, HGRN_WIDTH)
    return (o * jax.nn.silu(hg.astype(jnp.float32))).astype(hq.dtype)


def setup_inputs(seed: int = 0) -> dict:
    key = jax.random.key(seed)
    ks = jax.random.split(key, 24)
    f32 = jnp.float32

    def nrm(k, shape, fan_in, s=1.0):
        return jax.random.normal(k, shape, f32) * (s * fan_in ** -0.5)

    def gain(k, shape):
        return 1.0 + 0.05 * jax.random.normal(k, shape, f32)

    offs = jax.random.randint(ks[2], (BATCH, 1), 0, 1024, dtype=jnp.int32)
    positions = (jnp.arange(SEQ, dtype=jnp.int32)[None, :] + offs).astype(jnp.int32)
    return {
        "x": jax.random.normal(ks[0], (BATCH, SEQ, D_MODEL), f32),
        "c": jax.random.normal(ks[1], (BATCH, D_MODEL), f32),
        "positions": positions,
        "w_mod": nrm(ks[3], (DEPTH, D_MODEL, N_MOD * D_MODEL), D_MODEL, 0.5),
        "b_mod": 0.01 * jax.random.normal(ks[4], (DEPTH, N_MOD * D_MODEL), f32),
        "pre_mix_g": gain(ks[5], (DEPTH, D_MODEL)),
        "post_mix_g": gain(ks[6], (DEPTH, D_MODEL)),
        "pre_mlp_g": gain(ks[7], (DEPTH, D_MODEL)),
        "post_mlp_g": gain(ks[8], (DEPTH, D_MODEL)),
        "w_in": nrm(ks[9], (DEPTH, D_MODEL, IN_WIDTH), D_MODEL),
        "q_norm_g": gain(ks[10], (DEPTH, MLA_Q_RANK)),
        "kv_norm_g": gain(ks[11], (DEPTH, MLA_KV_RANK)),
        "w_uq": nrm(ks[12], (DEPTH, MLA_Q_RANK, MLA_HEADS * (MLA_NOPE_DIM + MLA_ROPE_DIM)), MLA_Q_RANK),
        "w_ukv": nrm(ks[13], (DEPTH, MLA_KV_RANK, MLA_HEADS * (MLA_NOPE_DIM + MLA_V_DIM)), MLA_KV_RANK),
        "hgrn_norm_g": gain(ks[14], (DEPTH, HGRN_VAL_DIM)),
        "hgrn_lb_logits_fwd": 0.1 * jax.random.normal(ks[15], (DEPTH + 1, HGRN_QK_WIDTH), f32),
        "hgrn_lb_logits_bwd": 0.1 * jax.random.normal(ks[16], (DEPTH + 1, HGRN_QK_WIDTH), f32),
        "w_out": nrm(ks[17], (DEPTH, MIX_WIDTH, D_MODEL), MIX_WIDTH),
        "w_up": nrm(ks[18], (DEPTH, D_MODEL, D_FF), D_MODEL),
        "w_down": nrm(ks[19], (DEPTH, D_FF, D_MODEL), D_FF),
    }


def reference(x, c, positions, w_mod, b_mod, pre_mix_g, post_mix_g, pre_mlp_g, post_mlp_g,
              w_in, q_norm_g, kv_norm_g, w_uq, w_ukv, hgrn_norm_g, hgrn_lb_logits_fwd,
              hgrn_lb_logits_bwd, w_out, w_up, w_down):
    cos, sin = _rope_angles(positions)
    cond = jax.nn.silu(c)
    offsets = _in_proj_offsets()
    for layer in range(DEPTH):
        mod = (cond @ w_mod[layer] + b_mod[layer]).astype(x.dtype)
        shift_a, scale_a, gate_a, shift_m, scale_m, gate_m = [
            m[:, None, :] for m in jnp.split(mod, N_MOD, axis=-1)]

        h = _rmsnorm(x, pre_mix_g[layer]) * (1.0 + scale_a) + shift_a
        proj = h @ w_in[layer]
        q_lat, kv_lat, k_rope, hq, f_fwd, f_bwd, hi, hg = jnp.split(proj, offsets, axis=-1)
        attn_out = _mla(q_lat, kv_lat, k_rope, cos, sin, q_norm_g[layer], kv_norm_g[layer],
                        w_uq[layer], w_ukv[layer])
        rec_out = _hgrn2_bidir(hq, f_fwd, f_bwd, hi, hg,
                               _lower_bound(hgrn_lb_logits_fwd, layer),
                               _lower_bound(hgrn_lb_logits_bwd, layer), hgrn_norm_g[layer])
        mix = jnp.concatenate([attn_out, rec_out], axis=-1) @ w_out[layer]
        x = x + gate_a * _rmsnorm(mix, post_mix_g[layer])

        h = _rmsnorm(x, pre_mlp_g[layer]) * (1.0 + scale_m) + shift_m
        y = jnp.square(jax.nn.relu(h @ w_up[layer])) @ w_down[layer]
        x = x + gate_m * _rmsnorm(y, post_mlp_g[layer])
    return x
```

```python
import functools

import jax
import jax.numpy as jnp
import numpy as np
from jax import lax
from jax.experimental import pallas as pl
from jax.experimental.pallas import tpu as pltpu

F32 = jnp.float32
BF16 = jnp.bfloat16

NORM_EPS = 1e-6
ROPE_BASE = 10000.0
N_MOD = 6
LANE = 128
VMEM_LIMIT = 56 * 1024 * 1024

MLA_HEADS = 8
MLA_NOPE = 128
MLA_ROPE = 64
MLA_V = 128
MLA_QK_PAD = 256
HGRN_HEADS = 8
HGRN_DK = 128
HGRN_DV = 128
HGRN_CHUNK = 64


def _cparams(*sem):
    return pltpu.CompilerParams(dimension_semantics=sem, vmem_limit_bytes=VMEM_LIMIT)


def _dot(a, b):
    return jnp.dot(a, b, preferred_element_type=F32)


def _dot_nt(a, b):
    return lax.dot_general(a, b, (((1,), (1,)), ((), ())), preferred_element_type=F32)


def _dot_tn(a, b):
    return lax.dot_general(a, b, (((0,), (0,)), ((), ())), preferred_element_type=F32)


def _rms(x, g):
    return x * lax.rsqrt(jnp.mean(x * x, axis=-1, keepdims=True) + NORM_EPS) * g


def _mod_kernel(c_ref, w_ref, b_ref, o_ref):
    c = c_ref[...]
    cond = c * jax.nn.sigmoid(c)
    hi = cond.astype(BF16)
    lo = (cond - hi.astype(F32)).astype(BF16)
    w = w_ref[...].astype(BF16)
    o_ref[...] = _dot(hi, w) + _dot(lo, w) + b_ref[...]


def _mod(c, w_mod, b_mod, tn=1024):
    bsz, d = c.shape
    n = w_mod.shape[1]
    return pl.pallas_call(
        _mod_kernel,
        grid=(n // tn,),
        in_specs=[pl.BlockSpec((bsz, d), lambda j: (0, 0)),
                  pl.BlockSpec((d, tn), lambda j: (0, j)),
                  pl.BlockSpec((1, tn), lambda j: (0, j))],
        out_specs=pl.BlockSpec((bsz, tn), lambda j: (0, j)),
        out_shape=jax.ShapeDtypeStruct((bsz, n), F32),
        compiler_params=_cparams("parallel"),
        name="mod",
    )(c, w_mod, b_mod.reshape(1, n))


def _in_proj_kernel(x_ref, mod_ref, g_ref, w_ref, o_ref, h_ref):
    @pl.when(pl.program_id(2) == 0)
    def _():
        h = _rms(x_ref[...], g_ref[...]) * (1.0 + mod_ref[1:2, :]) + mod_ref[0:1, :]
        h_ref[...] = h.astype(BF16)

    o_ref[...] = _dot(h_ref[...], w_ref[...]).astype(o_ref.dtype)


def _in_proj(x, mod3, g, w, tm=1024, tn=1024):
    bsz, s, d = x.shape
    n = w.shape[1]
    return pl.pallas_call(
        _in_proj_kernel,
        grid=(bsz, s // tm, n // tn),
        in_specs=[pl.BlockSpec((None, tm, d), lambda b, i, j: (b, i, 0)),
                  pl.BlockSpec((None, N_MOD, d), lambda b, i, j: (b, 0, 0)),
                  pl.BlockSpec((1, d), lambda b, i, j: (0, 0)),
                  pl.BlockSpec((d, tn), lambda b, i, j: (0, j))],
        out_specs=pl.BlockSpec((None, tm, tn), lambda b, i, j: (b, i, j)),
        out_shape=jax.ShapeDtypeStruct((bsz, s, n), BF16),
        scratch_shapes=[pltpu.VMEM((tm, d), BF16)],
        compiler_params=_cparams("parallel", "parallel", "arbitrary"),
        name="in_proj",
    )(x, mod3, g, w)


def _rope_tile(t, cos_m, sin_lo, sin_hi):
    half = MLA_ROPE // 2
    return (t * cos_m + pltpu.roll(t, LANE - half, 1) * sin_lo
            + pltpu.roll(t, half, 1) * sin_hi)


def _mla_prep_kernel(p_ref, pos_ref, invf_ref, qg_ref, kvg_ref, wq_ref, wkv_ref,
                     q_ref, k_ref, v_ref, *, q_rank, kv_rank, scale):
    half = MLA_ROPE // 2
    ang = pos_ref[...].astype(F32) * invf_ref[...]
    lane = lax.broadcasted_iota(jnp.int32, ang.shape, 1)
    cos, sin = jnp.cos(ang), jnp.sin(ang)
    cos_m = jnp.where(lane < MLA_ROPE, cos, 0.0)
    sin_lo = jnp.where(lane < half, -sin, 0.0)
    sin_hi = jnp.where((lane >= half) & (lane < MLA_ROPE), sin, 0.0)

    q_lat = p_ref[:, 0:q_rank].astype(F32)
    q = _dot(_rms(q_lat, qg_ref[...]).astype(BF16), wq_ref[...])
    kv_lat = p_ref[:, q_rank:q_rank + kv_rank].astype(F32)
    kv = _dot(_rms(kv_lat, kvg_ref[...]).astype(BF16), wkv_ref[...])
    kr_off = q_rank + kv_rank
    k_rope = _rope_tile(p_ref[:, kr_off:kr_off + LANE].astype(F32),
                        cos_m, sin_lo, sin_hi).astype(BF16)
    for h in range(MLA_HEADS):
        o = h * MLA_QK_PAD
        q_ref[:, o:o + MLA_NOPE] = (q[:, o:o + MLA_NOPE] * scale).astype(BF16)
        qr = _rope_tile(q[:, o + MLA_NOPE:o + MLA_QK_PAD], cos_m, sin_lo, sin_hi)
        q_ref[:, o + MLA_NOPE:o + MLA_QK_PAD] = (qr * scale).astype(BF16)
        k_ref[:, o:o + MLA_NOPE] = kv[:, h * MLA_NOPE:(h + 1) * MLA_NOPE].astype(BF16)
        k_ref[:, o + MLA_NOPE:o + MLA_QK_PAD] = k_rope
    nk = MLA_HEADS * MLA_NOPE
    v_ref[...] = kv[:, nk:].astype(BF16)


def _mla_prep(proj, positions, invf, qg, kvg, wq, wkv, q_rank, kv_rank, tm=512):
    bsz, s, _ = proj.shape
    hq = MLA_HEADS * MLA_QK_PAD
    hv = MLA_HEADS * MLA_V
    scale = float((MLA_NOPE + MLA_ROPE) ** -0.5)
    blk0 = 1024
    kern = functools.partial(_mla_prep_kernel, q_rank=q_rank, kv_rank=kv_rank, scale=scale)
    return pl.pallas_call(
        kern,
        grid=(bsz, s // tm),
        in_specs=[pl.BlockSpec((None, tm, blk0), lambda b, i: (b, i, 0)),
                  pl.BlockSpec((None, tm, 1), lambda b, i: (b, i, 0)),
                  pl.BlockSpec((1, LANE), lambda b, i: (0, 0)),
                  pl.BlockSpec((1, q_rank), lambda b, i: (0, 0)),
                  pl.BlockSpec((1, kv_rank), lambda b, i: (0, 0)),
                  pl.BlockSpec(wq.shape, lambda b, i: (0, 0)),
                  pl.BlockSpec(wkv.shape, lambda b, i: (0, 0))],
        out_specs=[pl.BlockSpec((None, tm, hq), lambda b, i: (b, i, 0)),
                   pl.BlockSpec((None, tm, hq), lambda b, i: (b, i, 0)),
                   pl.BlockSpec((None, tm, hv), lambda b, i: (b, i, 0))],
        out_shape=[jax.ShapeDtypeStruct((bsz, s, hq), BF16),
                   jax.ShapeDtypeStruct((bsz, s, hq), BF16),
                   jax.ShapeDtypeStruct((bsz, s, hv), BF16)],
        compiler_params=_cparams("parallel", "parallel"),
        name="mla_prep",
    )(proj, positions.reshape(bsz, s, 1), invf, qg, kvg, wq, wkv)


def _attn_kernel(q_ref, k_ref, v_ref, o_ref):
    s = _dot_nt(q_ref[...], k_ref[...])
    m = jnp.max(s, axis=-1, keepdims=True)
    p = jnp.exp(s - m)
    l = jnp.sum(p, axis=-1, keepdims=True)
    o = _dot(p.astype(BF16), v_ref[...])
    o_ref[...] = (o / l).astype(o_ref.dtype)


def _attention(q, k, v, tq=256):
    bsz, s, _ = q.shape
    return pl.pallas_call(
        _attn_kernel,
        grid=(bsz, MLA_HEADS, s // tq),
        in_specs=[pl.BlockSpec((None, tq, MLA_QK_PAD), lambda b, h, i: (b, i, h)),
                  pl.BlockSpec((None, s, MLA_QK_PAD), lambda b, h, i: (b, 0, h)),
                  pl.BlockSpec((None, s, MLA_V), lambda b, h, i: (b, 0, h))],
        out_specs=pl.BlockSpec((None, tq, MLA_V), lambda b, h, i: (b, i, h)),
        out_shape=jax.ShapeDtypeStruct((bsz, s, MLA_HEADS * MLA_V), BF16),
        compiler_params=_cparams("parallel", "parallel", "parallel"),
        name="attention",
    )(q, k, v)


def _split_bf16(x):
    hi = x.astype(BF16)
    return hi, (x - hi.astype(F32)).astype(BF16)


def _hgrn_chunk(q, z, v, lb, tri, causal, edge_row, st_ref):
    f = lb + (1.0 - lb) * jax.nn.sigmoid(z)
    k = 1.0 - f
    hi, lo = _split_bf16(jnp.log(f))
    b = _dot(tri, hi) + _dot(tri, lo)
    b_edge = b[edge_row:edge_row + 1, :]
    qs = (q * jnp.exp(b)).astype(BF16)
    ks = (k * jnp.exp(-b)).astype(BF16)
    a = jnp.where(causal, _dot_nt(qs, ks), 0.0).astype(BF16)
    vb = v.astype(BF16)
    st = st_ref[...]
    o = _dot(a, vb) + _dot_nt(qs, st.astype(BF16))
    kd = (k * jnp.exp(b_edge - b)).astype(BF16)
    st_ref[...] = st * jnp.exp(b_edge) + _dot_tn(vb, kd)
    return o


def _hgrn_kernel(qf_ref, zf_ref, vf_ref, qb_ref, zb_ref, vb_ref, lbf_ref, lbb_ref,
                 of_ref, ob_ref, stf_ref, stb_ref, *, n_chunks):
    @pl.when(pl.program_id(2) == 0)
    def _():
        stf_ref[...] = jnp.zeros_like(stf_ref)
        stb_ref[...] = jnp.zeros_like(stb_ref)

    def lower_bound(ref):
        l = ref[...]
        e = jnp.exp(l - jnp.max(l, axis=0, keepdims=True))
        return e[0:1, :] / jnp.sum(e, axis=0, keepdims=True)

    lbf, lbb = lower_bound(lbf_ref), lower_bound(lbb_ref)
    c = HGRN_CHUNK
    row = lax.broadcasted_iota(jnp.int32, (c, c), 0)
    col = lax.broadcasted_iota(jnp.int32, (c, c), 1)
    lower = col <= row
    upper = col >= row
    tri_lo = lower.astype(BF16)
    tri_up = upper.astype(BF16)
    for j in range(n_chunks):
        r = slice(j * c, (j + 1) * c)
        of_ref[r, :] = _hgrn_chunk(qf_ref[r, :].astype(F32), zf_ref[r, :].astype(F32),
                                   vf_ref[r, :], lbf, tri_lo, lower, c - 1,
                                   stf_ref).astype(of_ref.dtype)
        jb = n_chunks - 1 - j
        rb = slice(jb * c, (jb + 1) * c)
        ob_ref[rb, :] = _hgrn_chunk(qb_ref[rb, :].astype(F32), zb_ref[rb, :].astype(F32),
                                    vb_ref[rb, :], lbb, tri_up, upper, 0,
                                    stb_ref).astype(ob_ref.dtype)


def _hgrn(proj, lbf_logits, lbb_logits, col_q, col_ff, col_fb, col_v, tb=512):
    bsz, s, _ = proj.shape
    nb = s // tb
    width = HGRN_HEADS * HGRN_DV

    def fwd(col):
        return pl.BlockSpec((None, tb, LANE), lambda b, h, i: (b, i, col + h))

    def bwd(col):
        return pl.BlockSpec((None, tb, LANE), lambda b, h, i: (b, nb - 1 - i, col + h))

    lb_spec = pl.BlockSpec((lbf_logits.shape[0], LANE), lambda b, h, i: (0, h))
    kern = functools.partial(_hgrn_kernel, n_chunks=tb // HGRN_CHUNK)
    return pl.pallas_call(
        kern,
        grid=(bsz, HGRN_HEADS, nb),
        in_specs=[fwd(col_q), fwd(col_ff), fwd(col_v),
                  bwd(col_q), bwd(col_fb), bwd(col_v), lb_spec, lb_spec],
        out_specs=[pl.BlockSpec((None, tb, LANE), lambda b, h, i: (b, i, h)),
                   pl.BlockSpec((None, tb, LANE), lambda b, h, i: (b, nb - 1 - i, h))],
        out_shape=[jax.ShapeDtypeStruct((bsz, s, width), F32),
                   jax.ShapeDtypeStruct((bsz, s, width), F32)],
        scratch_shapes=[pltpu.VMEM((HGRN_DV, HGRN_DK), F32),
                        pltpu.VMEM((HGRN_DV, HGRN_DK), F32)],
        compiler_params=_cparams("parallel", "parallel", "arbitrary"),
        name="hgrn",
    )(proj, proj, proj, proj, proj, proj, lbf_logits, lbb_logits)


def _mix_out_kernel(attn_ref, of_ref, ob_ref, hg_ref, hgn_ref, w_ref, x_ref, mod_ref,
                    postg_ref, preg_ref, x1_ref, h2_ref):
    o = of_ref[...] + ob_ref[...]
    hg = hg_ref[...].astype(F32)
    gate = hg * jax.nn.sigmoid(hg)
    rec = []
    for h in range(HGRN_HEADS):
        sl = slice(h * HGRN_DV, (h + 1) * HGRN_DV)
        rec.append((_rms(o[:, sl], hgn_ref[...]) * gate[:, sl]).astype(BF16))
    rec = jnp.concatenate(rec, axis=-1)
    na = attn_ref.shape[-1]
    mix = _dot(attn_ref[...], w_ref[0:na, :]) + _dot(rec, w_ref[na:, :])
    x1 = x_ref[...] + mod_ref[2:3, :] * _rms(mix, postg_ref[...])
    x1_ref[...] = x1
    h2 = _rms(x1, preg_ref[...]) * (1.0 + mod_ref[4:5, :]) + mod_ref[3:4, :]
    h2_ref[...] = h2.astype(BF16)


def _mix_out(attn, o_f, o_b, proj, col_hg, hgn_g, w_out, x, mod3, post_g, pre_g, tm=512):
    bsz, s, d = x.shape
    na = attn.shape[-1]
    nr = o_f.shape[-1]

    def row(width):
        return pl.BlockSpec((None, tm, width), lambda b, i: (b, i, 0))

    def const(shape):
        return pl.BlockSpec(shape, lambda b, i: (0,) * len(shape))

    return pl.pallas_call(
        _mix_out_kernel,
        grid=(bsz, s // tm),
        in_specs=[row(na), row(nr), row(nr),
                  pl.BlockSpec((None, tm, nr), lambda b, i: (b, i, col_hg)),
                  const((1, HGRN_DV)), const(w_out.shape), row(d),
                  pl.BlockSpec((None, N_MOD, d), lambda b, i: (b, 0, 0)),
                  const((1, d)), const((1, d))],
        out_specs=[row(d), row(d)],
        out_shape=[jax.ShapeDtypeStruct((bsz, s, d), F32),
                   jax.ShapeDtypeStruct((bsz, s, d), BF16)],
        compiler_params=_cparams("parallel", "parallel"),
        name="mix_out",
    )(attn, o_f, o_b, proj, hgn_g, w_out, x, mod3, post_g, pre_g)


def _mlp_kernel(h_ref, wu_ref, wd_ref, x1_ref, mod_ref, g_ref, o_ref):
    j = pl.program_id(2)
    u = jnp.maximum(_dot(h_ref[...], wu_ref[...]), 0.0)
    y = _dot((u * u).astype(BF16), wd_ref[...])

    @pl.when(j == 0)
    def _():
        o_ref[...] = y

    @pl.when(j > 0)
    def _():
        o_ref[...] += y

    @pl.when(j == pl.num_programs(2) - 1)
    def _():
        o_ref[...] = x1_ref[...] + mod_ref[5:6, :] * _rms(o_ref[...], g_ref[...])


def _mlp(h2, w_up, w_down, x1, mod3, g, tm=512, tf=1024):
    bsz, s, d = x1.shape
    dff = w_up.shape[1]
    return pl.pallas_call(
        _mlp_kernel,
        grid=(bsz, s // tm, dff // tf),
        in_specs=[pl.BlockSpec((None, tm, d), lambda b, i, j: (b, i, 0)),
                  pl.BlockSpec((d, tf), lambda b, i, j: (0, j)),
                  pl.BlockSpec((tf, d), lambda b, i, j: (j, 0)),
                  pl.BlockSpec((None, tm, d), lambda b, i, j: (b, i, 0)),
                  pl.BlockSpec((None, N_MOD, d), lambda b, i, j: (b, 0, 0)),
                  pl.BlockSpec((1, d), lambda b, i, j: (0, 0))],
        out_specs=pl.BlockSpec((None, tm, d), lambda b, i, j: (b, i, 0)),
        out_shape=jax.ShapeDtypeStruct((bsz, s, d), F32),
        compiler_params=_cparams("parallel", "parallel", "arbitrary"),
        name="mlp",
    )(h2, w_up, w_down, x1, mod3, g)


def _pack_w_in(w, q_rank, kv_rank):
    head = q_rank + kv_rank + MLA_ROPE
    pad = 1024 - head
    d = w.shape[0]
    return jnp.concatenate([w[:, :head], jnp.zeros((d, pad), w.dtype), w[:, head:]],
                           axis=1).astype(BF16)


def _pack_w_uq(w):
    r = w.shape[0]
    w = w.reshape(r, MLA_HEADS, MLA_NOPE + MLA_ROPE)
    pad = jnp.zeros((r, MLA_HEADS, MLA_QK_PAD - MLA_NOPE - MLA_ROPE), w.dtype)
    return jnp.concatenate([w, pad], axis=-1).reshape(r, MLA_HEADS * MLA_QK_PAD).astype(BF16)


def _pack_w_ukv(w):
    r = w.shape[0]
    w = w.reshape(r, MLA_HEADS, MLA_NOPE + MLA_V)
    return jnp.concatenate([w[:, :, :MLA_NOPE].reshape(r, -1),
                            w[:, :, MLA_NOPE:].reshape(r, -1)], axis=1).astype(BF16)


def _rope_inv_freq():
    half = MLA_ROPE // 2
    f = ROPE_BASE ** (-jnp.arange(half, dtype=F32) / half)
    return jnp.concatenate([f, f, jnp.zeros((LANE - MLA_ROPE,), F32)]).reshape(1, LANE)


def kernel(x, c, positions, w_mod, b_mod, pre_mix_g, post_mix_g, pre_mlp_g, post_mlp_g,
           w_in, q_norm_g, kv_norm_g, w_uq, w_ukv, hgrn_norm_g, hgrn_lb_logits_fwd,
           hgrn_lb_logits_bwd, w_out, w_up, w_down):
    depth = w_mod.shape[0]
    assert depth == 1, "lower-bound slot selection is written for a single layer"
    bsz, s, d = x.shape
    q_rank = q_norm_g.shape[-1]
    kv_rank = kv_norm_g.shape[-1]
    invf = _rope_inv_freq()
    for layer in range(depth):
        mod3 = _mod(c, w_mod[layer], b_mod[layer]).reshape(bsz, N_MOD, d)
        proj = _in_proj(x, mod3, pre_mix_g[layer].reshape(1, d),
                        _pack_w_in(w_in[layer], q_rank, kv_rank))
        q, k, v = _mla_prep(proj, positions, invf, q_norm_g[layer].reshape(1, -1),
                            kv_norm_g[layer].reshape(1, -1), _pack_w_uq(w_uq[layer]),
                            _pack_w_ukv(w_ukv[layer]), q_rank, kv_rank)
        attn = _attention(q, k, v)
        o_f, o_b = _hgrn(proj, hgrn_lb_logits_fwd, hgrn_lb_logits_bwd,
                         col_q=8, col_ff=16, col_fb=24, col_v=32)
        x1, h2 = _mix_out(attn, o_f, o_b, proj, 5, hgrn_norm_g[layer].reshape(1, -1),
                          w_out[layer].astype(BF16), x, mod3,
                          post_mix_g[layer].reshape(1, d), pre_mlp_g[layer].reshape(1, d))
        x = _mlp(h2, w_up[layer].astype(BF16), w_down[layer].astype(BF16), x1, mod3,
                 post_mlp_g[layer].reshape(1, d))
    return x
```

```python
import functools

import jax
import jax.numpy as jnp
import numpy as np
from jax import lax
from jax.experimental import pallas as pl
from jax.experimental.pallas import tpu as pltpu

F32 = jnp.float32
BF16 = jnp.bfloat16

NORM_EPS = 1e-6
ROPE_BASE = 10000.0
N_MOD = 6
LANE = 128
VMEM_LIMIT = 56 * 1024 * 1024

MLA_HEADS = 8
MLA_NOPE = 128
MLA_ROPE = 64
MLA_V = 128
MLA_QK_PAD = 256
HGRN_HEADS = 8
HGRN_DK = 128
HGRN_DV = 128
HGRN_CHUNK = 64


def _cparams(*sem):
    return pltpu.CompilerParams(dimension_semantics=sem, vmem_limit_bytes=VMEM_LIMIT)


def _dot(a, b):
    return jnp.dot(a, b, preferred_element_type=F32)


def _dot_nt(a, b):
    return lax.dot_general(a, b, (((1,), (1,)), ((), ())), preferred_element_type=F32)


def _dot_tn(a, b):
    return lax.dot_general(a, b, (((0,), (0,)), ((), ())), preferred_element_type=F32)


def _rms(x, g):
    return x * lax.rsqrt(jnp.mean(x * x, axis=-1, keepdims=True) + NORM_EPS) * g


def _mod_kernel(c_ref, w_ref, b_ref, o_ref):
    c = c_ref[...]
    cond = c * jax.nn.sigmoid(c)
    hi = cond.astype(BF16)
    lo = (cond - hi.astype(F32)).astype(BF16)
    w = w_ref[...].astype(BF16)
    o_ref[...] = _dot(hi, w) + _dot(lo, w) + b_ref[...]


def _mod(c, w_mod, b_mod, tn=1024):
    bsz, d = c.shape
    n = w_mod.shape[1]
    return pl.pallas_call(
        _mod_kernel,
        grid=(n // tn,),
        in_specs=[pl.BlockSpec((bsz, d), lambda j: (0, 0)),
                  pl.BlockSpec((d, tn), lambda j: (0, j)),
                  pl.BlockSpec((1, tn), lambda j: (0, j))],
        out_specs=pl.BlockSpec((bsz, tn), lambda j: (0, j)),
        out_shape=jax.ShapeDtypeStruct((bsz, n), F32),
        compiler_params=_cparams("parallel"),
        name="mod",
    )(c, w_mod, b_mod.reshape(1, n))


def _in_proj_kernel(x_ref, mod_ref, g_ref, w_ref, o_ref, h_ref):
    @pl.when(pl.program_id(2) == 0)
    def _():
        h = _rms(x_ref[...], g_ref[...]) * (1.0 + mod_ref[1:2, :]) + mod_ref[0:1, :]
        h_ref[...] = h.astype(BF16)

    o_ref[...] = _dot(h_ref[...], w_ref[...]).astype(o_ref.dtype)


def _in_proj(x, mod3, g, w, tm=1024, tn=1024):
    bsz, s, d = x.shape
    n = w.shape[1]
    return pl.pallas_call(
        _in_proj_kernel,
        grid=(bsz, s // tm, n // tn),
        in_specs=[pl.BlockSpec((None, tm, d), lambda b, i, j: (b, i, 0)),
                  pl.BlockSpec((None, N_MOD, d), lambda b, i, j: (b, 0, 0)),
                  pl.BlockSpec((1, d), lambda b, i, j: (0, 0)),
                  pl.BlockSpec((d, tn), lambda b, i, j: (0, j))],
        out_specs=pl.BlockSpec((None, tm, tn), lambda b, i, j: (b, i, j)),
        out_shape=jax.ShapeDtypeStruct((bsz, s, n), BF16),
        scratch_shapes=[pltpu.VMEM((tm, d), BF16)],
        compiler_params=_cparams("parallel", "parallel", "arbitrary"),
        name="in_proj",
    )(x, mod3, g, w)


def _rope_tile(t, cos_m, sin_lo, sin_hi):
    half = MLA_ROPE // 2
    return (t * cos_m + pltpu.roll(t, LANE - half, 1) * sin_lo
            + pltpu.roll(t, half, 1) * sin_hi)


def _mla_prep_kernel(p_ref, pos_ref, invf_ref, qg_ref, kvg_ref, wq_ref, wk_ref, wvt_ref,
                     q_ref, k_ref, vt_ref, *, q_rank, kv_rank, scale):
    half = MLA_ROPE // 2
    ang = pos_ref[...].astype(F32) * invf_ref[...]
    lane = lax.broadcasted_iota(jnp.int32, ang.shape, 1)
    cos, sin = jnp.cos(ang), jnp.sin(ang)
    cos_m = jnp.where(lane < MLA_ROPE, cos, 0.0)
    sin_lo = jnp.where(lane < half, -sin, 0.0)
    sin_hi = jnp.where((lane >= half) & (lane < MLA_ROPE), sin, 0.0)

    q_lat = p_ref[:, 0:q_rank].astype(F32)
    q = _dot(_rms(q_lat, qg_ref[...]).astype(BF16), wq_ref[...])
    kv_lat = p_ref[:, q_rank:q_rank + kv_rank].astype(F32)
    kv_n = _rms(kv_lat, kvg_ref[...]).astype(BF16)
    k_nope = _dot(kv_n, wk_ref[...])
    vt_ref[...] = _dot_nt(wvt_ref[...], kv_n).astype(BF16)
    kr_off = q_rank + kv_rank
    k_rope = _rope_tile(p_ref[:, kr_off:kr_off + LANE].astype(F32),
                        cos_m, sin_lo, sin_hi).astype(BF16)
    for h in range(MLA_HEADS):
        o = h * MLA_QK_PAD
        q_ref[:, o:o + MLA_NOPE] = (q[:, o:o + MLA_NOPE] * scale).astype(BF16)
        qr = _rope_tile(q[:, o + MLA_NOPE:o + MLA_QK_PAD], cos_m, sin_lo, sin_hi)
        q_ref[:, o + MLA_NOPE:o + MLA_QK_PAD] = (qr * scale).astype(BF16)
        k_ref[:, o:o + MLA_NOPE] = k_nope[:, h * MLA_NOPE:(h + 1) * MLA_NOPE].astype(BF16)
        k_ref[:, o + MLA_NOPE:o + MLA_QK_PAD] = k_rope


def _mla_prep(proj, positions, invf, qg, kvg, wq, wk, wvt, q_rank, kv_rank, tm=512):
    bsz, s, _ = proj.shape
    hq = MLA_HEADS * MLA_QK_PAD
    hv = MLA_HEADS * MLA_V
    scale = float((MLA_NOPE + MLA_ROPE) ** -0.5 * np.log2(np.e))
    blk0 = 1024
    kern = functools.partial(_mla_prep_kernel, q_rank=q_rank, kv_rank=kv_rank, scale=scale)

    def const(a):
        return pl.BlockSpec(a.shape, lambda b, i: (0, 0))

    return pl.pallas_call(
        kern,
        grid=(bsz, s // tm),
        in_specs=[pl.BlockSpec((None, tm, blk0), lambda b, i: (b, i, 0)),
                  pl.BlockSpec((None, tm, 1), lambda b, i: (b, i, 0)),
                  const(invf), const(qg), const(kvg), const(wq), const(wk), const(wvt)],
        out_specs=[pl.BlockSpec((None, tm, hq), lambda b, i: (b, i, 0)),
                   pl.BlockSpec((None, tm, hq), lambda b, i: (b, i, 0)),
                   pl.BlockSpec((None, hv, tm), lambda b, i: (b, 0, i))],
        out_shape=[jax.ShapeDtypeStruct((bsz, s, hq), BF16),
                   jax.ShapeDtypeStruct((bsz, s, hq), BF16),
                   jax.ShapeDtypeStruct((bsz, hv, s), BF16)],
        compiler_params=_cparams("parallel", "parallel"),
        name="mla_prep",
    )(proj, positions.reshape(bsz, s, 1), invf, qg, kvg, wq, wk, wvt)


def _attn_kernel(q_ref, k_ref, vt_ref, o_ref, sa_ref, sb_ref, *, tq, tk):
    s_len = k_ref.shape[0]
    n_kv = s_len // tk
    n_q = s_len // tq
    sub = 8

    def scores(i, s_ref):
        q = q_ref[pl.ds(pl.multiple_of(i * tq, tq), tq), :]
        m8 = jnp.full((sub, tq), -jnp.inf, F32)
        for j in range(n_kv):
            st = _dot_nt(k_ref[j * tk:(j + 1) * tk, :], q)
            s_ref[j * tk:(j + 1) * tk, :] = st
            m8 = jnp.maximum(m8, jnp.max(st.reshape(tk // sub, sub, tq), axis=0))
        return jnp.broadcast_to(jnp.max(m8, axis=0, keepdims=True), (sub, tq))

    def values(i, s_ref, m8):
        l8 = jnp.zeros((sub, tq), F32)
        acc = jnp.zeros((MLA_V, tq), F32)
        for j in range(n_kv):
            st = s_ref[j * tk:(j + 1) * tk, :].reshape(tk // sub, sub, tq)
            p = jnp.exp2(st - m8[None])
            l8 = l8 + jnp.sum(p, axis=0)
            acc = acc + _dot(vt_ref[:, j * tk:(j + 1) * tk],
                             p.reshape(tk, tq).astype(BF16))
        l = jnp.sum(l8, axis=0, keepdims=True)
        o_ref[pl.ds(pl.multiple_of(i * tq, tq), tq), :] = (acc / l).T.astype(o_ref.dtype)

    def pair(k, m_a):
        m_b = scores(2 * k + 1, sb_ref)
        values(2 * k, sa_ref, m_a)
        m_a = scores(2 * k + 2, sa_ref)
        values(2 * k + 1, sb_ref, m_b)
        return m_a

    m_a = lax.fori_loop(0, n_q // 2 - 1, pair, scores(0, sa_ref))
    m_b = scores(n_q - 1, sb_ref)
    values(n_q - 2, sa_ref, m_a)
    values(n_q - 1, sb_ref, m_b)


def _attention(q, k, vt, tq=256, tk=512):
    bsz, s, _ = q.shape
    kern = functools.partial(_attn_kernel, tq=tq, tk=tk)
    return pl.pallas_call(
        kern,
        grid=(bsz, MLA_HEADS),
        in_specs=[pl.BlockSpec((None, s, MLA_QK_PAD), lambda b, h: (b, 0, h)),
                  pl.BlockSpec((None, s, MLA_QK_PAD), lambda b, h: (b, 0, h)),
                  pl.BlockSpec((None, MLA_V, s), lambda b, h: (b, h, 0))],
        out_specs=pl.BlockSpec((None, s, MLA_V), lambda b, h: (b, 0, h)),
        out_shape=jax.ShapeDtypeStruct((bsz, s, MLA_HEADS * MLA_V), BF16),
        scratch_shapes=[pltpu.VMEM((s, tq), F32), pltpu.VMEM((s, tq), F32)],
        compiler_params=_cparams("parallel", "parallel"),
        name="attention",
    )(q, k, vt)


def _split_bf16(x):
    hi = x.astype(BF16)
    return hi, (x - hi.astype(F32)).astype(BF16)


def _hgrn_chunk(q, z, v, lb, tri, causal, edge_row, st_ref):
    f = lb + (1.0 - lb) * jax.nn.sigmoid(z)
    k = 1.0 - f
    hi, lo = _split_bf16(jnp.log(f))
    b = _dot(tri, hi) + _dot(tri, lo)
    b_edge = b[edge_row:edge_row + 1, :]
    qs = (q * jnp.exp(b)).astype(BF16)
    ks = (k * jnp.exp(-b)).astype(BF16)
    a = jnp.where(causal, _dot_nt(qs, ks), 0.0).astype(BF16)
    vb = v.astype(BF16)
    st = st_ref[...]
    o = _dot(a, vb) + _dot_nt(qs, st.astype(BF16))
    kd = (k * jnp.exp(b_edge - b)).astype(BF16)
    st_ref[...] = st * jnp.exp(b_edge) + _dot_tn(vb, kd)
    return o


def _hgrn_kernel(qf_ref, zf_ref, vf_ref, qb_ref, zb_ref, vb_ref, lbf_ref, lbb_ref,
                 of_ref, ob_ref, stf_ref, stb_ref, *, n_chunks):
    @pl.when(pl.program_id(2) == 0)
    def _():
        stf_ref[...] = jnp.zeros_like(stf_ref)
        stb_ref[...] = jnp.zeros_like(stb_ref)

    def lower_bound(ref):
        l = ref[...]
        e = jnp.exp(l - jnp.max(l, axis=0, keepdims=True))
        return e[0:1, :] / jnp.sum(e, axis=0, keepdims=True)

    lbf, lbb = lower_bound(lbf_ref), lower_bound(lbb_ref)
    c = HGRN_CHUNK
    row = lax.broadcasted_iota(jnp.int32, (c, c), 0)
    col = lax.broadcasted_iota(jnp.int32, (c, c), 1)
    lower = col <= row
    upper = col >= row
    tri_lo = lower.astype(BF16)
    tri_up = upper.astype(BF16)
    for j in range(n_chunks):
        r = slice(j * c, (j + 1) * c)
        of_ref[r, :] = _hgrn_chunk(qf_ref[r, :].astype(F32), zf_ref[r, :].astype(F32),
                                   vf_ref[r, :], lbf, tri_lo, lower, c - 1,
                                   stf_ref).astype(of_ref.dtype)
        jb = n_chunks - 1 - j
        rb = slice(jb * c, (jb + 1) * c)
        ob_ref[rb, :] = _hgrn_chunk(qb_ref[rb, :].astype(F32), zb_ref[rb, :].astype(F32),
                                    vb_ref[rb, :], lbb, tri_up, upper, 0,
                                    stb_ref).astype(ob_ref.dtype)


def _hgrn(proj, lbf_logits, lbb_logits, col_q, col_ff, col_fb, col_v, tb=512):
    bsz, s, _ = proj.shape
    nb = s // tb
    width = HGRN_HEADS * HGRN_DV

    def fwd(col):
        return pl.BlockSpec((None, tb, LANE), lambda b, h, i: (b, i, col + h))

    def bwd(col):
        return pl.BlockSpec((None, tb, LANE), lambda b, h, i: (b, nb - 1 - i, col + h))

    lb_spec = pl.BlockSpec((lbf_logits.shape[0], LANE), lambda b, h, i: (0, h))
    kern = functools.partial(_hgrn_kernel, n_chunks=tb // HGRN_CHUNK)
    return pl.pallas_call(
        kern,
        grid=(bsz, HGRN_HEADS, nb),
        in_specs=[fwd(col_q), fwd(col_ff), fwd(col_v),
                  bwd(col_q), bwd(col_fb), bwd(col_v), lb_spec, lb_spec],
        out_specs=[pl.BlockSpec((None, tb, LANE), lambda b, h, i: (b, i, h)),
                   pl.BlockSpec((None, tb, LANE), lambda b, h, i: (b, nb - 1 - i, h))],
        out_shape=[jax.ShapeDtypeStruct((bsz, s, width), F32),
                   jax.ShapeDtypeStruct((bsz, s, width), F32)],
        scratch_shapes=[pltpu.VMEM((HGRN_DV, HGRN_DK), F32),
                        pltpu.VMEM((HGRN_DV, HGRN_DK), F32)],
        compiler_params=_cparams("parallel", "parallel", "arbitrary"),
        name="hgrn",
    )(proj, proj, proj, proj, proj, proj, lbf_logits, lbb_logits)


def _mix_out_kernel(attn_ref, of_ref, ob_ref, hg_ref, hgn_ref, w_ref, x_ref, mod_ref,
                    postg_ref, preg_ref, x1_ref, h2_ref):
    o = of_ref[...] + ob_ref[...]
    hg = hg_ref[...].astype(F32)
    gate = hg * jax.nn.sigmoid(hg)
    rec = []
    for h in range(HGRN_HEADS):
        sl = slice(h * HGRN_DV, (h + 1) * HGRN_DV)
        rec.append((_rms(o[:, sl], hgn_ref[...]) * gate[:, sl]).astype(BF16))
    rec = jnp.concatenate(rec, axis=-1)
    na = attn_ref.shape[-1]
    mix = _dot(attn_ref[...], w_ref[0:na, :]) + _dot(rec, w_ref[na:, :])
    x1 = x_ref[...] + mod_ref[2:3, :] * _rms(mix, postg_ref[...])
    x1_ref[...] = x1
    h2 = _rms(x1, preg_ref[...]) * (1.0 + mod_ref[4:5, :]) + mod_ref[3:4, :]
    h2_ref[...] = h2.astype(BF16)


def _mix_out(attn, o_f, o_b, proj, col_hg, hgn_g, w_out, x, mod3, post_g, pre_g, tm=512):
    bsz, s, d = x.shape
    na = attn.shape[-1]
    nr = o_f.shape[-1]

    def row(width):
        return pl.BlockSpec((None, tm, width), lambda b, i: (b, i, 0))

    def const(shape):
        return pl.BlockSpec(shape, lambda b, i: (0,) * len(shape))

    return pl.pallas_call(
        _mix_out_kernel,
        grid=(bsz, s // tm),
        in_specs=[row(na), row(nr), row(nr),
                  pl.BlockSpec((None, tm, nr), lambda b, i: (b, i, col_hg)),
                  const((1, HGRN_DV)), const(w_out.shape), row(d),
                  pl.BlockSpec((None, N_MOD, d), lambda b, i: (b, 0, 0)),
                  const((1, d)), const((1, d))],
        out_specs=[row(d), row(d)],
        out_shape=[jax.ShapeDtypeStruct((bsz, s, d), F32),
                   jax.ShapeDtypeStruct((bsz, s, d), BF16)],
        compiler_params=_cparams("parallel", "parallel"),
        name="mix_out",
    )(attn, o_f, o_b, proj, hgn_g, w_out, x, mod3, post_g, pre_g)


def _mlp_kernel(h_ref, wu_ref, wd_ref, x1_ref, mod_ref, g_ref, o_ref):
    j = pl.program_id(2)

    @pl.when(j == 0)
    def _():
        o_ref[...] = jnp.zeros_like(o_ref)

    u = jnp.maximum(_dot(h_ref[...], wu_ref[...]), 0.0)
    o_ref[...] += _dot((u * u).astype(BF16), wd_ref[...])

    @pl.when(j == pl.num_programs(2) - 1)
    def _():
        o_ref[...] = x1_ref[...] + mod_ref[5:6, :] * _rms(o_ref[...], g_ref[...])


def _mlp(h2, w_up, w_down, x1, mod3, g, tm=512, tf=1024):
    bsz, s, d = x1.shape
    dff = w_up.shape[1]
    return pl.pallas_call(
        _mlp_kernel,
        grid=(bsz, s // tm, dff // tf),
        in_specs=[pl.BlockSpec((None, tm, d), lambda b, i, j: (b, i, 0)),
                  pl.BlockSpec((d, tf), lambda b, i, j: (0, j)),
                  pl.BlockSpec((tf, d), lambda b, i, j: (j, 0)),
                  pl.BlockSpec((None, tm, d), lambda b, i, j: (b, i, 0)),
                  pl.BlockSpec((None, N_MOD, d), lambda b, i, j: (b, 0, 0)),
                  pl.BlockSpec((1, d), lambda b, i, j: (0, 0))],
        out_specs=pl.BlockSpec((None, tm, d), lambda b, i, j: (b, i, 0)),
        out_shape=jax.ShapeDtypeStruct((bsz, s, d), F32),
        compiler_params=_cparams("parallel", "parallel", "arbitrary"),
        name="mlp",
    )(h2, w_up, w_down, x1, mod3, g)


def _pack_w_in(w, q_rank, kv_rank):
    head = q_rank + kv_rank + MLA_ROPE
    pad = 1024 - head
    d = w.shape[0]
    return jnp.concatenate([w[:, :head], jnp.zeros((d, pad), w.dtype), w[:, head:]],
                           axis=1).astype(BF16)


def _pack_w_uq(w):
    r = w.shape[0]
    w = w.reshape(r, MLA_HEADS, MLA_NOPE + MLA_ROPE)
    pad = jnp.zeros((r, MLA_HEADS, MLA_QK_PAD - MLA_NOPE - MLA_ROPE), w.dtype)
    return jnp.concatenate([w, pad], axis=-1).reshape(r, MLA_HEADS * MLA_QK_PAD).astype(BF16)


def _pack_w_ukv(w):
    r = w.shape[0]
    w = w.reshape(r, MLA_HEADS, MLA_NOPE + MLA_V)
    wk = w[:, :, :MLA_NOPE].reshape(r, -1).astype(BF16)
    wvt = w[:, :, MLA_NOPE:].reshape(r, -1).T.astype(BF16)
    return wk, wvt


def _rope_inv_freq():
    half = MLA_ROPE // 2
    f = ROPE_BASE ** (-jnp.arange(half, dtype=F32) / half)
    return jnp.concatenate([f, f, jnp.zeros((LANE - MLA_ROPE,), F32)]).reshape(1, LANE)


def kernel(x, c, positions, w_mod, b_mod, pre_mix_g, post_mix_g, pre_mlp_g, post_mlp_g,
           w_in, q_norm_g, kv_norm_g, w_uq, w_ukv, hgrn_norm_g, hgrn_lb_logits_fwd,
           hgrn_lb_logits_bwd, w_out, w_up, w_down):
    depth = w_mod.shape[0]
    assert depth == 1, "lower-bound slot selection is written for a single layer"
    bsz, s, d = x.shape
    q_rank = q_norm_g.shape[-1]
    kv_rank = kv_norm_g.shape[-1]
    invf = _rope_inv_freq()
    for layer in range(depth):
        mod3 = _mod(c, w_mod[layer], b_mod[layer]).reshape(bsz, N_MOD, d)
        proj = _in_proj(x, mod3, pre_mix_g[layer].reshape(1, d),
                        _pack_w_in(w_in[layer], q_rank, kv_rank))
        wk, wvt = _pack_w_ukv(w_ukv[layer])
        q, k, vt = _mla_prep(proj, positions, invf, q_norm_g[layer].reshape(1, -1),
                             kv_norm_g[layer].reshape(1, -1), _pack_w_uq(w_uq[layer]),
                             wk, wvt, q_rank, kv_rank)
        attn = _attention(q, k, vt)
        o_f, o_b = _hgrn(proj, hgrn_lb_logits_fwd, hgrn_lb_logits_bwd,
                         col_q=8, col_ff=16, col_fb=24, col_v=32)
        x1, h2 = _mix_out(attn, o_f, o_b, proj, 5, hgrn_norm_g[layer].reshape(1, -1),
                          w_out[layer].astype(BF16), x, mod3,
                          post_mix_g[layer].reshape(1, d), pre_mlp_g[layer].reshape(1, d))
        x = _mlp(h2, w_up[layer].astype(BF16), w_down[layer].astype(BF16), x1, mod3,
                 post_mlp_g[layer].reshape(1, d))
    return x
```

```python
import functools

import jax
import jax.numpy as jnp
import numpy as np
from jax import lax
from jax.experimental import pallas as pl
from jax.experimental.pallas import tpu as pltpu

F32 = jnp.float32
BF16 = jnp.bfloat16

NORM_EPS = 1e-6
ROPE_BASE = 10000.0
N_MOD = 6
LANE = 128
VMEM_LIMIT = 56 * 1024 * 1024

MLA_HEADS = 8
MLA_NOPE = 128
MLA_ROPE = 64
MLA_V = 128
MLA_QK_PAD = 256
HGRN_HEADS = 8
HGRN_DK = 128
HGRN_DV = 128
HGRN_CHUNK = 64


def _cparams(*sem):
    return pltpu.CompilerParams(dimension_semantics=sem, vmem_limit_bytes=VMEM_LIMIT)


def _dot(a, b):
    return jnp.dot(a, b, preferred_element_type=F32)


def _dot_nt(a, b):
    return lax.dot_general(a, b, (((1,), (1,)), ((), ())), preferred_element_type=F32)


def _dot_tn(a, b):
    return lax.dot_general(a, b, (((0,), (0,)), ((), ())), preferred_element_type=F32)


def _rms(x, g):
    return x * lax.rsqrt(jnp.mean(x * x, axis=-1, keepdims=True) + NORM_EPS) * g


def _mod_kernel(c_ref, w_ref, b_ref, o_ref):
    c = c_ref[...]
    cond = c * jax.nn.sigmoid(c)
    hi = cond.astype(BF16)
    lo = (cond - hi.astype(F32)).astype(BF16)
    w = w_ref[...].astype(BF16)
    o_ref[...] = _dot(hi, w) + _dot(lo, w) + b_ref[...]


def _mod(c, w_mod, b_mod, tn=1024):
    bsz, d = c.shape
    n = w_mod.shape[1]
    return pl.pallas_call(
        _mod_kernel,
        grid=(n // tn,),
        in_specs=[pl.BlockSpec((bsz, d), lambda j: (0, 0)),
                  pl.BlockSpec((d, tn), lambda j: (0, j)),
                  pl.BlockSpec((1, tn), lambda j: (0, j))],
        out_specs=pl.BlockSpec((bsz, tn), lambda j: (0, j)),
        out_shape=jax.ShapeDtypeStruct((bsz, n), F32),
        compiler_params=_cparams("parallel"),
        name="mod",
    )(c, w_mod, b_mod.reshape(1, n))


def _in_proj_kernel(x_ref, mod_ref, g_ref, w_ref, o_ref, h_ref):
    @pl.when(pl.program_id(2) == 0)
    def _():
        h = _rms(x_ref[...], g_ref[...]) * (1.0 + mod_ref[1:2, :]) + mod_ref[0:1, :]
        h_ref[...] = h.astype(BF16)

    o_ref[...] = _dot(h_ref[...], w_ref[...]).astype(o_ref.dtype)


def _in_proj(x, mod3, g, w, tm=1024, tn=1024):
    bsz, s, d = x.shape
    n = w.shape[1]
    return pl.pallas_call(
        _in_proj_kernel,
        grid=(bsz, s // tm, n // tn),
        in_specs=[pl.BlockSpec((None, tm, d), lambda b, i, j: (b, i, 0)),
                  pl.BlockSpec((None, N_MOD, d), lambda b, i, j: (b, 0, 0)),
                  pl.BlockSpec((1, d), lambda b, i, j: (0, 0)),
                  pl.BlockSpec((d, tn), lambda b, i, j: (0, j))],
        out_specs=pl.BlockSpec((None, tm, tn), lambda b, i, j: (b, i, j)),
        out_shape=jax.ShapeDtypeStruct((bsz, s, n), BF16),
        scratch_shapes=[pltpu.VMEM((tm, d), BF16)],
        compiler_params=_cparams("parallel", "parallel", "arbitrary"),
        name="in_proj",
    )(x, mod3, g, w)


def _rope_tile(t, cos_m, sin_lo, sin_hi):
    half = MLA_ROPE // 2
    return (t * cos_m + pltpu.roll(t, LANE - half, 1) * sin_lo
            + pltpu.roll(t, half, 1) * sin_hi)


def _mla_prep_kernel(p_ref, pos_ref, invf_ref, qg_ref, kvg_ref, wq_ref, wk_ref, wvt_ref,
                     q_ref, k_ref, vt_ref, *, q_rank, kv_rank, scale):
    half = MLA_ROPE // 2
    ang = pos_ref[...].astype(F32) * invf_ref[...]
    lane = lax.broadcasted_iota(jnp.int32, ang.shape, 1)
    cos, sin = jnp.cos(ang), jnp.sin(ang)
    cos_m = jnp.where(lane < MLA_ROPE, cos, 0.0)
    sin_lo = jnp.where(lane < half, -sin, 0.0)
    sin_hi = jnp.where((lane >= half) & (lane < MLA_ROPE), sin, 0.0)

    q_lat = p_ref[:, 0:q_rank].astype(F32)
    q = _dot(_rms(q_lat, qg_ref[...]).astype(BF16), wq_ref[...])
    kv_lat = p_ref[:, q_rank:q_rank + kv_rank].astype(F32)
    kv_n = _rms(kv_lat, kvg_ref[...]).astype(BF16)
    k_nope = _dot(kv_n, wk_ref[...])
    vt_ref[...] = _dot_nt(wvt_ref[...], kv_n).astype(BF16)
    kr_off = q_rank + kv_rank
    k_rope = _rope_tile(p_ref[:, kr_off:kr_off + LANE].astype(F32),
                        cos_m, sin_lo, sin_hi).astype(BF16)
    for h in range(MLA_HEADS):
        o = h * MLA_QK_PAD
        q_ref[:, o:o + MLA_NOPE] = (q[:, o:o + MLA_NOPE] * scale).astype(BF16)
        qr = _rope_tile(q[:, o + MLA_NOPE:o + MLA_QK_PAD], cos_m, sin_lo, sin_hi)
        q_ref[:, o + MLA_NOPE:o + MLA_QK_PAD] = (qr * scale).astype(BF16)
        k_ref[:, o:o + MLA_NOPE] = k_nope[:, h * MLA_NOPE:(h + 1) * MLA_NOPE].astype(BF16)
        k_ref[:, o + MLA_NOPE:o + MLA_QK_PAD] = k_rope


def _mla_prep(proj, positions, invf, qg, kvg, wq, wk, wvt, q_rank, kv_rank, tm=512):
    bsz, s, _ = proj.shape
    hq = MLA_HEADS * MLA_QK_PAD
    hv = MLA_HEADS * MLA_V
    scale = float((MLA_NOPE + MLA_ROPE) ** -0.5 * np.log2(np.e))
    blk0 = 1024
    kern = functools.partial(_mla_prep_kernel, q_rank=q_rank, kv_rank=kv_rank, scale=scale)

    def const(a):
        return pl.BlockSpec(a.shape, lambda b, i: (0, 0))

    return pl.pallas_call(
        kern,
        grid=(bsz, s // tm),
        in_specs=[pl.BlockSpec((None, tm, blk0), lambda b, i: (b, i, 0)),
                  pl.BlockSpec((None, tm, 1), lambda b, i: (b, i, 0)),
                  const(invf), const(qg), const(kvg), const(wq), const(wk), const(wvt)],
        out_specs=[pl.BlockSpec((None, tm, hq), lambda b, i: (b, i, 0)),
                   pl.BlockSpec((None, tm, hq), lambda b, i: (b, i, 0)),
                   pl.BlockSpec((None, hv, tm), lambda b, i: (b, 0, i))],
        out_shape=[jax.ShapeDtypeStruct((bsz, s, hq), BF16),
                   jax.ShapeDtypeStruct((bsz, s, hq), BF16),
                   jax.ShapeDtypeStruct((bsz, hv, s), BF16)],
        compiler_params=_cparams("parallel", "parallel"),
        name="mla_prep",
    )(proj, positions.reshape(bsz, s, 1), invf, qg, kvg, wq, wk, wvt)


def _attn_kernel(q_ref, k_ref, vt_ref, o_ref, sa_ref, sb_ref, *, tq, tk):
    s_len = k_ref.shape[0]
    n_kv = s_len // tk
    n_q = s_len // tq
    sub = 8

    def scores(i, s_ref):
        q = q_ref[pl.ds(pl.multiple_of(i * tq, tq), tq), :]
        m8 = jnp.full((sub, tq), -jnp.inf, F32)
        for j in range(n_kv):
            st = _dot_nt(k_ref[j * tk:(j + 1) * tk, :], q)
            s_ref[j * tk:(j + 1) * tk, :] = st
            m8 = jnp.maximum(m8, jnp.max(st.reshape(tk // sub, sub, tq), axis=0))
            yield
        return jnp.broadcast_to(jnp.max(m8, axis=0, keepdims=True), (sub, tq))

    def values(i, s_ref, m8):
        l8 = jnp.zeros((sub, tq), F32)
        acc = jnp.zeros((MLA_V, tq), F32)
        for j in range(n_kv):
            st = s_ref[j * tk:(j + 1) * tk, :].reshape(tk // sub, sub, tq)
            p = jnp.exp2(st - m8[None])
            l8 = l8 + jnp.sum(p, axis=0)
            acc = acc + _dot(vt_ref[:, j * tk:(j + 1) * tk],
                             p.reshape(tk, tq).astype(BF16))
            yield
        l = jnp.sum(l8, axis=0, keepdims=True)
        o_ref[pl.ds(pl.multiple_of(i * tq, tq), tq), :] = (acc / l).T.astype(o_ref.dtype)

    def run(*gens):
        result = None
        live = list(gens)
        while live:
            for g in list(live):
                try:
                    next(g)
                except StopIteration as stop:
                    if g is gens[0]:
                        result = stop.value
                    live.remove(g)
        return result

    def pair(k, m_a):
        m_b = run(scores(2 * k + 1, sb_ref), values(2 * k, sa_ref, m_a))
        return run(scores(2 * k + 2, sa_ref), values(2 * k + 1, sb_ref, m_b))

    m_a = lax.fori_loop(0, n_q // 2 - 1, pair, run(scores(0, sa_ref)))
    m_b = run(scores(n_q - 1, sb_ref), values(n_q - 2, sa_ref, m_a))
    run(values(n_q - 1, sb_ref, m_b))


def _attention(q, k, vt, tq=256, tk=512):
    bsz, s, _ = q.shape
    kern = functools.partial(_attn_kernel, tq=tq, tk=tk)
    return pl.pallas_call(
        kern,
        grid=(bsz, MLA_HEADS),
        in_specs=[pl.BlockSpec((None, s, MLA_QK_PAD), lambda b, h: (b, 0, h)),
                  pl.BlockSpec((None, s, MLA_QK_PAD), lambda b, h: (b, 0, h)),
                  pl.BlockSpec((None, MLA_V, s), lambda b, h: (b, h, 0))],
        out_specs=pl.BlockSpec((None, s, MLA_V), lambda b, h: (b, 0, h)),
        out_shape=jax.ShapeDtypeStruct((bsz, s, MLA_HEADS * MLA_V), BF16),
        scratch_shapes=[pltpu.VMEM((s, tq), F32), pltpu.VMEM((s, tq), F32)],
        compiler_params=_cparams("parallel", "parallel"),
        name="attention",
    )(q, k, vt)


def _split_bf16(x):
    hi = x.astype(BF16)
    return hi, (x - hi.astype(F32)).astype(BF16)


def _hgrn_direction(q_ref, z_ref, v_ref, lb, st_ref, o_ref, *, forward):
    c = HGRN_CHUNK
    n = q_ref.shape[0] // c
    row = lax.broadcasted_iota(jnp.int32, (c, c), 0)
    col = lax.broadcasted_iota(jnp.int32, (c, c), 1)
    causal = (col <= row) if forward else (col >= row)
    tri = causal.astype(BF16)
    edge = c - 1 if forward else 0
    mid = c // 2

    chunks = [slice(j * c, (j + 1) * c) for j in range(n)]
    f = lb + (1.0 - lb) * jax.nn.sigmoid(z_ref[...].astype(F32))
    k = (1.0 - f).reshape(n, c, HGRN_DK)
    log_f = jnp.concatenate(_split_bf16(jnp.log(f)), axis=1)
    q = q_ref[...].astype(F32).reshape(n, c, HGRN_DK)
    vt = v_ref[...].astype(F32).T.astype(BF16)
    yield
    b2 = jnp.concatenate([_dot(tri, log_f[r]) for r in chunks], axis=0)
    yield
    b = (b2[:, :HGRN_DK] + b2[:, HGRN_DK:]).reshape(n, c, HGRN_DK)
    b_edge = b[:, edge:edge + 1]
    bm = b - b[:, mid:mid + 1]
    qs = (q * jnp.exp(b)).astype(BF16)
    qm = (q * jnp.exp(bm)).astype(BF16)
    km = (k * jnp.exp(-bm)).astype(BF16)
    kd = (k * jnp.exp(b_edge - b)).astype(BF16)
    decay = jnp.exp(b_edge)
    yield
    a = [_dot_nt(qm[j], km[j]) for j in range(n)]
    upd = [_dot(vt[:, chunks[j]], kd[j]) for j in range(n)]
    yield
    a = [jnp.where(causal, x, 0.0).astype(BF16) for x in a]
    st = st_ref[...]
    states = [None] * n
    for j in (range(n) if forward else reversed(range(n))):
        states[j] = st.astype(BF16)
        st = st * decay[j] + upd[j]
    st_ref[...] = st
    yield
    outs = [_dot_nt(jnp.concatenate([a[j], qs[j]], axis=1),
                    jnp.concatenate([vt[:, chunks[j]], states[j]], axis=1))
            for j in range(n)]
    o_ref[...] = jnp.concatenate(outs, axis=0).astype(o_ref.dtype)
    yield


def _hgrn_kernel(qf_ref, zf_ref, vf_ref, qb_ref, zb_ref, vb_ref, lbf_ref, lbb_ref,
                 of_ref, ob_ref, stf_ref, stb_ref):
    @pl.when(pl.program_id(2) == 0)
    def _():
        stf_ref[...] = jnp.zeros_like(stf_ref)
        stb_ref[...] = jnp.zeros_like(stb_ref)

    def lower_bound(ref):
        l = ref[...]
        e = jnp.exp(l - jnp.max(l, axis=0, keepdims=True))
        return e[0:1, :] / jnp.sum(e, axis=0, keepdims=True)

    fwd = _hgrn_direction(qf_ref, zf_ref, vf_ref, lower_bound(lbf_ref), stf_ref, of_ref,
                          forward=True)
    bwd = _hgrn_direction(qb_ref, zb_ref, vb_ref, lower_bound(lbb_ref), stb_ref, ob_ref,
                          forward=False)
    for _ in zip(fwd, bwd):
        pass


def _hgrn(proj, lbf_logits, lbb_logits, col_q, col_ff, col_fb, col_v, tb=512):
    bsz, s, _ = proj.shape
    nb = s // tb
    width = HGRN_HEADS * HGRN_DV

    def fwd(col):
        return pl.BlockSpec((None, tb, LANE), lambda b, h, i: (b, i, col + h))

    def bwd(col):
        return pl.BlockSpec((None, tb, LANE), lambda b, h, i: (b, nb - 1 - i, col + h))

    lb_spec = pl.BlockSpec((lbf_logits.shape[0], LANE), lambda b, h, i: (0, h))
    return pl.pallas_call(
        _hgrn_kernel,
        grid=(bsz, HGRN_HEADS, nb),
        in_specs=[fwd(col_q), fwd(col_ff), fwd(col_v),
                  bwd(col_q), bwd(col_fb), bwd(col_v), lb_spec, lb_spec],
        out_specs=[pl.BlockSpec((None, tb, LANE), lambda b, h, i: (b, i, h)),
                   pl.BlockSpec((None, tb, LANE), lambda b, h, i: (b, nb - 1 - i, h))],
        out_shape=[jax.ShapeDtypeStruct((bsz, s, width), F32),
                   jax.ShapeDtypeStruct((bsz, s, width), F32)],
        scratch_shapes=[pltpu.VMEM((HGRN_DV, HGRN_DK), F32),
                        pltpu.VMEM((HGRN_DV, HGRN_DK), F32)],
        compiler_params=_cparams("parallel", "parallel", "arbitrary"),
        name="hgrn",
    )(proj, proj, proj, proj, proj, proj, lbf_logits, lbb_logits)


def _mix_out_kernel(attn_ref, of_ref, ob_ref, hg_ref, hgn_ref, w_ref, x_ref, mod_ref,
                    postg_ref, preg_ref, x1_ref, h2_ref):
    o = of_ref[...] + ob_ref[...]
    hg = hg_ref[...].astype(F32)
    gate = hg * jax.nn.sigmoid(hg)
    rec = []
    for h in range(HGRN_HEADS):
        sl = slice(h * HGRN_DV, (h + 1) * HGRN_DV)
        rec.append((_rms(o[:, sl], hgn_ref[...]) * gate[:, sl]).astype(BF16))
    rec = jnp.concatenate(rec, axis=-1)
    na = attn_ref.shape[-1]
    mix = _dot(attn_ref[...], w_ref[0:na, :]) + _dot(rec, w_ref[na:, :])
    x1 = x_ref[...] + mod_ref[2:3, :] * _rms(mix, postg_ref[...])
    x1_ref[...] = x1
    h2 = _rms(x1, preg_ref[...]) * (1.0 + mod_ref[4:5, :]) + mod_ref[3:4, :]
    h2_ref[...] = h2.astype(BF16)


def _mix_out(attn, o_f, o_b, proj, col_hg, hgn_g, w_out, x, mod3, post_g, pre_g, tm=512):
    bsz, s, d = x.shape
    na = attn.shape[-1]
    nr = o_f.shape[-1]

    def row(width):
        return pl.BlockSpec((None, tm, width), lambda b, i: (b, i, 0))

    def const(shape):
        return pl.BlockSpec(shape, lambda b, i: (0,) * len(shape))

    return pl.pallas_call(
        _mix_out_kernel,
        grid=(bsz, s // tm),
        in_specs=[row(na), row(nr), row(nr),
                  pl.BlockSpec((None, tm, nr), lambda b, i: (b, i, col_hg)),
                  const((1, HGRN_DV)), const(w_out.shape), row(d),
                  pl.BlockSpec((None, N_MOD, d), lambda b, i: (b, 0, 0)),
                  const((1, d)), const((1, d))],
        out_specs=[row(d), row(d)],
        out_shape=[jax.ShapeDtypeStruct((bsz, s, d), F32),
                   jax.ShapeDtypeStruct((bsz, s, d), BF16)],
        compiler_params=_cparams("parallel", "parallel"),
        name="mix_out",
    )(attn, o_f, o_b, proj, hgn_g, w_out, x, mod3, post_g, pre_g)


def _mlp_kernel(h_ref, wu_ref, wd_ref, x1_ref, mod_ref, g_ref, o_ref):
    j = pl.program_id(2)

    @pl.when(j == 0)
    def _():
        o_ref[...] = jnp.zeros_like(o_ref)

    u = jnp.maximum(_dot(h_ref[...], wu_ref[...]), 0.0)
    o_ref[...] += _dot((u * u).astype(BF16), wd_ref[...])

    @pl.when(j == pl.num_programs(2) - 1)
    def _():
        o_ref[...] = x1_ref[...] + mod_ref[5:6, :] * _rms(o_ref[...], g_ref[...])


def _mlp(h2, w_up, w_down, x1, mod3, g, tm=512, tf=1024):
    bsz, s, d = x1.shape
    dff = w_up.shape[1]
    return pl.pallas_call(
        _mlp_kernel,
        grid=(bsz, s // tm, dff // tf),
        in_specs=[pl.BlockSpec((None, tm, d), lambda b, i, j: (b, i, 0)),
                  pl.BlockSpec((d, tf), lambda b, i, j: (0, j)),
                  pl.BlockSpec((tf, d), lambda b, i, j: (j, 0)),
                  pl.BlockSpec((None, tm, d), lambda b, i, j: (b, i, 0)),
                  pl.BlockSpec((None, N_MOD, d), lambda b, i, j: (b, 0, 0)),
                  pl.BlockSpec((1, d), lambda b, i, j: (0, 0))],
        out_specs=pl.BlockSpec((None, tm, d), lambda b, i, j: (b, i, 0)),
        out_shape=jax.ShapeDtypeStruct((bsz, s, d), F32),
        compiler_params=_cparams("parallel", "parallel", "arbitrary"),
        name="mlp",
    )(h2, w_up, w_down, x1, mod3, g)


def _pack_w_in(w, q_rank, kv_rank):
    head = q_rank + kv_rank + MLA_ROPE
    pad = 1024 - head
    d = w.shape[0]
    return jnp.concatenate([w[:, :head], jnp.zeros((d, pad), w.dtype), w[:, head:]],
                           axis=1).astype(BF16)


def _pack_w_uq(w):
    r = w.shape[0]
    w = w.reshape(r, MLA_HEADS, MLA_NOPE + MLA_ROPE)
    pad = jnp.zeros((r, MLA_HEADS, MLA_QK_PAD - MLA_NOPE - MLA_ROPE), w.dtype)
    return jnp.concatenate([w, pad], axis=-1).reshape(r, MLA_HEADS * MLA_QK_PAD).astype(BF16)


def _pack_w_ukv(w):
    r = w.shape[0]
    w = w.reshape(r, MLA_HEADS, MLA_NOPE + MLA_V)
    wk = w[:, :, :MLA_NOPE].reshape(r, -1).astype(BF16)
    wvt = w[:, :, MLA_NOPE:].reshape(r, -1).T.astype(BF16)
    return wk, wvt


def _rope_inv_freq():
    half = MLA_ROPE // 2
    f = ROPE_BASE ** (-jnp.arange(half, dtype=F32) / half)
    return jnp.concatenate([f, f, jnp.zeros((LANE - MLA_ROPE,), F32)]).reshape(1, LANE)


def kernel(x, c, positions, w_mod, b_mod, pre_mix_g, post_mix_g, pre_mlp_g, post_mlp_g,
           w_in, q_norm_g, kv_norm_g, w_uq, w_ukv, hgrn_norm_g, hgrn_lb_logits_fwd,
           hgrn_lb_logits_bwd, w_out, w_up, w_down):
    depth = w_mod.shape[0]
    assert depth == 1, "lower-bound slot selection is written for a single layer"
    bsz, s, d = x.shape
    q_rank = q_norm_g.shape[-1]
    kv_rank = kv_norm_g.shape[-1]
    invf = _rope_inv_freq()
    for layer in range(depth):
        mod3 = _mod(c, w_mod[layer], b_mod[layer]).reshape(bsz, N_MOD, d)
        proj = _in_proj(x, mod3, pre_mix_g[layer].reshape(1, d),
                        _pack_w_in(w_in[layer], q_rank, kv_rank))
        wk, wvt = _pack_w_ukv(w_ukv[layer])
        q, k, vt = _mla_prep(proj, positions, invf, q_norm_g[layer].reshape(1, -1),
                             kv_norm_g[layer].reshape(1, -1), _pack_w_uq(w_uq[layer]),
                             wk, wvt, q_rank, kv_rank)
        attn = _attention(q, k, vt)
        o_f, o_b = _hgrn(proj, hgrn_lb_logits_fwd, hgrn_lb_logits_bwd,
                         col_q=8, col_ff=16, col_fb=24, col_v=32)
        x1, h2 = _mix_out(attn, o_f, o_b, proj, 5, hgrn_norm_g[layer].reshape(1, -1),
                          w_out[layer].astype(BF16), x, mod3,
                          post_mix_g[layer].reshape(1, d), pre_mlp_g[layer].reshape(1, d))
        x = _mlp(h2, w_up[layer].astype(BF16), w_down[layer].astype(BF16), x1, mod3,
                 post_mlp_g[layer].reshape(1, d))
    return x
```

```python
import functools

import jax
import jax.numpy as jnp
import numpy as np
from jax import lax
from jax.experimental import pallas as pl
from jax.experimental.pallas import tpu as pltpu

F32 = jnp.float32
BF16 = jnp.bfloat16

NORM_EPS = 1e-6
ROPE_BASE = 10000.0
N_MOD = 6
LANE = 128
VMEM_LIMIT = 56 * 1024 * 1024

MLA_HEADS = 8
MLA_NOPE = 128
MLA_ROPE = 64
MLA_V = 128
MLA_QK_PAD = 256
MLA_LAT_PAD = 1024
HGRN_HEADS = 8
HGRN_DK = 128
HGRN_DV = 128
HGRN_CHUNK = 64
MIX_SUB_ROWS = 128


def _cparams(*sem):
    return pltpu.CompilerParams(dimension_semantics=sem, vmem_limit_bytes=VMEM_LIMIT)


def _dot(a, b):
    return jnp.dot(a, b, preferred_element_type=F32)


def _dot_nt(a, b):
    return lax.dot_general(a, b, (((1,), (1,)), ((), ())), preferred_element_type=F32)


def _dot_tn(a, b):
    return lax.dot_general(a, b, (((0,), (0,)), ((), ())), preferred_element_type=F32)


def _rms(x, g):
    return x * lax.rsqrt(jnp.mean(x * x, axis=-1, keepdims=True) + NORM_EPS) * g


def _mod_kernel(c_ref, w_ref, b_ref, o_ref):
    c = c_ref[...]
    cond = c * jax.nn.sigmoid(c)
    hi = cond.astype(BF16)
    lo = (cond - hi.astype(F32)).astype(BF16)
    w = w_ref[...].astype(BF16)
    o_ref[...] = _dot(hi, w) + _dot(lo, w) + b_ref[...]


def _mod(c, w_mod, b_mod, tn=1024):
    bsz, d = c.shape
    n = w_mod.shape[1]
    return pl.pallas_call(
        _mod_kernel,
        grid=(n // tn,),
        in_specs=[pl.BlockSpec((bsz, d), lambda j: (0, 0)),
                  pl.BlockSpec((d, tn), lambda j: (0, j)),
                  pl.BlockSpec((1, tn), lambda j: (0, j))],
        out_specs=pl.BlockSpec((bsz, tn), lambda j: (0, j)),
        out_shape=jax.ShapeDtypeStruct((bsz, n), F32),
        compiler_params=_cparams("parallel"),
        name="mod",
    )(c, w_mod, b_mod.reshape(1, n))


def _in_proj_kernel(x_ref, mod_ref, g_ref, wa_ref, wb_ref, oa_ref, ob_ref, h_ref):
    @pl.when(pl.program_id(2) == 0)
    def _():
        h = _rms(x_ref[...], g_ref[...]) * (1.0 + mod_ref[1:2, :]) + mod_ref[0:1, :]
        h = h.astype(BF16)
        h_ref[...] = h
        oa_ref[...] = _dot(h, wa_ref[...]).astype(oa_ref.dtype)

    ob_ref[...] = _dot(h_ref[...], wb_ref[...]).astype(ob_ref.dtype)


def _in_proj(x, mod3, g, wa, wb, tm=1024, tn=1280):
    bsz, s, d = x.shape
    na, nb = wa.shape[1], wb.shape[1]
    return pl.pallas_call(
        _in_proj_kernel,
        grid=(bsz, s // tm, nb // tn),
        in_specs=[pl.BlockSpec((None, tm, d), lambda b, i, j: (b, i, 0)),
                  pl.BlockSpec((None, N_MOD, d), lambda b, i, j: (b, 0, 0)),
                  pl.BlockSpec((1, d), lambda b, i, j: (0, 0)),
                  pl.BlockSpec((d, na), lambda b, i, j: (0, 0)),
                  pl.BlockSpec((d, tn), lambda b, i, j: (0, j))],
        out_specs=[pl.BlockSpec((None, tm, na), lambda b, i, j: (b, i, 0)),
                   pl.BlockSpec((None, tm, tn), lambda b, i, j: (b, i, j))],
        out_shape=[jax.ShapeDtypeStruct((bsz, s, na), BF16),
                   jax.ShapeDtypeStruct((bsz, s, nb), BF16)],
        scratch_shapes=[pltpu.VMEM((tm, d), BF16)],
        compiler_params=_cparams("parallel", "parallel", "arbitrary"),
        name="in_proj",
    )(x, mod3, g, wa, wb)


def _rope_tile(t, cos_m, sin_lo, sin_hi):
    half = MLA_ROPE // 2
    return (t * cos_m + pltpu.roll(t, LANE - half, 1) * sin_lo
            + pltpu.roll(t, half, 1) * sin_hi)


def _mla_prep_kernel(p_ref, pos_ref, invf_ref, qg_ref, kvg_ref, wq_ref, wk_ref, wvt_ref,
                     q_ref, k_ref, vt_ref, *, q_rank, kv_rank, scale):
    half = MLA_ROPE // 2
    ang = pos_ref[...].astype(F32) * invf_ref[...]
    lane = lax.broadcasted_iota(jnp.int32, ang.shape, 1)
    cos, sin = jnp.cos(ang), jnp.sin(ang)
    cos_m = jnp.where(lane < MLA_ROPE, cos, 0.0)
    sin_lo = jnp.where(lane < half, -sin, 0.0)
    sin_hi = jnp.where((lane >= half) & (lane < MLA_ROPE), sin, 0.0)

    q_lat = p_ref[:, 0:q_rank].astype(F32)
    q = _dot(_rms(q_lat, qg_ref[...]).astype(BF16), wq_ref[...])
    kv_lat = p_ref[:, q_rank:q_rank + kv_rank].astype(F32)
    kv_n = _rms(kv_lat, kvg_ref[...]).astype(BF16)
    k_nope = _dot(kv_n, wk_ref[...])
    vt_ref[...] = _dot_nt(wvt_ref[...], kv_n).astype(BF16)
    kr_off = q_rank + kv_rank
    k_rope = _rope_tile(p_ref[:, kr_off:kr_off + LANE].astype(F32),
                        cos_m, sin_lo, sin_hi).astype(BF16)
    for h in range(MLA_HEADS):
        o = h * MLA_QK_PAD
        q_ref[:, o:o + MLA_NOPE] = (q[:, o:o + MLA_NOPE] * scale).astype(BF16)
        qr = _rope_tile(q[:, o + MLA_NOPE:o + MLA_QK_PAD], cos_m, sin_lo, sin_hi)
        q_ref[:, o + MLA_NOPE:o + MLA_QK_PAD] = (qr * scale).astype(BF16)
        k_ref[:, o:o + MLA_NOPE] = k_nope[:, h * MLA_NOPE:(h + 1) * MLA_NOPE].astype(BF16)
        k_ref[:, o + MLA_NOPE:o + MLA_QK_PAD] = k_rope


def _mla_prep(proj, positions, invf, qg, kvg, wq, wk, wvt, q_rank, kv_rank, tm=512):
    bsz, s, blk0 = proj.shape
    hq = MLA_HEADS * MLA_QK_PAD
    hv = MLA_HEADS * MLA_V
    scale = float((MLA_NOPE + MLA_ROPE) ** -0.5 * np.log2(np.e))
    kern = functools.partial(_mla_prep_kernel, q_rank=q_rank, kv_rank=kv_rank, scale=scale)

    def const(a):
        return pl.BlockSpec(a.shape, lambda b, i: (0, 0))

    return pl.pallas_call(
        kern,
        grid=(bsz, s // tm),
        in_specs=[pl.BlockSpec((None, tm, blk0), lambda b, i: (b, i, 0)),
                  pl.BlockSpec((None, tm, 1), lambda b, i: (b, i, 0)),
                  const(invf), const(qg), const(kvg), const(wq), const(wk), const(wvt)],
        out_specs=[pl.BlockSpec((None, tm, hq), lambda b, i: (b, i, 0)),
                   pl.BlockSpec((None, tm, hq), lambda b, i: (b, i, 0)),
                   pl.BlockSpec((None, hv, tm), lambda b, i: (b, 0, i))],
        out_shape=[jax.ShapeDtypeStruct((bsz, s, hq), BF16),
                   jax.ShapeDtypeStruct((bsz, s, hq), BF16),
                   jax.ShapeDtypeStruct((bsz, hv, s), BF16)],
        compiler_params=_cparams("parallel", "parallel"),
        name="mla_prep",
    )(proj, positions.reshape(bsz, s, 1), invf, qg, kvg, wq, wk, wvt)


def _attn_kernel(q_ref, k_ref, vt_ref, o_ref, sa_ref, sb_ref, *, tq, tk):
    s_len = k_ref.shape[0]
    n_kv = s_len // tk
    n_q = s_len // tq
    sub = 8

    def scores(i, s_ref):
        q = q_ref[pl.ds(pl.multiple_of(i * tq, tq), tq), :]
        m8 = jnp.full((sub, tq), -jnp.inf, F32)
        for j in range(n_kv):
            st = _dot_nt(k_ref[j * tk:(j + 1) * tk, :], q)
            s_ref[j * tk:(j + 1) * tk, :] = st
            m8 = jnp.maximum(m8, jnp.max(st.reshape(tk // sub, sub, tq), axis=0))
            yield
        return jnp.broadcast_to(jnp.max(m8, axis=0, keepdims=True), (sub, tq))

    def values(i, s_ref, m8):
        l8 = jnp.zeros((sub, tq), F32)
        acc = jnp.zeros((MLA_V, tq), F32)
        for j in range(n_kv):
            st = s_ref[j * tk:(j + 1) * tk, :].reshape(tk // sub, sub, tq)
            p = jnp.exp2(st - m8[None])
            l8 = l8 + jnp.sum(p, axis=0)
            acc = acc + _dot(vt_ref[:, j * tk:(j + 1) * tk],
                             p.reshape(tk, tq).astype(BF16))
            yield
        l = jnp.sum(l8, axis=0, keepdims=True)
        o_ref[pl.ds(pl.multiple_of(i * tq, tq), tq), :] = (acc / l).T.astype(o_ref.dtype)

    def run(*gens):
        result = None
        live = list(gens)
        while live:
            for g in list(live):
                try:
                    next(g)
                except StopIteration as stop:
                    if g is gens[0]:
                        result = stop.value
                    live.remove(g)
        return result

    def pair(k, m_a):
        m_b = run(scores(2 * k + 1, sb_ref), values(2 * k, sa_ref, m_a))
        return run(scores(2 * k + 2, sa_ref), values(2 * k + 1, sb_ref, m_b))

    m_a = lax.fori_loop(0, n_q // 2 - 1, pair, run(scores(0, sa_ref)))
    m_b = run(scores(n_q - 1, sb_ref), values(n_q - 2, sa_ref, m_a))
    run(values(n_q - 1, sb_ref, m_b))


def _attention(q, k, vt, tq=256, tk=512):
    bsz, s, _ = q.shape
    kern = functools.partial(_attn_kernel, tq=tq, tk=tk)
    return pl.pallas_call(
        kern,
        grid=(bsz, MLA_HEADS),
        in_specs=[pl.BlockSpec((None, s, MLA_QK_PAD), lambda b, h: (b, 0, h)),
                  pl.BlockSpec((None, s, MLA_QK_PAD), lambda b, h: (b, 0, h)),
                  pl.BlockSpec((None, MLA_V, s), lambda b, h: (b, h, 0))],
        out_specs=pl.BlockSpec((None, s, MLA_V), lambda b, h: (b, 0, h)),
        out_shape=jax.ShapeDtypeStruct((bsz, s, MLA_HEADS * MLA_V), BF16),
        scratch_shapes=[pltpu.VMEM((s, tq), F32), pltpu.VMEM((s, tq), F32)],
        compiler_params=_cparams("parallel", "parallel"),
        name="attention",
    )(q, k, vt)


def _split_bf16(x):
    hi = x.astype(BF16)
    return hi, (x - hi.astype(F32)).astype(BF16)


def _hgrn_direction(q_ref, z_ref, v_ref, lb, st_ref, o_ref, *, forward):
    c = HGRN_CHUNK
    n = q_ref.shape[0] // c
    row = lax.broadcasted_iota(jnp.int32, (c, c), 0)
    col = lax.broadcasted_iota(jnp.int32, (c, c), 1)
    causal = (col <= row) if forward else (col >= row)
    tri = causal.astype(BF16)
    edge = c - 1 if forward else 0
    mid = c // 2

    chunks = [slice(j * c, (j + 1) * c) for j in range(n)]
    f = lb + (1.0 - lb) * jax.nn.sigmoid(z_ref[...].astype(F32))
    k = (1.0 - f).reshape(n, c, HGRN_DK)
    log_f = jnp.concatenate(_split_bf16(jnp.log(f)), axis=1)
    q = q_ref[...].astype(F32).reshape(n, c, HGRN_DK)
    vt = v_ref[...].astype(F32).T.astype(BF16)
    yield
    b2 = jnp.concatenate([_dot(tri, log_f[r]) for r in chunks], axis=0)
    yield
    b = (b2[:, :HGRN_DK] + b2[:, HGRN_DK:]).reshape(n, c, HGRN_DK)
    b_edge = b[:, edge:edge + 1]
    bm = b - b[:, mid:mid + 1]
    qs = (q * jnp.exp(b)).astype(BF16)
    qm = (q * jnp.exp(bm)).astype(BF16)
    km = (k * jnp.exp(-bm)).astype(BF16)
    kd = (k * jnp.exp(b_edge - b)).astype(BF16)
    decay = jnp.exp(b_edge)
    yield
    a = [_dot_nt(qm[j], km[j]) for j in range(n)]
    upd = [_dot(vt[:, chunks[j]], kd[j]) for j in range(n)]
    yield
    a = [jnp.where(causal, x, 0.0).astype(BF16) for x in a]
    st = st_ref[...]
    states = [None] * n
    for j in (range(n) if forward else reversed(range(n))):
        states[j] = st.astype(BF16)
        st = st * decay[j] + upd[j]
    st_ref[...] = st
    yield
    outs = [_dot_nt(jnp.concatenate([a[j], qs[j]], axis=1),
                    jnp.concatenate([vt[:, chunks[j]], states[j]], axis=1))
            for j in range(n)]
    o_ref[...] = jnp.concatenate(outs, axis=0).astype(o_ref.dtype)
    yield


def _hgrn_kernel(qf_ref, zf_ref, vf_ref, qb_ref, zb_ref, vb_ref, lbf_ref, lbb_ref,
                 of_ref, ob_ref, stf_ref, stb_ref):
    @pl.when(pl.program_id(2) == 0)
    def _():
        stf_ref[...] = jnp.zeros_like(stf_ref)
        stb_ref[...] = jnp.zeros_like(stb_ref)

    def lower_bound(ref):
        l = ref[...]
        e = jnp.exp(l - jnp.max(l, axis=0, keepdims=True))
        return e[0:1, :] / jnp.sum(e, axis=0, keepdims=True)

    fwd = _hgrn_direction(qf_ref, zf_ref, vf_ref, lower_bound(lbf_ref), stf_ref, of_ref,
                          forward=True)
    bwd = _hgrn_direction(qb_ref, zb_ref, vb_ref, lower_bound(lbb_ref), stb_ref, ob_ref,
                          forward=False)
    for _ in zip(fwd, bwd):
        pass


def _hgrn(proj, lbf_logits, lbb_logits, col_q, col_ff, col_fb, col_v, tb=512):
    bsz, s, _ = proj.shape
    nb = s // tb
    width = HGRN_HEADS * HGRN_DV

    def fwd(col):
        return pl.BlockSpec((None, tb, LANE), lambda b, h, i: (b, i, col + h))

    def bwd(col):
        return pl.BlockSpec((None, tb, LANE), lambda b, h, i: (b, nb - 1 - i, col + h))

    lb_spec = pl.BlockSpec((lbf_logits.shape[0], LANE), lambda b, h, i: (0, h))
    return pl.pallas_call(
        _hgrn_kernel,
        grid=(bsz, HGRN_HEADS, nb),
        in_specs=[fwd(col_q), fwd(col_ff), fwd(col_v),
                  bwd(col_q), bwd(col_fb), bwd(col_v), lb_spec, lb_spec],
        out_specs=[pl.BlockSpec((None, tb, LANE), lambda b, h, i: (b, i, h)),
                   pl.BlockSpec((None, tb, LANE), lambda b, h, i: (b, nb - 1 - i, h))],
        out_shape=[jax.ShapeDtypeStruct((bsz, s, width), F32),
                   jax.ShapeDtypeStruct((bsz, s, width), F32)],
        scratch_shapes=[pltpu.VMEM((HGRN_DV, HGRN_DK), F32),
                        pltpu.VMEM((HGRN_DV, HGRN_DK), F32)],
        compiler_params=_cparams("parallel", "parallel", "arbitrary"),
        name="hgrn",
    )(proj, proj, proj, proj, proj, proj, lbf_logits, lbb_logits)


def _mix_out_kernel(attn_ref, of_ref, ob_ref, hg_ref, hgn_ref, w_ref, x_ref, mod_ref,
                    postg_ref, preg_ref, x1_ref, h2_ref):
    na = attn_ref.shape[-1]
    tm = attn_ref.shape[0]
    tiles = [slice(r, r + MIX_SUB_ROWS) for r in range(0, tm, MIX_SUB_ROWS)]
    recs = []
    for t in tiles:
        o = of_ref[t, :] + ob_ref[t, :]
        hg = hg_ref[t, :].astype(F32)
        gate = hg * jax.nn.sigmoid(hg)
        rec = []
        for h in range(HGRN_HEADS):
            sl = slice(h * HGRN_DV, (h + 1) * HGRN_DV)
            rec.append((_rms(o[:, sl], hgn_ref[...]) * gate[:, sl]).astype(BF16))
        recs.append(jnp.concatenate(rec, axis=-1))
    mixes = [_dot(attn_ref[t, :], w_ref[0:na, :]) + _dot(rec, w_ref[na:, :])
             for t, rec in zip(tiles, recs)]
    for t, mix in zip(tiles, mixes):
        x1 = x_ref[t, :] + mod_ref[2:3, :] * _rms(mix, postg_ref[...])
        x1_ref[t, :] = x1
        h2 = _rms(x1, preg_ref[...]) * (1.0 + mod_ref[4:5, :]) + mod_ref[3:4, :]
        h2_ref[t, :] = h2.astype(BF16)


def _mix_out(attn, o_f, o_b, proj, col_hg, hgn_g, w_out, x, mod3, post_g, pre_g, tm=512):
    bsz, s, d = x.shape
    na = attn.shape[-1]
    nr = o_f.shape[-1]

    def row(width):
        return pl.BlockSpec((None, tm, width), lambda b, i: (b, i, 0))

    def const(shape):
        return pl.BlockSpec(shape, lambda b, i: (0,) * len(shape))

    return pl.pallas_call(
        _mix_out_kernel,
        grid=(bsz, s // tm),
        in_specs=[row(na), row(nr), row(nr),
                  pl.BlockSpec((None, tm, nr), lambda b, i: (b, i, col_hg)),
                  const((1, HGRN_DV)), const(w_out.shape), row(d),
                  pl.BlockSpec((None, N_MOD, d), lambda b, i: (b, 0, 0)),
                  const((1, d)), const((1, d))],
        out_specs=[row(d), row(d)],
        out_shape=[jax.ShapeDtypeStruct((bsz, s, d), F32),
                   jax.ShapeDtypeStruct((bsz, s, d), BF16)],
        compiler_params=_cparams("parallel", "parallel"),
        name="mix_out",
    )(attn, o_f, o_b, proj, hgn_g, w_out, x, mod3, post_g, pre_g)


def _mlp_kernel(h_ref, wu_ref, wd_ref, x1_ref, mod_ref, g_ref, o_ref):
    j = pl.program_id(2)

    @pl.when(j == 0)
    def _():
        o_ref[...] = jnp.zeros_like(o_ref)

    u = jnp.maximum(_dot(h_ref[...], wu_ref[...]), 0.0)
    o_ref[...] += _dot((u * u).astype(BF16), wd_ref[...])

    @pl.when(j == pl.num_programs(2) - 1)
    def _():
        o_ref[...] = x1_ref[...] + mod_ref[5:6, :] * _rms(o_ref[...], g_ref[...])


def _mlp(h2, w_up, w_down, x1, mod3, g, tm=512, tf=1024):
    bsz, s, d = x1.shape
    dff = w_up.shape[1]
    return pl.pallas_call(
        _mlp_kernel,
        grid=(bsz, s // tm, dff // tf),
        in_specs=[pl.BlockSpec((None, tm, d), lambda b, i, j: (b, i, 0)),
                  pl.BlockSpec((d, tf), lambda b, i, j: (0, j)),
                  pl.BlockSpec((tf, d), lambda b, i, j: (j, 0)),
                  pl.BlockSpec((None, tm, d), lambda b, i, j: (b, i, 0)),
                  pl.BlockSpec((None, N_MOD, d), lambda b, i, j: (b, 0, 0)),
                  pl.BlockSpec((1, d), lambda b, i, j: (0, 0))],
        out_specs=pl.BlockSpec((None, tm, d), lambda b, i, j: (b, i, 0)),
        out_shape=jax.ShapeDtypeStruct((bsz, s, d), F32),
        compiler_params=_cparams("parallel", "parallel", "arbitrary"),
        name="mlp",
    )(h2, w_up, w_down, x1, mod3, g)


def _pack_w_in(w, q_rank, kv_rank):
    head = q_rank + kv_rank + MLA_ROPE
    wa = jnp.pad(w[:, :head].astype(BF16), ((0, 0), (0, MLA_LAT_PAD - head)))
    return wa, w[:, head:].astype(BF16)


def _pack_w_uq(w):
    r = w.shape[0]
    w = w.reshape(r, MLA_HEADS, MLA_NOPE + MLA_ROPE)
    pad = jnp.zeros((r, MLA_HEADS, MLA_QK_PAD - MLA_NOPE - MLA_ROPE), w.dtype)
    return jnp.concatenate([w, pad], axis=-1).reshape(r, MLA_HEADS * MLA_QK_PAD).astype(BF16)


def _pack_w_ukv(w):
    r = w.shape[0]
    w = w.reshape(r, MLA_HEADS, MLA_NOPE + MLA_V)
    wk = w[:, :, :MLA_NOPE].reshape(r, -1).astype(BF16)
    wvt = w[:, :, MLA_NOPE:].reshape(r, -1).T.astype(BF16)
    return wk, wvt


def _rope_inv_freq():
    half = MLA_ROPE // 2
    f = ROPE_BASE ** (-jnp.arange(half, dtype=F32) / half)
    return jnp.concatenate([f, f, jnp.zeros((LANE - MLA_ROPE,), F32)]).reshape(1, LANE)


def kernel(x, c, positions, w_mod, b_mod, pre_mix_g, post_mix_g, pre_mlp_g, post_mlp_g,
           w_in, q_norm_g, kv_norm_g, w_uq, w_ukv, hgrn_norm_g, hgrn_lb_logits_fwd,
           hgrn_lb_logits_bwd, w_out, w_up, w_down):
    depth = w_mod.shape[0]
    assert depth == 1, "lower-bound slot selection is written for a single layer"
    bsz, s, d = x.shape
    q_rank = q_norm_g.shape[-1]
    kv_rank = kv_norm_g.shape[-1]
    invf = _rope_inv_freq()
    for layer in range(depth):
        mod3 = _mod(c, w_mod[layer], b_mod[layer]).reshape(bsz, N_MOD, d)
        lat, proj = _in_proj(x, mod3, pre_mix_g[layer].reshape(1, d),
                             *_pack_w_in(w_in[layer], q_rank, kv_rank))
        wk, wvt = _pack_w_ukv(w_ukv[layer])
        q, k, vt = _mla_prep(lat, positions, invf, q_norm_g[layer].reshape(1, -1),
                             kv_norm_g[layer].reshape(1, -1), _pack_w_uq(w_uq[layer]),
                             wk, wvt, q_rank, kv_rank)
        attn = _attention(q, k, vt)
        o_f, o_b = _hgrn(proj, hgrn_lb_logits_fwd, hgrn_lb_logits_bwd,
                         col_q=0, col_ff=8, col_fb=16, col_v=24)
        x1, h2 = _mix_out(attn, o_f, o_b, proj, 4, hgrn_norm_g[layer].reshape(1, -1),
                          w_out[layer].astype(BF16), x, mod3,
                          post_mix_g[layer].reshape(1, d), pre_mlp_g[layer].reshape(1, d))
        x = _mlp(h2, w_up[layer].astype(BF16), w_down[layer].astype(BF16), x1, mod3,
                 post_mlp_g[layer].reshape(1, d))
    return x
```

```python
import functools

import jax
import jax.numpy as jnp
import numpy as np
from jax import lax
from jax.experimental import pallas as pl
from jax.experimental.pallas import tpu as pltpu

F32 = jnp.float32
BF16 = jnp.bfloat16

NORM_EPS = 1e-6
ROPE_BASE = 10000.0
N_MOD = 6
LANE = 128
VMEM_LIMIT = 56 * 1024 * 1024

MLA_HEADS = 8
MLA_NOPE = 128
MLA_ROPE = 64
MLA_V = 128
MLA_QK_PAD = 256
MLA_LAT_PAD = 1024
HGRN_HEADS = 8
HGRN_DK = 128
HGRN_DV = 128
HGRN_CHUNK = 64
HGRN_SAFE_LOG2_SPREAD = 96.0
MIX_SUB_ROWS = 128


def _cparams(*sem):
    return pltpu.CompilerParams(dimension_semantics=sem, vmem_limit_bytes=VMEM_LIMIT)


def _dot(a, b):
    return jnp.dot(a, b, preferred_element_type=F32)


def _dot_nt(a, b):
    return lax.dot_general(a, b, (((1,), (1,)), ((), ())), preferred_element_type=F32)


def _dot_tn(a, b):
    return lax.dot_general(a, b, (((0,), (0,)), ((), ())), preferred_element_type=F32)


def _rms(x, g):
    return x * lax.rsqrt(jnp.mean(x * x, axis=-1, keepdims=True) + NORM_EPS) * g


def _mod_kernel(c_ref, w_ref, b_ref, o_ref):
    c = c_ref[...]
    cond = c * jax.nn.sigmoid(c)
    hi = cond.astype(BF16)
    lo = (cond - hi.astype(F32)).astype(BF16)
    w = w_ref[...].astype(BF16)
    o_ref[...] = _dot(hi, w) + _dot(lo, w) + b_ref[...]


def _mod(c, w_mod, b_mod, tn=1024):
    bsz, d = c.shape
    n = w_mod.shape[1]
    return pl.pallas_call(
        _mod_kernel,
        grid=(n // tn,),
        in_specs=[pl.BlockSpec((bsz, d), lambda j: (0, 0)),
                  pl.BlockSpec((d, tn), lambda j: (0, j)),
                  pl.BlockSpec((1, tn), lambda j: (0, j))],
        out_specs=pl.BlockSpec((bsz, tn), lambda j: (0, j)),
        out_shape=jax.ShapeDtypeStruct((bsz, n), F32),
        compiler_params=_cparams("parallel"),
        name="mod",
    )(c, w_mod, b_mod.reshape(1, n))


def _in_proj_kernel(x_ref, mod_ref, g_ref, wa_ref, wb_ref, oa_ref, ob_ref, h_ref):
    @pl.when(pl.program_id(2) == 0)
    def _():
        h = _rms(x_ref[...], g_ref[...]) * (1.0 + mod_ref[1:2, :]) + mod_ref[0:1, :]
        h = h.astype(BF16)
        h_ref[...] = h
        oa_ref[...] = _dot(h, wa_ref[...]).astype(oa_ref.dtype)

    ob_ref[...] = _dot(h_ref[...], wb_ref[...]).astype(ob_ref.dtype)


def _in_proj(x, mod3, g, wa, wb, tm=1024, tn=1280):
    bsz, s, d = x.shape
    na, nb = wa.shape[1], wb.shape[1]
    return pl.pallas_call(
        _in_proj_kernel,
        grid=(bsz, s // tm, nb // tn),
        in_specs=[pl.BlockSpec((None, tm, d), lambda b, i, j: (b, i, 0)),
                  pl.BlockSpec((None, N_MOD, d), lambda b, i, j: (b, 0, 0)),
                  pl.BlockSpec((1, d), lambda b, i, j: (0, 0)),
                  pl.BlockSpec((d, na), lambda b, i, j: (0, 0)),
                  pl.BlockSpec((d, tn), lambda b, i, j: (0, j))],
        out_specs=[pl.BlockSpec((None, tm, na), lambda b, i, j: (b, i, 0)),
                   pl.BlockSpec((None, tm, tn), lambda b, i, j: (b, i, j))],
        out_shape=[jax.ShapeDtypeStruct((bsz, s, na), BF16),
                   jax.ShapeDtypeStruct((bsz, s, nb), BF16)],
        scratch_shapes=[pltpu.VMEM((tm, d), BF16)],
        compiler_params=_cparams("parallel", "parallel", "arbitrary"),
        name="in_proj",
    )(x, mod3, g, wa, wb)


def _rope_tile(t, cos_m, sin_lo, sin_hi):
    half = MLA_ROPE // 2
    return (t * cos_m + pltpu.roll(t, LANE - half, 1) * sin_lo
            + pltpu.roll(t, half, 1) * sin_hi)


def _mla_prep_kernel(p_ref, pos_ref, invf_ref, qg_ref, kvg_ref, wq_ref, wk_ref, wvt_ref,
                     q_ref, k_ref, vt_ref, *, q_rank, kv_rank, scale):
    half = MLA_ROPE // 2
    ang = pos_ref[...].astype(F32) * invf_ref[...]
    lane = lax.broadcasted_iota(jnp.int32, ang.shape, 1)
    cos, sin = jnp.cos(ang), jnp.sin(ang)
    cos_m = jnp.where(lane < MLA_ROPE, cos, 0.0)
    sin_lo = jnp.where(lane < half, -sin, 0.0)
    sin_hi = jnp.where((lane >= half) & (lane < MLA_ROPE), sin, 0.0)

    q_lat = p_ref[:, 0:q_rank].astype(F32)
    q = _dot(_rms(q_lat, qg_ref[...]).astype(BF16), wq_ref[...])
    kv_lat = p_ref[:, q_rank:q_rank + kv_rank].astype(F32)
    kv_n = _rms(kv_lat, kvg_ref[...]).astype(BF16)
    k_nope = _dot(kv_n, wk_ref[...])
    vt_ref[...] = _dot_nt(wvt_ref[...], kv_n).astype(BF16)
    kr_off = q_rank + kv_rank
    k_rope = _rope_tile(p_ref[:, kr_off:kr_off + LANE].astype(F32),
                        cos_m, sin_lo, sin_hi).astype(BF16)
    for h in range(MLA_HEADS):
        o = h * MLA_QK_PAD
        q_ref[:, o:o + MLA_NOPE] = (q[:, o:o + MLA_NOPE] * scale).astype(BF16)
        qr = _rope_tile(q[:, o + MLA_NOPE:o + MLA_QK_PAD], cos_m, sin_lo, sin_hi)
        q_ref[:, o + MLA_NOPE:o + MLA_QK_PAD] = (qr * scale).astype(BF16)
        k_ref[:, o:o + MLA_NOPE] = k_nope[:, h * MLA_NOPE:(h + 1) * MLA_NOPE].astype(BF16)
        k_ref[:, o + MLA_NOPE:o + MLA_QK_PAD] = k_rope


def _mla_prep(proj, positions, invf, qg, kvg, wq, wk, wvt, q_rank, kv_rank, tm=512):
    bsz, s, blk0 = proj.shape
    hq = MLA_HEADS * MLA_QK_PAD
    hv = MLA_HEADS * MLA_V
    scale = float((MLA_NOPE + MLA_ROPE) ** -0.5 * np.log2(np.e))
    kern = functools.partial(_mla_prep_kernel, q_rank=q_rank, kv_rank=kv_rank, scale=scale)

    def const(a):
        return pl.BlockSpec(a.shape, lambda b, i: (0, 0))

    return pl.pallas_call(
        kern,
        grid=(bsz, s // tm),
        in_specs=[pl.BlockSpec((None, tm, blk0), lambda b, i: (b, i, 0)),
                  pl.BlockSpec((None, tm, 1), lambda b, i: (b, i, 0)),
                  const(invf), const(qg), const(kvg), const(wq), const(wk), const(wvt)],
        out_specs=[pl.BlockSpec((None, tm, hq), lambda b, i: (b, i, 0)),
                   pl.BlockSpec((None, tm, hq), lambda b, i: (b, i, 0)),
                   pl.BlockSpec((None, hv, tm), lambda b, i: (b, 0, i))],
        out_shape=[jax.ShapeDtypeStruct((bsz, s, hq), BF16),
                   jax.ShapeDtypeStruct((bsz, s, hq), BF16),
                   jax.ShapeDtypeStruct((bsz, hv, s), BF16)],
        compiler_params=_cparams("parallel", "parallel"),
        name="mla_prep",
    )(proj, positions.reshape(bsz, s, 1), invf, qg, kvg, wq, wk, wvt)


def _attn_kernel(q_ref, k_ref, vt_ref, o_ref, sa_ref, sb_ref, *, tq, tk):
    s_len = k_ref.shape[0]
    n_kv = s_len // tk
    n_q = s_len // tq
    sub = 8

    def scores(i, s_ref):
        q = q_ref[pl.ds(pl.multiple_of(i * tq, tq), tq), :]
        m8 = jnp.full((sub, tq), -jnp.inf, F32)
        for j in range(n_kv):
            st = _dot_nt(k_ref[j * tk:(j + 1) * tk, :], q)
            s_ref[j * tk:(j + 1) * tk, :] = st
            m8 = jnp.maximum(m8, jnp.max(st.reshape(tk // sub, sub, tq), axis=0))
            yield
        return jnp.broadcast_to(jnp.max(m8, axis=0, keepdims=True), (sub, tq))

    def values(i, s_ref, m8):
        l8 = jnp.zeros((sub, tq), F32)
        acc = jnp.zeros((MLA_V, tq), F32)
        for j in range(n_kv):
            st = s_ref[j * tk:(j + 1) * tk, :].reshape(tk // sub, sub, tq)
            p = jnp.exp2(st - m8[None])
            l8 = l8 + jnp.sum(p, axis=0)
            acc = acc + _dot(vt_ref[:, j * tk:(j + 1) * tk],
                             p.reshape(tk, tq).astype(BF16))
            yield
        l = jnp.sum(l8, axis=0, keepdims=True)
        o_ref[pl.ds(pl.multiple_of(i * tq, tq), tq), :] = (acc / l).T.astype(o_ref.dtype)

    def run(*gens):
        result = None
        live = list(gens)
        while live:
            for g in list(live):
                try:
                    next(g)
                except StopIteration as stop:
                    if g is gens[0]:
                        result = stop.value
                    live.remove(g)
        return result

    def pair(k, m_a):
        m_b = run(scores(2 * k + 1, sb_ref), values(2 * k, sa_ref, m_a))
        return run(scores(2 * k + 2, sa_ref), values(2 * k + 1, sb_ref, m_b))

    m_a = lax.fori_loop(0, n_q // 2 - 1, pair, run(scores(0, sa_ref)))
    m_b = run(scores(n_q - 1, sb_ref), values(n_q - 2, sa_ref, m_a))
    run(values(n_q - 1, sb_ref, m_b))


def _attention(q, k, vt, tq=256, tk=512):
    bsz, s, _ = q.shape
    kern = functools.partial(_attn_kernel, tq=tq, tk=tk)
    return pl.pallas_call(
        kern,
        grid=(bsz, MLA_HEADS),
        in_specs=[pl.BlockSpec((None, s, MLA_QK_PAD), lambda b, h: (b, 0, h)),
                  pl.BlockSpec((None, s, MLA_QK_PAD), lambda b, h: (b, 0, h)),
                  pl.BlockSpec((None, MLA_V, s), lambda b, h: (b, h, 0))],
        out_specs=pl.BlockSpec((None, s, MLA_V), lambda b, h: (b, 0, h)),
        out_shape=jax.ShapeDtypeStruct((bsz, s, MLA_HEADS * MLA_V), BF16),
        scratch_shapes=[pltpu.VMEM((s, tq), F32), pltpu.VMEM((s, tq), F32)],
        compiler_params=_cparams("parallel", "parallel"),
        name="attention",
    )(q, k, vt)


def _split_bf16(x):
    hi = x.astype(BF16)
    return hi, (x - hi.astype(F32)).astype(BF16)


def _hgrn_exact_block(q, k, b, v_ref, qs, states, o_ref, k_s, b_s, v_s, *, forward):
    n, c, _ = q.shape
    k_s[...] = k.reshape(n * c, HGRN_DK)
    b_s[...] = b.reshape(n * c, HGRN_DK)
    v_s[...] = v_ref[...].astype(F32)
    t_idx = lax.broadcasted_iota(jnp.int32, (c, 1), 0)
    for j in range(n):
        def key_row(s, acc, j=j):
            row = pl.ds(j * c + s, 1)
            w = jnp.exp2(jnp.minimum(b[j] - b_s[row, :], 0.0))
            seen = (t_idx >= s) if forward else (t_idx <= s)
            a_col = jnp.sum(jnp.where(seen, q[j] * w * k_s[row, :], 0.0),
                            axis=1, keepdims=True)
            return acc + a_col * v_s[row, :]

        intra = lax.fori_loop(0, c, key_row, jnp.zeros((c, HGRN_DV), F32))
        inter = _dot_nt(qs[j], states[j])
        o_ref[j * c:(j + 1) * c, :] = (inter + intra).astype(o_ref.dtype)


def _hgrn_direction(q_ref, z_ref, v_ref, lb, st_ref, o_ref, exact_scratch, *, forward):
    c = HGRN_CHUNK
    n = q_ref.shape[0] // c
    row = lax.broadcasted_iota(jnp.int32, (c, c), 0)
    col = lax.broadcasted_iota(jnp.int32, (c, c), 1)
    causal = (col <= row) if forward else (col >= row)
    tri = causal.astype(BF16)
    edge = c - 1 if forward else 0
    mid = c // 2

    chunks = [slice(j * c, (j + 1) * c) for j in range(n)]
    f = lb + (1.0 - lb) * jax.nn.sigmoid(z_ref[...].astype(F32))
    k = (1.0 - f).reshape(n, c, HGRN_DK)
    log_f = jnp.concatenate(_split_bf16(jnp.log2(f)), axis=1)
    q = q_ref[...].astype(F32).reshape(n, c, HGRN_DK)
    vt = v_ref[...].astype(F32).T.astype(BF16)
    yield
    b2 = jnp.concatenate([_dot(tri, log_f[r]) for r in chunks], axis=0)
    yield
    b = (b2[:, :HGRN_DK] + b2[:, HGRN_DK:]).reshape(n, c, HGRN_DK)
    b_edge = b[:, edge:edge + 1]
    b_mid = b[:, mid:mid + 1]
    qs = (q * jnp.exp2(b)).astype(BF16)
    qm = (q * jnp.exp2(b - b_mid)).astype(BF16)
    km = (k * jnp.exp2(b_mid - b)).astype(BF16)
    kd = (k * jnp.exp2(b_edge - b)).astype(BF16)
    decay = jnp.exp2(b_edge)
    spread = jnp.max(jnp.abs(b[:, 0:1] - b[:, c - 1:c]))
    yield
    a = [_dot_nt(qm[j], km[j]) for j in range(n)]
    upd = [_dot(vt[:, chunks[j]], kd[j]) for j in range(n)]
    yield
    a = [jnp.where(causal, x, 0.0).astype(BF16) for x in a]
    st = st_ref[...]
    states = [None] * n
    for j in (range(n) if forward else reversed(range(n))):
        states[j] = st.astype(BF16)
        st = st * decay[j] + upd[j]
    st_ref[...] = st
    yield
    outs = [_dot_nt(jnp.concatenate([a[j], qs[j]], axis=1),
                    jnp.concatenate([vt[:, chunks[j]], states[j]], axis=1))
            for j in range(n)]
    o_ref[...] = jnp.concatenate(outs, axis=0).astype(o_ref.dtype)
    yield

    @pl.when(spread > HGRN_SAFE_LOG2_SPREAD)
    def _():
        _hgrn_exact_block(q, k, b, v_ref, qs, states, o_ref, *exact_scratch,
                          forward=forward)
    yield


def _hgrn_kernel(qf_ref, zf_ref, vf_ref, qb_ref, zb_ref, vb_ref, lbf_ref, lbb_ref,
                 of_ref, ob_ref, stf_ref, stb_ref, k_s, b_s, v_s):
    @pl.when(pl.program_id(2) == 0)
    def _():
        stf_ref[...] = jnp.zeros_like(stf_ref)
        stb_ref[...] = jnp.zeros_like(stb_ref)

    def lower_bound(ref):
        l = ref[...]
        e = jnp.exp(l - jnp.max(l, axis=0, keepdims=True))
        return e[0:1, :] / jnp.sum(e, axis=0, keepdims=True)

    fwd = _hgrn_direction(qf_ref, zf_ref, vf_ref, lower_bound(lbf_ref), stf_ref, of_ref,
                          (k_s, b_s, v_s), forward=True)
    bwd = _hgrn_direction(qb_ref, zb_ref, vb_ref, lower_bound(lbb_ref), stb_ref, ob_ref,
                          (k_s, b_s, v_s), forward=False)
    for _ in zip(fwd, bwd):
        pass


def _hgrn(proj, lbf_logits, lbb_logits, col_q, col_ff, col_fb, col_v, tb=1024):
    bsz, s, _ = proj.shape
    nb = s // tb
    width = HGRN_HEADS * HGRN_DV

    def fwd(col):
        return pl.BlockSpec((None, tb, LANE), lambda b, h, i: (b, i, col + h))

    def bwd(col):
        return pl.BlockSpec((None, tb, LANE), lambda b, h, i: (b, nb - 1 - i, col + h))

    lb_spec = pl.BlockSpec((lbf_logits.shape[0], LANE), lambda b, h, i: (0, h))
    return pl.pallas_call(
        _hgrn_kernel,
        grid=(bsz, HGRN_HEADS, nb),
        in_specs=[fwd(col_q), fwd(col_ff), fwd(col_v),
                  bwd(col_q), bwd(col_fb), bwd(col_v), lb_spec, lb_spec],
        out_specs=[pl.BlockSpec((None, tb, LANE), lambda b, h, i: (b, i, h)),
                   pl.BlockSpec((None, tb, LANE), lambda b, h, i: (b, nb - 1 - i, h))],
        out_shape=[jax.ShapeDtypeStruct((bsz, s, width), F32),
                   jax.ShapeDtypeStruct((bsz, s, width), F32)],
        scratch_shapes=[pltpu.VMEM((HGRN_DV, HGRN_DK), F32),
                        pltpu.VMEM((HGRN_DV, HGRN_DK), F32),
                        pltpu.VMEM((tb, HGRN_DK), F32), pltpu.VMEM((tb, HGRN_DK), F32),
                        pltpu.VMEM((tb, HGRN_DV), F32)],
        compiler_params=_cparams("parallel", "parallel", "arbitrary"),
        name="hgrn",
    )(proj, proj, proj, proj, proj, proj, lbf_logits, lbb_logits)


def _mix_out_kernel(attn_ref, of_ref, ob_ref, hg_ref, hgn_ref, w_ref, x_ref, mod_ref,
                    postg_ref, preg_ref, x1_ref, h2_ref):
    na = attn_ref.shape[-1]
    tm = attn_ref.shape[0]
    tiles = [slice(r, r + MIX_SUB_ROWS) for r in range(0, tm, MIX_SUB_ROWS)]
    recs = []
    for t in tiles:
        o = of_ref[t, :] + ob_ref[t, :]
        hg = hg_ref[t, :].astype(F32)
        gate = hg * jax.nn.sigmoid(hg)
        rec = []
        for h in range(HGRN_HEADS):
            sl = slice(h * HGRN_DV, (h + 1) * HGRN_DV)
            rec.append((_rms(o[:, sl], hgn_ref[...]) * gate[:, sl]).astype(BF16))
        recs.append(jnp.concatenate(rec, axis=-1))
    mixes = [_dot(attn_ref[t, :], w_ref[0:na, :]) + _dot(rec, w_ref[na:, :])
             for t, rec in zip(tiles, recs)]
    for t, mix in zip(tiles, mixes):
        x1 = x_ref[t, :] + mod_ref[2:3, :] * _rms(mix, postg_ref[...])
        x1_ref[t, :] = x1
        h2 = _rms(x1, preg_ref[...]) * (1.0 + mod_ref[4:5, :]) + mod_ref[3:4, :]
        h2_ref[t, :] = h2.astype(BF16)


def _mix_out(attn, o_f, o_b, proj, col_hg, hgn_g, w_out, x, mod3, post_g, pre_g, tm=512):
    bsz, s, d = x.shape
    na = attn.shape[-1]
    nr = o_f.shape[-1]

    def row(width):
        return pl.BlockSpec((None, tm, width), lambda b, i: (b, i, 0))

    def const(shape):
        return pl.BlockSpec(shape, lambda b, i: (0,) * len(shape))

    return pl.pallas_call(
        _mix_out_kernel,
        grid=(bsz, s // tm),
        in_specs=[row(na), row(nr), row(nr),
                  pl.BlockSpec((None, tm, nr), lambda b, i: (b, i, col_hg)),
                  const((1, HGRN_DV)), const(w_out.shape), row(d),
                  pl.BlockSpec((None, N_MOD, d), lambda b, i: (b, 0, 0)),
                  const((1, d)), const((1, d))],
        out_specs=[row(d), row(d)],
        out_shape=[jax.ShapeDtypeStruct((bsz, s, d), F32),
                   jax.ShapeDtypeStruct((bsz, s, d), BF16)],
        compiler_params=_cparams("parallel", "parallel"),
        name="mix_out",
    )(attn, o_f, o_b, proj, hgn_g, w_out, x, mod3, post_g, pre_g)


def _mlp_kernel(h_ref, wu_ref, wd_ref, x1_ref, mod_ref, g_ref, o_ref):
    j = pl.program_id(2)

    @pl.when(j == 0)
    def _():
        o_ref[...] = jnp.zeros_like(o_ref)

    u = jnp.maximum(_dot(h_ref[...], wu_ref[...]), 0.0)
    o_ref[...] += _dot((u * u).astype(BF16), wd_ref[...])

    @pl.when(j == pl.num_programs(2) - 1)
    def _():
        o_ref[...] = x1_ref[...] + mod_ref[5:6, :] * _rms(o_ref[...], g_ref[...])


def _mlp(h2, w_up, w_down, x1, mod3, g, tm=512, tf=1024):
    bsz, s, d = x1.shape
    dff = w_up.shape[1]
    return pl.pallas_call(
        _mlp_kernel,
        grid=(bsz, s // tm, dff // tf),
        in_specs=[pl.BlockSpec((None, tm, d), lambda b, i, j: (b, i, 0)),
                  pl.BlockSpec((d, tf), lambda b, i, j: (0, j)),
                  pl.BlockSpec((tf, d), lambda b, i, j: (j, 0)),
                  pl.BlockSpec((None, tm, d), lambda b, i, j: (b, i, 0)),
                  pl.BlockSpec((None, N_MOD, d), lambda b, i, j: (b, 0, 0)),
                  pl.BlockSpec((1, d), lambda b, i, j: (0, 0))],
        out_specs=pl.BlockSpec((None, tm, d), lambda b, i, j: (b, i, 0)),
        out_shape=jax.ShapeDtypeStruct((bsz, s, d), F32),
        compiler_params=_cparams("parallel", "parallel", "arbitrary"),
        name="mlp",
    )(h2, w_up, w_down, x1, mod3, g)


def _pack_w_in(w, q_rank, kv_rank):
    head = q_rank + kv_rank + MLA_ROPE
    wa = jnp.pad(w[:, :head].astype(BF16), ((0, 0), (0, MLA_LAT_PAD - head)))
    return wa, w[:, head:].astype(BF16)


def _pack_w_uq(w):
    r = w.shape[0]
    w = w.reshape(r, MLA_HEADS, MLA_NOPE + MLA_ROPE)
    pad = jnp.zeros((r, MLA_HEADS, MLA_QK_PAD - MLA_NOPE - MLA_ROPE), w.dtype)
    return jnp.concatenate([w, pad], axis=-1).reshape(r, MLA_HEADS * MLA_QK_PAD).astype(BF16)


def _pack_w_ukv(w):
    r = w.shape[0]
    w = w.reshape(r, MLA_HEADS, MLA_NOPE + MLA_V)
    wk = w[:, :, :MLA_NOPE].reshape(r, -1).astype(BF16)
    wvt = w[:, :, MLA_NOPE:].reshape(r, -1).T.astype(BF16)
    return wk, wvt


def _rope_inv_freq():
    half = MLA_ROPE // 2
    f = ROPE_BASE ** (-jnp.arange(half, dtype=F32) / half)
    return jnp.concatenate([f, f, jnp.zeros((LANE - MLA_ROPE,), F32)]).reshape(1, LANE)


def kernel(x, c, positions, w_mod, b_mod, pre_mix_g, post_mix_g, pre_mlp_g, post_mlp_g,
           w_in, q_norm_g, kv_norm_g, w_uq, w_ukv, hgrn_norm_g, hgrn_lb_logits_fwd,
           hgrn_lb_logits_bwd, w_out, w_up, w_down):
    depth = w_mod.shape[0]
    assert depth == 1, "lower-bound slot selection is written for a single layer"
    bsz, s, d = x.shape
    q_rank = q_norm_g.shape[-1]
    kv_rank = kv_norm_g.shape[-1]
    invf = _rope_inv_freq()
    for layer in range(depth):
        mod3 = _mod(c, w_mod[layer], b_mod[layer]).reshape(bsz, N_MOD, d)
        lat, proj = _in_proj(x, mod3, pre_mix_g[layer].reshape(1, d),
                             *_pack_w_in(w_in[layer], q_rank, kv_rank))
        wk, wvt = _pack_w_ukv(w_ukv[layer])
        q, k, vt = _mla_prep(lat, positions, invf, q_norm_g[layer].reshape(1, -1),
                             kv_norm_g[layer].reshape(1, -1), _pack_w_uq(w_uq[layer]),
                             wk, wvt, q_rank, kv_rank)
        attn = _attention(q, k, vt)
        o_f, o_b = _hgrn(proj, hgrn_lb_logits_fwd, hgrn_lb_logits_bwd,
                         col_q=0, col_ff=8, col_fb=16, col_v=24)
        x1, h2 = _mix_out(attn, o_f, o_b, proj, 4, hgrn_norm_g[layer].reshape(1, -1),
                          w_out[layer].astype(BF16), x, mod3,
                          post_mix_g[layer].reshape(1, d), pre_mlp_g[layer].reshape(1, d))
        x = _mlp(h2, w_up[layer].astype(BF16), w_down[layer].astype(BF16), x1, mod3,
                 post_mlp_g[layer].reshape(1, d))
    return x
```

```python
import functools

import jax
import jax.numpy as jnp
import numpy as np
from jax import lax
from jax.experimental import pallas as pl
from jax.experimental.pallas import tpu as pltpu

F32 = jnp.float32
BF16 = jnp.bfloat16

NORM_EPS = 1e-6
ROPE_BASE = 10000.0
N_MOD = 6
LANE = 128
VMEM_LIMIT = 56 * 1024 * 1024

MLA_HEADS = 8
MLA_NOPE = 128
MLA_ROPE = 64
MLA_V = 128
MLA_QK_PAD = 256
ATTN_TILES_PER_TRIP = 4
MLA_LAT_PAD = 1024
HGRN_HEADS = 8
HGRN_DK = 128
HGRN_DV = 128
HGRN_CHUNK = 64
HGRN_SAFE_LOG2_SPREAD = 96.0
MIX_SUB_ROWS = 128


def _cparams(*sem):
    return pltpu.CompilerParams(dimension_semantics=sem, vmem_limit_bytes=VMEM_LIMIT)


def _dot(a, b):
    return jnp.dot(a, b, preferred_element_type=F32)


def _dot_nt(a, b):
    return lax.dot_general(a, b, (((1,), (1,)), ((), ())), preferred_element_type=F32)


def _dot_tn(a, b):
    return lax.dot_general(a, b, (((0,), (0,)), ((), ())), preferred_element_type=F32)


def _rms(x, g):
    return x * lax.rsqrt(jnp.mean(x * x, axis=-1, keepdims=True) + NORM_EPS) * g


def _mod_kernel(c_ref, w_ref, b_ref, o_ref):
    c = c_ref[...]
    cond = c * jax.nn.sigmoid(c)
    hi = cond.astype(BF16)
    lo = (cond - hi.astype(F32)).astype(BF16)
    w = w_ref[...].astype(BF16)
    o_ref[...] = _dot(hi, w) + _dot(lo, w) + b_ref[...]


def _mod(c, w_mod, b_mod, tn=1024):
    bsz, d = c.shape
    n = w_mod.shape[1]
    return pl.pallas_call(
        _mod_kernel,
        grid=(n // tn,),
        in_specs=[pl.BlockSpec((bsz, d), lambda j: (0, 0)),
                  pl.BlockSpec((d, tn), lambda j: (0, j)),
                  pl.BlockSpec((1, tn), lambda j: (0, j))],
        out_specs=pl.BlockSpec((bsz, tn), lambda j: (0, j)),
        out_shape=jax.ShapeDtypeStruct((bsz, n), F32),
        compiler_params=_cparams("parallel"),
        name="mod",
    )(c, w_mod, b_mod.reshape(1, n))


def _in_proj_kernel(x_ref, mod_ref, g_ref, wa_ref, wb_ref, oa_ref, ob_ref, h_ref):
    @pl.when(pl.program_id(2) == 0)
    def _():
        h = _rms(x_ref[...], g_ref[...]) * (1.0 + mod_ref[1:2, :]) + mod_ref[0:1, :]
        h = h.astype(BF16)
        h_ref[...] = h
        oa_ref[...] = _dot(h, wa_ref[...]).astype(oa_ref.dtype)

    y = _dot(h_ref[...], wb_ref[...]).astype(ob_ref.dtype)
    for c in range(ob_ref.shape[0]):
        ob_ref[c] = y[:, c * LANE:(c + 1) * LANE]


def _in_proj(x, mod3, g, wa, wb, tm=1024, tn=1280):
    bsz, s, d = x.shape
    na, nb = wa.shape[1], wb.shape[1]
    return pl.pallas_call(
        _in_proj_kernel,
        grid=(bsz, s // tm, nb // tn),
        in_specs=[pl.BlockSpec((None, tm, d), lambda b, i, j: (b, i, 0)),
                  pl.BlockSpec((None, N_MOD, d), lambda b, i, j: (b, 0, 0)),
                  pl.BlockSpec((1, d), lambda b, i, j: (0, 0)),
                  pl.BlockSpec((d, na), lambda b, i, j: (0, 0)),
                  pl.BlockSpec((d, tn), lambda b, i, j: (0, j))],
        out_specs=[pl.BlockSpec((None, tm, na), lambda b, i, j: (b, i, 0)),
                   pl.BlockSpec((None, tn // LANE, tm, LANE), lambda b, i, j: (b, j, i, 0))],
        out_shape=[jax.ShapeDtypeStruct((bsz, s, na), BF16),
                   jax.ShapeDtypeStruct((bsz, nb // LANE, s, LANE), BF16)],
        scratch_shapes=[pltpu.VMEM((tm, d), BF16)],
        compiler_params=_cparams("parallel", "parallel", "arbitrary"),
        name="in_proj",
    )(x, mod3, g, wa, wb)


def _rope_tile(t, cos_m, sin_lo, sin_hi):
    half = MLA_ROPE // 2
    return (t * cos_m + pltpu.roll(t, LANE - half, 1) * sin_lo
            + pltpu.roll(t, half, 1) * sin_hi)


def _mla_prep_kernel(p_ref, pos_ref, invf_ref, qg_ref, kvg_ref, wq_ref, wk_ref, wvt_ref,
                     q_ref, k_ref, vt_ref, *, q_rank, kv_rank, scale):
    half = MLA_ROPE // 2
    ang = pos_ref[...].astype(F32) * invf_ref[...]
    lane = lax.broadcasted_iota(jnp.int32, ang.shape, 1)
    cos, sin = jnp.cos(ang), jnp.sin(ang)
    cos_m = jnp.where(lane < MLA_ROPE, cos, 0.0)
    sin_lo = jnp.where(lane < half, -sin, 0.0)
    sin_hi = jnp.where((lane >= half) & (lane < MLA_ROPE), sin, 0.0)

    q_lat = p_ref[:, 0:q_rank].astype(F32)
    q = _dot(_rms(q_lat, qg_ref[...]).astype(BF16), wq_ref[...])
    kv_lat = p_ref[:, q_rank:q_rank + kv_rank].astype(F32)
    kv_n = _rms(kv_lat, kvg_ref[...]).astype(BF16)
    k_nope = _dot(kv_n, wk_ref[...])
    vt_ref[...] = _dot_nt(wvt_ref[...], kv_n).astype(BF16)
    kr_off = q_rank + kv_rank
    k_rope = _rope_tile(p_ref[:, kr_off:kr_off + LANE].astype(F32),
                        cos_m, sin_lo, sin_hi).astype(BF16)
    for h in range(MLA_HEADS):
        o = h * MLA_QK_PAD
        q_ref[:, o:o + MLA_NOPE] = (q[:, o:o + MLA_NOPE] * scale).astype(BF16)
        qr = _rope_tile(q[:, o + MLA_NOPE:o + MLA_QK_PAD], cos_m, sin_lo, sin_hi)
        q_ref[:, o + MLA_NOPE:o + MLA_QK_PAD] = (qr * scale).astype(BF16)
        k_ref[:, o:o + MLA_NOPE] = k_nope[:, h * MLA_NOPE:(h + 1) * MLA_NOPE].astype(BF16)
        k_ref[:, o + MLA_NOPE:o + MLA_QK_PAD] = k_rope


def _mla_prep(proj, positions, invf, qg, kvg, wq, wk, wvt, q_rank, kv_rank, tm=512):
    bsz, s, blk0 = proj.shape
    hq = MLA_HEADS * MLA_QK_PAD
    hv = MLA_HEADS * MLA_V
    scale = float((MLA_NOPE + MLA_ROPE) ** -0.5 * np.log2(np.e))
    kern = functools.partial(_mla_prep_kernel, q_rank=q_rank, kv_rank=kv_rank, scale=scale)

    def const(a):
        return pl.BlockSpec(a.shape, lambda b, i: (0, 0))

    return pl.pallas_call(
        kern,
        grid=(bsz, s // tm),
        in_specs=[pl.BlockSpec((None, tm, blk0), lambda b, i: (b, i, 0)),
                  pl.BlockSpec((None, tm, 1), lambda b, i: (b, i, 0)),
                  const(invf), const(qg), const(kvg), const(wq), const(wk), const(wvt)],
        out_specs=[pl.BlockSpec((None, tm, hq), lambda b, i: (b, i, 0)),
                   pl.BlockSpec((None, tm, hq), lambda b, i: (b, i, 0)),
                   pl.BlockSpec((None, hv, tm), lambda b, i: (b, 0, i))],
        out_shape=[jax.ShapeDtypeStruct((bsz, s, hq), BF16),
                   jax.ShapeDtypeStruct((bsz, s, hq), BF16),
                   jax.ShapeDtypeStruct((bsz, hv, s), BF16)],
        compiler_params=_cparams("parallel", "parallel"),
        name="mla_prep",
    )(proj, positions.reshape(bsz, s, 1), invf, qg, kvg, wq, wk, wvt)


def _attn_kernel(q_ref, k_ref, vt_ref, o_ref, sa_ref, sb_ref, *, tq, tk):
    s_len = k_ref.shape[0]
    n_kv = s_len // tk
    n_q = s_len // tq
    sub = 8

    def scores(i, s_ref):
        q = q_ref[pl.ds(pl.multiple_of(i * tq, tq), tq), :]
        m8 = jnp.full((sub, tq), -jnp.inf, F32)
        for j in range(n_kv):
            st = _dot_nt(k_ref[j * tk:(j + 1) * tk, :], q)
            s_ref[j * tk:(j + 1) * tk, :] = st
            m8 = jnp.maximum(m8, jnp.max(st.reshape(tk // sub, sub, tq), axis=0))
            yield
        return jnp.broadcast_to(jnp.max(m8, axis=0, keepdims=True), (sub, tq))

    def values(i, s_ref, m8):
        l8 = jnp.zeros((sub, tq), F32)
        acc = jnp.zeros((MLA_V, tq), F32)
        for j in range(n_kv):
            st = s_ref[j * tk:(j + 1) * tk, :].reshape(tk // sub, sub, tq)
            p = jnp.exp2(st - m8[None])
            l8 = l8 + jnp.sum(p, axis=0)
            acc = acc + _dot(vt_ref[:, j * tk:(j + 1) * tk],
                             p.reshape(tk, tq).astype(BF16))
            yield
        l = jnp.sum(l8, axis=0, keepdims=True)
        o_ref[pl.ds(pl.multiple_of(i * tq, tq), tq), :] = (acc / l).T.astype(o_ref.dtype)

    def run(*gens):
        result = None
        live = list(gens)
        while live:
            for g in list(live):
                try:
                    next(g)
                except StopIteration as stop:
                    if g is gens[0]:
                        result = stop.value
                    live.remove(g)
        return result

    bufs = (sa_ref, sb_ref)
    group = ATTN_TILES_PER_TRIP

    def tiles(first, count, m):
        for u in range(count):
            m = run(scores(first + u + 1, bufs[(u + 1) % 2]),
                    values(first + u, bufs[u % 2], m))
        return m

    m = lax.fori_loop(0, n_q // group - 1, lambda k, m: tiles(group * k, group, m),
                      run(scores(0, bufs[0])))
    m = tiles(n_q - group, group - 1, m)
    run(values(n_q - 1, bufs[(group - 1) % 2], m))


def _attention(q, k, vt, tq=256, tk=512):
    bsz, s, _ = q.shape
    kern = functools.partial(_attn_kernel, tq=tq, tk=tk)
    return pl.pallas_call(
        kern,
        grid=(bsz, MLA_HEADS),
        in_specs=[pl.BlockSpec((None, s, MLA_QK_PAD), lambda b, h: (b, 0, h)),
                  pl.BlockSpec((None, s, MLA_QK_PAD), lambda b, h: (b, 0, h)),
                  pl.BlockSpec((None, MLA_V, s), lambda b, h: (b, h, 0))],
        out_specs=pl.BlockSpec((None, s, MLA_V), lambda b, h: (b, 0, h)),
        out_shape=jax.ShapeDtypeStruct((bsz, s, MLA_HEADS * MLA_V), BF16),
        scratch_shapes=[pltpu.VMEM((s, tq), F32), pltpu.VMEM((s, tq), F32)],
        compiler_params=_cparams("parallel", "parallel"),
        name="attention",
    )(q, k, vt)


def _split_bf16(x):
    hi = x.astype(BF16)
    return hi, (x - hi.astype(F32)).astype(BF16)


def _hgrn_exact_block(q, k, b, v_ref, qs, states, o_ref, k_s, b_s, v_s, *, forward):
    n, c, _ = q.shape
    k_s[...] = k.reshape(n * c, HGRN_DK)
    b_s[...] = b.reshape(n * c, HGRN_DK)
    v_s[...] = v_ref[...].astype(F32)
    t_idx = lax.broadcasted_iota(jnp.int32, (c, 1), 0)
    for j in range(n):
        def key_row(s, acc, j=j):
            row = pl.ds(j * c + s, 1)
            w = jnp.exp2(jnp.minimum(b[j] - b_s[row, :], 0.0))
            seen = (t_idx >= s) if forward else (t_idx <= s)
            a_col = jnp.sum(jnp.where(seen, q[j] * w * k_s[row, :], 0.0),
                            axis=1, keepdims=True)
            return acc + a_col * v_s[row, :]

        intra = lax.fori_loop(0, c, key_row, jnp.zeros((c, HGRN_DV), F32))
        inter = _dot_nt(qs[j], states[j])
        o_ref[j * c:(j + 1) * c, :] = (inter + intra).astype(o_ref.dtype)


def _hgrn_direction(q_ref, z_ref, v_ref, lb, st_ref, o_ref, exact_scratch, *, forward):
    c = HGRN_CHUNK
    n = q_ref.shape[0] // c
    row = lax.broadcasted_iota(jnp.int32, (c, c), 0)
    col = lax.broadcasted_iota(jnp.int32, (c, c), 1)
    causal = (col <= row) if forward else (col >= row)
    tri = causal.astype(BF16)
    edge = c - 1 if forward else 0
    mid = c // 2

    chunks = [slice(j * c, (j + 1) * c) for j in range(n)]
    f = lb + (1.0 - lb) * jax.nn.sigmoid(z_ref[...].astype(F32))
    k = (1.0 - f).reshape(n, c, HGRN_DK)
    log_f = jnp.concatenate(_split_bf16(jnp.log2(f)), axis=1)
    q = q_ref[...].astype(F32).reshape(n, c, HGRN_DK)
    vt = v_ref[...].astype(F32).T.astype(BF16)
    yield
    b2 = jnp.concatenate([_dot(tri, log_f[r]) for r in chunks], axis=0)
    yield
    b = (b2[:, :HGRN_DK] + b2[:, HGRN_DK:]).reshape(n, c, HGRN_DK)
    b_edge = b[:, edge:edge + 1]
    b_mid = b[:, mid:mid + 1]
    qs = (q * jnp.exp2(b)).astype(BF16)
    qm = (q * jnp.exp2(b - b_mid)).astype(BF16)
    km = (k * jnp.exp2(b_mid - b)).astype(BF16)
    kd = (k * jnp.exp2(b_edge - b)).astype(BF16)
    decay = jnp.exp2(b_edge)
    spread = jnp.max(jnp.abs(b[:, 0:1] - b[:, c - 1:c]))
    yield
    a = [_dot_nt(qm[j], km[j]) for j in range(n)]
    upd = [_dot(vt[:, chunks[j]], kd[j]) for j in range(n)]
    yield
    a = [jnp.where(causal, x, 0.0).astype(BF16) for x in a]
    st = st_ref[...]
    states = [None] * n
    for j in (range(n) if forward else reversed(range(n))):
        states[j] = st.astype(BF16)
        st = st * decay[j] + upd[j]
    st_ref[...] = st
    yield
    outs = [_dot_nt(jnp.concatenate([a[j], qs[j]], axis=1),
                    jnp.concatenate([vt[:, chunks[j]], states[j]], axis=1))
            for j in range(n)]
    o_ref[...] = jnp.concatenate(outs, axis=0).astype(o_ref.dtype)
    yield

    @pl.when(spread > HGRN_SAFE_LOG2_SPREAD)
    def _():
        _hgrn_exact_block(q, k, b, v_ref, qs, states, o_ref, *exact_scratch,
                          forward=forward)
    yield


def _hgrn_kernel(qf_ref, zf_ref, vf_ref, qb_ref, zb_ref, vb_ref, lbf_ref, lbb_ref,
                 of_ref, ob_ref, stf_ref, stb_ref, k_s, b_s, v_s):
    @pl.when(pl.program_id(2) == 0)
    def _():
        stf_ref[...] = jnp.zeros_like(stf_ref)
        stb_ref[...] = jnp.zeros_like(stb_ref)

    def lower_bound(ref):
        l = ref[...]
        e = jnp.exp(l - jnp.max(l, axis=0, keepdims=True))
        return e[0:1, :] / jnp.sum(e, axis=0, keepdims=True)

    fwd = _hgrn_direction(qf_ref, zf_ref, vf_ref, lower_bound(lbf_ref), stf_ref, of_ref,
                          (k_s, b_s, v_s), forward=True)
    bwd = _hgrn_direction(qb_ref, zb_ref, vb_ref, lower_bound(lbb_ref), stb_ref, ob_ref,
                          (k_s, b_s, v_s), forward=False)
    for _ in zip(fwd, bwd):
        pass


def _hgrn(proj, lbf_logits, lbb_logits, col_q, col_ff, col_fb, col_v, tb=1024):
    bsz, _, s, _ = proj.shape
    nb = s // tb

    def fwd(col):
        return pl.BlockSpec((None, None, tb, LANE), lambda b, h, i: (b, col + h, i, 0))

    def bwd(col):
        return pl.BlockSpec((None, None, tb, LANE),
                            lambda b, h, i: (b, col + h, nb - 1 - i, 0))

    lb_spec = pl.BlockSpec((lbf_logits.shape[0], LANE), lambda b, h, i: (0, h))
    out_shape = jax.ShapeDtypeStruct((bsz, HGRN_HEADS, s, HGRN_DV), BF16)
    return pl.pallas_call(
        _hgrn_kernel,
        grid=(bsz, HGRN_HEADS, nb),
        in_specs=[fwd(col_q), fwd(col_ff), fwd(col_v),
                  bwd(col_q), bwd(col_fb), bwd(col_v), lb_spec, lb_spec],
        out_specs=[fwd(0), bwd(0)],
        out_shape=[out_shape, out_shape],
        scratch_shapes=[pltpu.VMEM((HGRN_DV, HGRN_DK), F32),
                        pltpu.VMEM((HGRN_DV, HGRN_DK), F32),
                        pltpu.VMEM((tb, HGRN_DK), F32), pltpu.VMEM((tb, HGRN_DK), F32),
                        pltpu.VMEM((tb, HGRN_DV), F32)],
        compiler_params=_cparams("parallel", "parallel", "arbitrary"),
        name="hgrn",
    )(proj, proj, proj, proj, proj, proj, lbf_logits, lbb_logits)


def _mix_out_kernel(attn_ref, of_ref, ob_ref, hg_ref, hgn_ref, w_ref, x_ref, mod_ref,
                    postg_ref, preg_ref, x1_ref, h2_ref):
    na = attn_ref.shape[-1]
    tm = attn_ref.shape[0]
    tiles = [slice(r, r + MIX_SUB_ROWS) for r in range(0, tm, MIX_SUB_ROWS)]
    recs = []
    for t in tiles:
        rec = []
        for h in range(HGRN_HEADS):
            o = of_ref[h, t, :].astype(F32) + ob_ref[h, t, :].astype(F32)
            hg = hg_ref[h, t, :].astype(F32)
            rec.append((_rms(o, hgn_ref[...]) * (hg * jax.nn.sigmoid(hg))).astype(BF16))
        recs.append(jnp.concatenate(rec, axis=-1))
    mixes = [_dot(attn_ref[t, :], w_ref[0:na, :]) + _dot(rec, w_ref[na:, :])
             for t, rec in zip(tiles, recs)]
    post_gain = mod_ref[2:3, :] * postg_ref[...]
    pre_gain = preg_ref[...] * (1.0 + mod_ref[4:5, :])
    for t, mix in zip(tiles, mixes):
        x1 = x_ref[t, :] + _rms(mix, post_gain)
        x1_ref[t, :] = x1
        h2_ref[t, :] = (_rms(x1, pre_gain) + mod_ref[3:4, :]).astype(BF16)


def _mix_out(attn, o_f, o_b, proj, col_hg, hgn_g, w_out, x, mod3, post_g, pre_g, tm=512):
    bsz, s, d = x.shape
    na = attn.shape[-1]
    heads = o_f.shape[1]

    def row(width):
        return pl.BlockSpec((None, tm, width), lambda b, i: (b, i, 0))

    def per_head(first):
        return pl.BlockSpec((None, heads, tm, LANE), lambda b, i: (b, first // heads, i, 0))

    def const(shape):
        return pl.BlockSpec(shape, lambda b, i: (0,) * len(shape))

    assert col_hg % heads == 0
    return pl.pallas_call(
        _mix_out_kernel,
        grid=(bsz, s // tm),
        in_specs=[row(na), per_head(0), per_head(0), per_head(col_hg),
                  const((1, HGRN_DV)), const(w_out.shape), row(d),
                  pl.BlockSpec((None, N_MOD, d), lambda b, i: (b, 0, 0)),
                  const((1, d)), const((1, d))],
        out_specs=[row(d), row(d)],
        out_shape=[jax.ShapeDtypeStruct((bsz, s, d), F32),
                   jax.ShapeDtypeStruct((bsz, s, d), BF16)],
        compiler_params=_cparams("parallel", "parallel"),
        name="mix_out",
    )(attn, o_f, o_b, proj, hgn_g, w_out, x, mod3, post_g, pre_g)


def _mlp_kernel(h_ref, wu_ref, wd_ref, x1_ref, mod_ref, g_ref, o_ref):
    j = pl.program_id(2)

    @pl.when(j == 0)
    def _():
        o_ref[...] = jnp.zeros_like(o_ref)

    u = jnp.maximum(_dot(h_ref[...], wu_ref[...]), 0.0)
    o_ref[...] += _dot((u * u).astype(BF16), wd_ref[...])

    @pl.when(j == pl.num_programs(2) - 1)
    def _():
        o_ref[...] = x1_ref[...] + mod_ref[5:6, :] * _rms(o_ref[...], g_ref[...])


def _mlp(h2, w_up, w_down, x1, mod3, g, tm=512, tf=1024):
    bsz, s, d = x1.shape
    dff = w_up.shape[1]
    return pl.pallas_call(
        _mlp_kernel,
        grid=(bsz, s // tm, dff // tf),
        in_specs=[pl.BlockSpec((None, tm, d), lambda b, i, j: (b, i, 0)),
                  pl.BlockSpec((d, tf), lambda b, i, j: (0, j)),
                  pl.BlockSpec((tf, d), lambda b, i, j: (j, 0)),
                  pl.BlockSpec((None, tm, d), lambda b, i, j: (b, i, 0)),
                  pl.BlockSpec((None, N_MOD, d), lambda b, i, j: (b, 0, 0)),
                  pl.BlockSpec((1, d), lambda b, i, j: (0, 0))],
        out_specs=pl.BlockSpec((None, tm, d), lambda b, i, j: (b, i, 0)),
        out_shape=jax.ShapeDtypeStruct((bsz, s, d), F32),
        compiler_params=_cparams("parallel", "parallel", "arbitrary"),
        name="mlp",
    )(h2, w_up, w_down, x1, mod3, g)


def _pack_w_in(w, q_rank, kv_rank):
    head = q_rank + kv_rank + MLA_ROPE
    wa = jnp.pad(w[:, :head].astype(BF16), ((0, 0), (0, MLA_LAT_PAD - head)))
    return wa, w[:, head:].astype(BF16)


def _pack_w_uq(w):
    r = w.shape[0]
    w = w.reshape(r, MLA_HEADS, MLA_NOPE + MLA_ROPE)
    pad = jnp.zeros((r, MLA_HEADS, MLA_QK_PAD - MLA_NOPE - MLA_ROPE), w.dtype)
    return jnp.concatenate([w, pad], axis=-1).reshape(r, MLA_HEADS * MLA_QK_PAD).astype(BF16)


def _pack_w_ukv(w):
    r = w.shape[0]
    w = w.reshape(r, MLA_HEADS, MLA_NOPE + MLA_V)
    wk = w[:, :, :MLA_NOPE].reshape(r, -1).astype(BF16)
    wvt = w[:, :, MLA_NOPE:].reshape(r, -1).T.astype(BF16)
    return wk, wvt


def _rope_inv_freq():
    half = MLA_ROPE // 2
    f = ROPE_BASE ** (-jnp.arange(half, dtype=F32) / half)
    return jnp.concatenate([f, f, jnp.zeros((LANE - MLA_ROPE,), F32)]).reshape(1, LANE)


def kernel(x, c, positions, w_mod, b_mod, pre_mix_g, post_mix_g, pre_mlp_g, post_mlp_g,
           w_in, q_norm_g, kv_norm_g, w_uq, w_ukv, hgrn_norm_g, hgrn_lb_logits_fwd,
           hgrn_lb_logits_bwd, w_out, w_up, w_down):
    depth = w_mod.shape[0]
    assert depth == 1, "lower-bound slot selection is written for a single layer"
    bsz, s, d = x.shape
    q_rank = q_norm_g.shape[-1]
    kv_rank = kv_norm_g.shape[-1]
    invf = _rope_inv_freq()
    for layer in range(depth):
        mod3 = _mod(c, w_mod[layer], b_mod[layer]).reshape(bsz, N_MOD, d)
        lat, proj = _in_proj(x, mod3, pre_mix_g[layer].reshape(1, d),
                             *_pack_w_in(w_in[layer], q_rank, kv_rank))
        wk, wvt = _pack_w_ukv(w_ukv[layer])
        q, k, vt = _mla_prep(lat, positions, invf, q_norm_g[layer].reshape(1, -1),
                             kv_norm_g[layer].reshape(1, -1), _pack_w_uq(w_uq[layer]),
                             wk, wvt, q_rank, kv_rank)
        attn = _attention(q, k, vt)
        o_f, o_b = _hgrn(proj, hgrn_lb_logits_fwd, hgrn_lb_logits_bwd,
                         col_q=0, col_ff=8, col_fb=16, col_v=24)
        x1, h2 = _mix_out(attn, o_f, o_b, proj, 32, hgrn_norm_g[layer].reshape(1, -1),
                          w_out[layer].astype(BF16), x, mod3,
                          post_mix_g[layer].reshape(1, d), pre_mlp_g[layer].reshape(1, d))
        x = _mlp(h2, w_up[layer].astype(BF16), w_down[layer].astype(BF16), x1, mod3,
                 post_mlp_g[layer].reshape(1, d))
    return x
```

```python
import functools

import jax
import jax.numpy as jnp
import numpy as np
from jax import lax
from jax.experimental import pallas as pl
from jax.experimental.pallas import tpu as pltpu

F32 = jnp.float32
BF16 = jnp.bfloat16

NORM_EPS = 1e-6
ROPE_BASE = 10000.0
N_MOD = 6
LANE = 128
VMEM_LIMIT = 56 * 1024 * 1024

MLA_HEADS = 8
MLA_NOPE = 128
MLA_ROPE = 64
MLA_V = 128
MLA_QK_PAD = 256
ATTN_TILES_PER_TRIP = 4
MLA_LAT_PAD = 1024
HGRN_HEADS = 8
HGRN_DK = 128
HGRN_DV = 128
HGRN_CHUNK = 64
HGRN_SAFE_LOG2_SPREAD = 96.0
MIX_SUB_ROWS = 128


def _cparams(*sem):
    return pltpu.CompilerParams(dimension_semantics=sem, vmem_limit_bytes=VMEM_LIMIT)


def _dot(a, b):
    return jnp.dot(a, b, preferred_element_type=F32)


def _dot_nt(a, b):
    return lax.dot_general(a, b, (((1,), (1,)), ((), ())), preferred_element_type=F32)


def _dot_tn(a, b):
    return lax.dot_general(a, b, (((0,), (0,)), ((), ())), preferred_element_type=F32)


def _rms(x, g):
    return x * lax.rsqrt(jnp.mean(x * x, axis=-1, keepdims=True) + NORM_EPS) * g


def _mod_kernel(c_ref, w_ref, b_ref, o_ref):
    c = c_ref[...]
    cond = c * jax.nn.sigmoid(c)
    hi = cond.astype(BF16)
    lo = (cond - hi.astype(F32)).astype(BF16)
    w = w_ref[...].astype(BF16)
    o_ref[...] = _dot(hi, w) + _dot(lo, w) + b_ref[...]


def _mod(c, w_mod, b_mod, tn=1024):
    bsz, d = c.shape
    n = w_mod.shape[1]
    return pl.pallas_call(
        _mod_kernel,
        grid=(n // tn,),
        in_specs=[pl.BlockSpec((bsz, d), lambda j: (0, 0)),
                  pl.BlockSpec((d, tn), lambda j: (0, j)),
                  pl.BlockSpec((1, tn), lambda j: (0, j))],
        out_specs=pl.BlockSpec((bsz, tn), lambda j: (0, j)),
        out_shape=jax.ShapeDtypeStruct((bsz, n), F32),
        compiler_params=_cparams("parallel"),
        name="mod",
    )(c, w_mod, b_mod.reshape(1, n))


def _in_proj_kernel(x_ref, mod_ref, g_ref, wa_ref, wb_ref, oa_ref, ob_ref, h_ref):
    @pl.when(pl.program_id(2) == 0)
    def _():
        h = _rms(x_ref[...], g_ref[...]) * (1.0 + mod_ref[1:2, :]) + mod_ref[0:1, :]
        h = h.astype(BF16)
        h_ref[...] = h
        oa_ref[...] = _dot(h, wa_ref[...]).astype(oa_ref.dtype)

    y = _dot(h_ref[...], wb_ref[...]).astype(ob_ref.dtype)
    for c in range(ob_ref.shape[0]):
        ob_ref[c] = y[:, c * LANE:(c + 1) * LANE]


def _in_proj(x, mod3, g, wa, wb, tm=1024, tn=1280):
    bsz, s, d = x.shape
    na, nb = wa.shape[1], wb.shape[1]
    return pl.pallas_call(
        _in_proj_kernel,
        grid=(bsz, s // tm, nb // tn),
        in_specs=[pl.BlockSpec((None, tm, d), lambda b, i, j: (b, i, 0)),
                  pl.BlockSpec((None, N_MOD, d), lambda b, i, j: (b, 0, 0)),
                  pl.BlockSpec((1, d), lambda b, i, j: (0, 0)),
                  pl.BlockSpec((d, na), lambda b, i, j: (0, 0)),
                  pl.BlockSpec((d, tn), lambda b, i, j: (0, j))],
        out_specs=[pl.BlockSpec((None, tm, na), lambda b, i, j: (b, i, 0)),
                   pl.BlockSpec((None, tn // LANE, tm, LANE), lambda b, i, j: (b, j, i, 0))],
        out_shape=[jax.ShapeDtypeStruct((bsz, s, na), BF16),
                   jax.ShapeDtypeStruct((bsz, nb // LANE, s, LANE), BF16)],
        scratch_shapes=[pltpu.VMEM((tm, d), BF16)],
        compiler_params=_cparams("parallel", "parallel", "arbitrary"),
        name="in_proj",
    )(x, mod3, g, wa, wb)


def _rope_tile(t, cos_m, sin_lo, sin_hi):
    half = MLA_ROPE // 2
    return (t * cos_m + pltpu.roll(t, LANE - half, 1) * sin_lo
            + pltpu.roll(t, half, 1) * sin_hi)


def _mla_prep_kernel(p_ref, pos_ref, invf_ref, qg_ref, kvg_ref, wq_ref, wk_ref, wvt_ref,
                     q_ref, k_ref, vt_ref, *, q_rank, kv_rank, scale):
    half = MLA_ROPE // 2
    ang = pos_ref[...].astype(F32) * invf_ref[...]
    lane = lax.broadcasted_iota(jnp.int32, ang.shape, 1)
    cos, sin = jnp.cos(ang), jnp.sin(ang)
    cos_m = jnp.where(lane < MLA_ROPE, cos, 0.0)
    sin_lo = jnp.where(lane < half, -sin, 0.0)
    sin_hi = jnp.where((lane >= half) & (lane < MLA_ROPE), sin, 0.0)

    q_lat = p_ref[:, 0:q_rank].astype(F32)
    q = _dot(_rms(q_lat, qg_ref[...]).astype(BF16), wq_ref[...])
    kv_lat = p_ref[:, q_rank:q_rank + kv_rank].astype(F32)
    kv_n = _rms(kv_lat, kvg_ref[...]).astype(BF16)
    k_nope = _dot(kv_n, wk_ref[...])
    vt_ref[...] = _dot_nt(wvt_ref[...], kv_n).astype(BF16)
    kr_off = q_rank + kv_rank
    k_rope = _rope_tile(p_ref[:, kr_off:kr_off + LANE].astype(F32),
                        cos_m, sin_lo, sin_hi).astype(BF16)
    for h in range(MLA_HEADS):
        o = h * MLA_QK_PAD
        q_ref[:, o:o + MLA_NOPE] = (q[:, o:o + MLA_NOPE] * scale).astype(BF16)
        qr = _rope_tile(q[:, o + MLA_NOPE:o + MLA_QK_PAD], cos_m, sin_lo, sin_hi)
        q_ref[:, o + MLA_NOPE:o + MLA_QK_PAD] = (qr * scale).astype(BF16)
        k_ref[:, o:o + MLA_NOPE] = k_nope[:, h * MLA_NOPE:(h + 1) * MLA_NOPE].astype(BF16)
        k_ref[:, o + MLA_NOPE:o + MLA_QK_PAD] = k_rope


def _mla_prep(proj, positions, invf, qg, kvg, wq, wk, wvt, q_rank, kv_rank, tm=512):
    bsz, s, blk0 = proj.shape
    hq = MLA_HEADS * MLA_QK_PAD
    hv = MLA_HEADS * MLA_V
    scale = float((MLA_NOPE + MLA_ROPE) ** -0.5 * np.log2(np.e))
    kern = functools.partial(_mla_prep_kernel, q_rank=q_rank, kv_rank=kv_rank, scale=scale)

    def const(a):
        return pl.BlockSpec(a.shape, lambda b, i: (0, 0))

    return pl.pallas_call(
        kern,
        grid=(bsz, s // tm),
        in_specs=[pl.BlockSpec((None, tm, blk0), lambda b, i: (b, i, 0)),
                  pl.BlockSpec((None, tm, 1), lambda b, i: (b, i, 0)),
                  const(invf), const(qg), const(kvg), const(wq), const(wk), const(wvt)],
        out_specs=[pl.BlockSpec((None, tm, hq), lambda b, i: (b, i, 0)),
                   pl.BlockSpec((None, tm, hq), lambda b, i: (b, i, 0)),
                   pl.BlockSpec((None, hv, tm), lambda b, i: (b, 0, i))],
        out_shape=[jax.ShapeDtypeStruct((bsz, s, hq), BF16),
                   jax.ShapeDtypeStruct((bsz, s, hq), BF16),
                   jax.ShapeDtypeStruct((bsz, hv, s), BF16)],
        compiler_params=_cparams("parallel", "parallel"),
        name="mla_prep",
    )(proj, positions.reshape(bsz, s, 1), invf, qg, kvg, wq, wk, wvt)


def _attn_kernel(*refs, tq, tk, n_cast):
    q_ref, k_ref, vt_ref = refs[:3]
    o_ref = refs[3 + n_cast]
    sa_ref, sb_ref = refs[-2:]
    for src, dst in zip(refs[3:3 + n_cast], refs[4 + n_cast:4 + 2 * n_cast]):
        dst[...] = src[...].astype(dst.dtype)
    s_len = k_ref.shape[0]
    n_kv = s_len // tk
    n_q = s_len // tq
    sub = 8

    def scores(i, s_ref):
        q = q_ref[pl.ds(pl.multiple_of(i * tq, tq), tq), :]
        m8 = jnp.full((sub, tq), -jnp.inf, F32)
        for j in range(n_kv):
            st = _dot_nt(k_ref[j * tk:(j + 1) * tk, :], q)
            s_ref[j * tk:(j + 1) * tk, :] = st
            m8 = jnp.maximum(m8, jnp.max(st.reshape(tk // sub, sub, tq), axis=0))
            yield
        return jnp.broadcast_to(jnp.max(m8, axis=0, keepdims=True), (sub, tq))

    def values(i, s_ref, m8):
        l8 = jnp.zeros((sub, tq), F32)
        acc = jnp.zeros((MLA_V, tq), F32)
        for j in range(n_kv):
            st = s_ref[j * tk:(j + 1) * tk, :].reshape(tk // sub, sub, tq)
            p = jnp.exp2(st - m8[None])
            l8 = l8 + jnp.sum(p, axis=0)
            acc = acc + _dot(vt_ref[:, j * tk:(j + 1) * tk],
                             p.reshape(tk, tq).astype(BF16))
            yield
        l = jnp.sum(l8, axis=0, keepdims=True)
        o_ref[pl.ds(pl.multiple_of(i * tq, tq), tq), :] = (acc / l).T.astype(o_ref.dtype)

    def run(*gens):
        result = None
        live = list(gens)
        while live:
            for g in list(live):
                try:
                    next(g)
                except StopIteration as stop:
                    if g is gens[0]:
                        result = stop.value
                    live.remove(g)
        return result

    bufs = (sa_ref, sb_ref)
    group = ATTN_TILES_PER_TRIP

    def tiles(first, count, m):
        for u in range(count):
            m = run(scores(first + u + 1, bufs[(u + 1) % 2]),
                    values(first + u, bufs[u % 2], m))
        return m

    m = lax.fori_loop(0, n_q // group - 1, lambda k, m: tiles(group * k, group, m),
                      run(scores(0, bufs[0])))
    m = tiles(n_q - group, group - 1, m)
    run(values(n_q - 1, bufs[(group - 1) % 2], m))


def _attention(q, k, vt, f32_weights, tq=256, tk=512):
    bsz, s, _ = q.shape
    steps = bsz * MLA_HEADS
    kern = functools.partial(_attn_kernel, tq=tq, tk=tk, n_cast=len(f32_weights))

    def slab(w):
        return pl.BlockSpec((w.shape[0] // steps, w.shape[1]),
                            lambda b, h: (b * MLA_HEADS + h, 0))

    assert all(w.shape[0] % (8 * steps) == 0 for w in f32_weights)
    out = pl.pallas_call(
        kern,
        grid=(bsz, MLA_HEADS),
        in_specs=[pl.BlockSpec((None, s, MLA_QK_PAD), lambda b, h: (b, 0, h)),
                  pl.BlockSpec((None, s, MLA_QK_PAD), lambda b, h: (b, 0, h)),
                  pl.BlockSpec((None, MLA_V, s), lambda b, h: (b, h, 0))]
                 + [slab(w) for w in f32_weights],
        out_specs=[pl.BlockSpec((None, s, MLA_V), lambda b, h: (b, 0, h))]
                  + [slab(w) for w in f32_weights],
        out_shape=[jax.ShapeDtypeStruct((bsz, s, MLA_HEADS * MLA_V), BF16)]
                  + [jax.ShapeDtypeStruct(w.shape, BF16) for w in f32_weights],
        scratch_shapes=[pltpu.VMEM((s, tq), F32), pltpu.VMEM((s, tq), F32)],
        compiler_params=_cparams("parallel", "parallel"),
        name="attention",
    )(q, k, vt, *f32_weights)
    return out[0], out[1:]


def _split_bf16(x):
    hi = x.astype(BF16)
    return hi, (x - hi.astype(F32)).astype(BF16)


def _run_staggered(gens):
    live = list(enumerate(gens))
    tick = 0
    while live:
        for delay, g in list(live):
            if tick >= delay and next(g, StopIteration) is StopIteration:
                live.remove((delay, g))
        tick += 1


def _hgrn_exact_block(q, k, b, v_ref, qs, states, o_ref, k_s, b_s, v_s, *, forward):
    n, c, _ = q.shape
    k_s[...] = k.reshape(n * c, HGRN_DK)
    b_s[...] = b.reshape(n * c, HGRN_DK)
    v_s[...] = v_ref[...].astype(F32)
    t_idx = lax.broadcasted_iota(jnp.int32, (c, 1), 0)
    for j in range(n):
        def key_row(s, acc, j=j):
            row = pl.ds(j * c + s, 1)
            w = jnp.exp2(jnp.minimum(b[j] - b_s[row, :], 0.0))
            seen = (t_idx >= s) if forward else (t_idx <= s)
            a_col = jnp.sum(jnp.where(seen, q[j] * w * k_s[row, :], 0.0),
                            axis=1, keepdims=True)
            return acc + a_col * v_s[row, :]

        intra = lax.fori_loop(0, c, key_row, jnp.zeros((c, HGRN_DV), F32))
        inter = _dot_nt(qs[j], states[j])
        o_ref[j * c:(j + 1) * c, :] = (inter + intra).astype(o_ref.dtype)


def _hgrn_direction(q_ref, z_ref, v_ref, lb, st_ref, o_ref, exact_scratch, *, forward):
    c = HGRN_CHUNK
    n = q_ref.shape[0] // c
    row = lax.broadcasted_iota(jnp.int32, (c, c), 0)
    col = lax.broadcasted_iota(jnp.int32, (c, c), 1)
    causal = (col <= row) if forward else (col >= row)
    tri = causal.astype(BF16)
    edge = c - 1 if forward else 0
    mid = c // 2

    chunks = [slice(j * c, (j + 1) * c) for j in range(n)]
    f = lb + (1.0 - lb) * jax.nn.sigmoid(z_ref[...].astype(F32))
    k = (1.0 - f).reshape(n, c, HGRN_DK)
    log_f = jnp.concatenate(_split_bf16(jnp.log2(f)), axis=1)
    q = q_ref[...].astype(F32).reshape(n, c, HGRN_DK)
    vt = v_ref[...].astype(F32).T.astype(BF16)
    yield
    b2 = jnp.concatenate([_dot(tri, log_f[r]) for r in chunks], axis=0)
    yield
    b = (b2[:, :HGRN_DK] + b2[:, HGRN_DK:]).reshape(n, c, HGRN_DK)
    b_edge = b[:, edge:edge + 1]
    b_mid = b[:, mid:mid + 1]
    qs = (q * jnp.exp2(b)).astype(BF16)
    qm = (q * jnp.exp2(b - b_mid)).astype(BF16)
    km = (k * jnp.exp2(b_mid - b)).astype(BF16)
    kd = (k * jnp.exp2(b_edge - b)).astype(BF16)
    decay = jnp.exp2(b_edge)
    spread = jnp.max(jnp.abs(b[:, 0:1] - b[:, c - 1:c]))
    yield
    a = [_dot_nt(qm[j], km[j]) for j in range(n)]
    upd = [_dot(vt[:, chunks[j]], kd[j]) for j in range(n)]
    yield
    a = [jnp.where(causal, x, 0.0).astype(BF16) for x in a]
    st = st_ref[...]
    states = [None] * n
    for j in (range(n) if forward else reversed(range(n))):
        states[j] = st.astype(BF16)
        st = st * decay[j] + upd[j]
    st_ref[...] = st
    yield
    outs = [_dot_nt(jnp.concatenate([a[j], qs[j]], axis=1),
                    jnp.concatenate([vt[:, chunks[j]], states[j]], axis=1))
            for j in range(n)]
    o_ref[...] = jnp.concatenate(outs, axis=0).astype(o_ref.dtype)
    yield
    yield

    @pl.when(spread > HGRN_SAFE_LOG2_SPREAD)
    def _():
        _hgrn_exact_block(q, k, b, v_ref, qs, states, o_ref, *exact_scratch,
                          forward=forward)
    yield


def _hgrn_kernel(qf_ref, zf_ref, vf_ref, qb_ref, zb_ref, vb_ref, lbf_ref, lbb_ref,
                 of_ref, ob_ref, stf_ref, stb_ref, k_s, b_s, v_s):
    @pl.when(pl.program_id(2) == 0)
    def _():
        stf_ref[...] = jnp.zeros_like(stf_ref)
        stb_ref[...] = jnp.zeros_like(stb_ref)

    def lower_bound(ref):
        l = ref[...]
        e = jnp.exp(l - jnp.max(l, axis=0, keepdims=True))
        return e[0:1, :] / jnp.sum(e, axis=0, keepdims=True)

    fwd = _hgrn_direction(qf_ref, zf_ref, vf_ref, lower_bound(lbf_ref), stf_ref, of_ref,
                          (k_s, b_s, v_s), forward=True)
    bwd = _hgrn_direction(qb_ref, zb_ref, vb_ref, lower_bound(lbb_ref), stb_ref, ob_ref,
                          (k_s, b_s, v_s), forward=False)
    _run_staggered([fwd, bwd])


def _hgrn(proj, lbf_logits, lbb_logits, col_q, col_ff, col_fb, col_v, tb=1024):
    bsz, _, s, _ = proj.shape
    nb = s // tb

    def fwd(col):
        return pl.BlockSpec((None, None, tb, LANE), lambda b, h, i: (b, col + h, i, 0))

    def bwd(col):
        return pl.BlockSpec((None, None, tb, LANE),
                            lambda b, h, i: (b, col + h, nb - 1 - i, 0))

    lb_spec = pl.BlockSpec((lbf_logits.shape[0], LANE), lambda b, h, i: (0, h))
    out_shape = jax.ShapeDtypeStruct((bsz, HGRN_HEADS, s, HGRN_DV), BF16)
    return pl.pallas_call(
        _hgrn_kernel,
        grid=(bsz, HGRN_HEADS, nb),
        in_specs=[fwd(col_q), fwd(col_ff), fwd(col_v),
                  bwd(col_q), bwd(col_fb), bwd(col_v), lb_spec, lb_spec],
        out_specs=[fwd(0), bwd(0)],
        out_shape=[out_shape, out_shape],
        scratch_shapes=[pltpu.VMEM((HGRN_DV, HGRN_DK), F32),
                        pltpu.VMEM((HGRN_DV, HGRN_DK), F32),
                        pltpu.VMEM((tb, HGRN_DK), F32), pltpu.VMEM((tb, HGRN_DK), F32),
                        pltpu.VMEM((tb, HGRN_DV), F32)],
        compiler_params=_cparams("parallel", "parallel", "arbitrary"),
        name="hgrn",
    )(proj, proj, proj, proj, proj, proj, lbf_logits, lbb_logits)


def _mix_out_kernel(attn_ref, of_ref, ob_ref, hg_ref, hgn_ref, w_ref, x_ref, mod_ref,
                    postg_ref, preg_ref, x1_ref, h2_ref):
    na = attn_ref.shape[-1]
    tm = attn_ref.shape[0]
    tiles = [slice(r, r + MIX_SUB_ROWS) for r in range(0, tm, MIX_SUB_ROWS)]
    recs = []
    for t in tiles:
        rec = []
        for h in range(HGRN_HEADS):
            o = of_ref[h, t, :].astype(F32) + ob_ref[h, t, :].astype(F32)
            hg = hg_ref[h, t, :].astype(F32)
            rec.append((_rms(o, hgn_ref[...]) * (hg * jax.nn.sigmoid(hg))).astype(BF16))
        recs.append(jnp.concatenate(rec, axis=-1))
    mixes = [_dot(attn_ref[t, :], w_ref[0:na, :]) + _dot(rec, w_ref[na:, :])
             for t, rec in zip(tiles, recs)]
    post_gain = mod_ref[2:3, :] * postg_ref[...]
    pre_gain = preg_ref[...] * (1.0 + mod_ref[4:5, :])
    for t, mix in zip(tiles, mixes):
        x1 = x_ref[t, :] + _rms(mix, post_gain)
        x1_ref[t, :] = x1
        h2_ref[t, :] = (_rms(x1, pre_gain) + mod_ref[3:4, :]).astype(BF16)


def _mix_out(attn, o_f, o_b, proj, col_hg, hgn_g, w_out, x, mod3, post_g, pre_g, tm=512):
    bsz, s, d = x.shape
    na = attn.shape[-1]
    heads = o_f.shape[1]

    def row(width):
        return pl.BlockSpec((None, tm, width), lambda b, i: (b, i, 0))

    def per_head(first):
        return pl.BlockSpec((None, heads, tm, LANE), lambda b, i: (b, first // heads, i, 0))

    def const(shape):
        return pl.BlockSpec(shape, lambda b, i: (0,) * len(shape))

    assert col_hg % heads == 0
    return pl.pallas_call(
        _mix_out_kernel,
        grid=(bsz, s // tm),
        in_specs=[row(na), per_head(0), per_head(0), per_head(col_hg),
                  const((1, HGRN_DV)), const(w_out.shape), row(d),
                  pl.BlockSpec((None, N_MOD, d), lambda b, i: (b, 0, 0)),
                  const((1, d)), const((1, d))],
        out_specs=[row(d), row(d)],
        out_shape=[jax.ShapeDtypeStruct((bsz, s, d), F32),
                   jax.ShapeDtypeStruct((bsz, s, d), BF16)],
        compiler_params=_cparams("parallel", "parallel"),
        name="mix_out",
    )(attn, o_f, o_b, proj, hgn_g, w_out, x, mod3, post_g, pre_g)


def _mlp_kernel(h_ref, wu_ref, wd_ref, x1_ref, mod_ref, g_ref, o_ref):
    j = pl.program_id(2)

    @pl.when(j == 0)
    def _():
        o_ref[...] = jnp.zeros_like(o_ref)

    u = jnp.maximum(_dot(h_ref[...], wu_ref[...]), 0.0)
    o_ref[...] += _dot((u * u).astype(BF16), wd_ref[...])

    @pl.when(j == pl.num_programs(2) - 1)
    def _():
        o_ref[...] = x1_ref[...] + mod_ref[5:6, :] * _rms(o_ref[...], g_ref[...])


def _mlp(h2, w_up, w_down, x1, mod3, g, tm=512, tf=1024):
    bsz, s, d = x1.shape
    dff = w_up.shape[1]
    return pl.pallas_call(
        _mlp_kernel,
        grid=(bsz, s // tm, dff // tf),
        in_specs=[pl.BlockSpec((None, tm, d), lambda b, i, j: (b, i, 0)),
                  pl.BlockSpec((d, tf), lambda b, i, j: (0, j)),
                  pl.BlockSpec((tf, d), lambda b, i, j: (j, 0)),
                  pl.BlockSpec((None, tm, d), lambda b, i, j: (b, i, 0)),
                  pl.BlockSpec((None, N_MOD, d), lambda b, i, j: (b, 0, 0)),
                  pl.BlockSpec((1, d), lambda b, i, j: (0, 0))],
        out_specs=pl.BlockSpec((None, tm, d), lambda b, i, j: (b, i, 0)),
        out_shape=jax.ShapeDtypeStruct((bsz, s, d), F32),
        compiler_params=_cparams("parallel", "parallel", "arbitrary"),
        name="mlp",
    )(h2, w_up, w_down, x1, mod3, g)


def _pack_w_in_kernel(w_ref, wa_ref, wb_ref, *, head):
    w = w_ref[...]
    wa_ref[:, :head] = w[:, :head].astype(BF16)
    wa_ref[:, head:] = jnp.zeros((w.shape[0], wa_ref.shape[1] - head), BF16)
    wb_ref[...] = w[:, head:].astype(BF16)


def _pack_w_in(w, q_rank, kv_rank, rows=256):
    head = q_rank + kv_rank + MLA_ROPE
    d, n = w.shape
    return pl.pallas_call(
        functools.partial(_pack_w_in_kernel, head=head),
        grid=(d // rows,),
        in_specs=[pl.BlockSpec((rows, n), lambda i: (i, 0))],
        out_specs=[pl.BlockSpec((rows, MLA_LAT_PAD), lambda i: (i, 0)),
                   pl.BlockSpec((rows, n - head), lambda i: (i, 0))],
        out_shape=[jax.ShapeDtypeStruct((d, MLA_LAT_PAD), BF16),
                   jax.ShapeDtypeStruct((d, n - head), BF16)],
        compiler_params=_cparams("parallel"),
        name="pack_w_in",
    )(w)


def _pack_w_uq(w):
    r = w.shape[0]
    w = w.reshape(r, MLA_HEADS, MLA_NOPE + MLA_ROPE)
    pad = jnp.zeros((r, MLA_HEADS, MLA_QK_PAD - MLA_NOPE - MLA_ROPE), w.dtype)
    return jnp.concatenate([w, pad], axis=-1).reshape(r, MLA_HEADS * MLA_QK_PAD).astype(BF16)


def _pack_w_ukv(w):
    r = w.shape[0]
    w = w.reshape(r, MLA_HEADS, MLA_NOPE + MLA_V)
    wk = w[:, :, :MLA_NOPE].reshape(r, -1).astype(BF16)
    wvt = w[:, :, MLA_NOPE:].reshape(r, -1).T.astype(BF16)
    return wk, wvt


def _rope_inv_freq():
    half = MLA_ROPE // 2
    f = ROPE_BASE ** (-jnp.arange(half, dtype=F32) / half)
    return jnp.concatenate([f, f, jnp.zeros((LANE - MLA_ROPE,), F32)]).reshape(1, LANE)


def kernel(x, c, positions, w_mod, b_mod, pre_mix_g, post_mix_g, pre_mlp_g, post_mlp_g,
           w_in, q_norm_g, kv_norm_g, w_uq, w_ukv, hgrn_norm_g, hgrn_lb_logits_fwd,
           hgrn_lb_logits_bwd, w_out, w_up, w_down):
    depth = w_mod.shape[0]
    assert depth == 1, "lower-bound slot selection is written for a single layer"
    bsz, s, d = x.shape
    q_rank = q_norm_g.shape[-1]
    kv_rank = kv_norm_g.shape[-1]
    invf = _rope_inv_freq()
    for layer in range(depth):
        mod3 = _mod(c, w_mod[layer], b_mod[layer]).reshape(bsz, N_MOD, d)
        lat, proj = _in_proj(x, mod3, pre_mix_g[layer].reshape(1, d),
                             *_pack_w_in(w_in[layer], q_rank, kv_rank))
        wk, wvt = _pack_w_ukv(w_ukv[layer])
        q, k, vt = _mla_prep(lat, positions, invf, q_norm_g[layer].reshape(1, -1),
                             kv_norm_g[layer].reshape(1, -1), _pack_w_uq(w_uq[layer]),
                             wk, wvt, q_rank, kv_rank)
        attn, (w_out_b, w_up_b, w_down_b) = _attention(
            q, k, vt, (w_out[layer], w_up[layer], w_down[layer]))
        o_f, o_b = _hgrn(proj, hgrn_lb_logits_fwd, hgrn_lb_logits_bwd,
                         col_q=0, col_ff=8, col_fb=16, col_v=24)
        x1, h2 = _mix_out(attn, o_f, o_b, proj, 32, hgrn_norm_g[layer].reshape(1, -1),
                          w_out_b, x, mod3,
                          post_mix_g[layer].reshape(1, d), pre_mlp_g[layer].reshape(1, d))
        x = _mlp(h2, w_up_b, w_down_b, x1, mod3, post_mlp_g[layer].reshape(1, d))
    return x
```

```python
import functools

import jax
import jax.numpy as jnp
import numpy as np
from jax import lax
from jax.experimental import pallas as pl
from jax.experimental.pallas import tpu as pltpu

F32 = jnp.float32
BF16 = jnp.bfloat16

NORM_EPS = 1e-6
ROPE_BASE = 10000.0
N_MOD = 6
LANE = 128
VMEM_LIMIT = 56 * 1024 * 1024

MLA_HEADS = 8
MLA_NOPE = 128
MLA_ROPE = 64
MLA_V = 128
MLA_QK_PAD = 256
ATTN_TILES_PER_TRIP = 4
MLA_LAT_PAD = 1024
HGRN_HEADS = 8
HGRN_DK = 128
HGRN_DV = 128
HGRN_CHUNK = 64
HGRN_SAFE_LOG2_SPREAD = 96.0
MIX_SUB_ROWS = 128


def _cparams(*sem):
    return pltpu.CompilerParams(dimension_semantics=sem, vmem_limit_bytes=VMEM_LIMIT)


def _dot(a, b):
    return jnp.dot(a, b, preferred_element_type=F32)


def _dot_nt(a, b):
    return lax.dot_general(a, b, (((1,), (1,)), ((), ())), preferred_element_type=F32)


def _dot_tn(a, b):
    return lax.dot_general(a, b, (((0,), (0,)), ((), ())), preferred_element_type=F32)


def _rms(x, g):
    return x * lax.rsqrt(jnp.mean(x * x, axis=-1, keepdims=True) + NORM_EPS) * g


def _mod_kernel(c_ref, w_ref, b_ref, o_ref):
    c = c_ref[...]
    cond = c * jax.nn.sigmoid(c)
    hi = cond.astype(BF16)
    lo = (cond - hi.astype(F32)).astype(BF16)
    w = w_ref[...].astype(BF16)
    o_ref[...] = _dot(hi, w) + _dot(lo, w) + b_ref[...]


def _mod(c, w_mod, b_mod, tn=2048):
    bsz, d = c.shape
    n = w_mod.shape[1]
    return pl.pallas_call(
        _mod_kernel,
        grid=(n // tn,),
        in_specs=[pl.BlockSpec((bsz, d), lambda j: (0, 0)),
                  pl.BlockSpec((d, tn), lambda j: (0, j)),
                  pl.BlockSpec((1, tn), lambda j: (0, j))],
        out_specs=pl.BlockSpec((bsz, tn), lambda j: (0, j)),
        out_shape=jax.ShapeDtypeStruct((bsz, n), F32),
        compiler_params=_cparams("parallel"),
        name="mod",
    )(c, w_mod, b_mod.reshape(1, n))


def _in_proj_kernel(x_ref, mod_ref, g_ref, wa_ref, wb_ref, oa_ref, ob_ref, h_ref):
    @pl.when(pl.program_id(2) == 0)
    def _():
        h = _rms(x_ref[...], g_ref[...]) * (1.0 + mod_ref[1:2, :]) + mod_ref[0:1, :]
        h = h.astype(BF16)
        h_ref[...] = h
        oa_ref[...] = _dot_nt(h, wa_ref[...]).astype(oa_ref.dtype)

    y = _dot_nt(h_ref[...], wb_ref[...]).astype(ob_ref.dtype)
    for c in range(ob_ref.shape[0]):
        ob_ref[c] = y[:, c * LANE:(c + 1) * LANE]


def _in_proj(x, mod3, g, wt, head, tm=1024, tn=1280):
    bsz, s, d = x.shape
    na, nb = MLA_LAT_PAD, wt.shape[0] - head
    row_align = int(np.gcd(head, tn))
    assert row_align % 16 == 0
    return pl.pallas_call(
        _in_proj_kernel,
        grid=(bsz, s // tm, nb // tn),
        in_specs=[pl.BlockSpec((None, tm, d), lambda b, i, j: (b, i, 0)),
                  pl.BlockSpec((None, N_MOD, d), lambda b, i, j: (b, 0, 0)),
                  pl.BlockSpec((1, d), lambda b, i, j: (0, 0)),
                  pl.BlockSpec((pl.Element(na), pl.Element(d)), lambda b, i, j: (0, 0)),
                  pl.BlockSpec((pl.Element(tn), pl.Element(d)),
                               lambda b, i, j: (pl.multiple_of(head + j * tn, row_align), 0))],
        out_specs=[pl.BlockSpec((None, tm, na), lambda b, i, j: (b, i, 0)),
                   pl.BlockSpec((None, tn // LANE, tm, LANE), lambda b, i, j: (b, j, i, 0))],
        out_shape=[jax.ShapeDtypeStruct((bsz, s, na), BF16),
                   jax.ShapeDtypeStruct((bsz, nb // LANE, s, LANE), BF16)],
        scratch_shapes=[pltpu.VMEM((tm, d), BF16)],
        compiler_params=_cparams("parallel", "parallel", "arbitrary"),
        name="in_proj",
    )(x, mod3, g, wt, wt)


def _rope_tile(t, cos_m, sin_lo, sin_hi):
    half = MLA_ROPE // 2
    return (t * cos_m + pltpu.roll(t, LANE - half, 1) * sin_lo
            + pltpu.roll(t, half, 1) * sin_hi)


def _mla_prep_kernel(p_ref, pos_ref, invf_ref, qg_ref, kvg_ref, wq_ref, wk_ref, wvt_ref,
                     q_ref, k_ref, vt_ref, *, q_rank, kv_rank, scale):
    half = MLA_ROPE // 2
    ang = pos_ref[...].astype(F32) * invf_ref[...]
    lane = lax.broadcasted_iota(jnp.int32, ang.shape, 1)
    cos, sin = jnp.cos(ang), jnp.sin(ang)
    cos_m = jnp.where(lane < MLA_ROPE, cos, 0.0)
    sin_lo = jnp.where(lane < half, -sin, 0.0)
    sin_hi = jnp.where((lane >= half) & (lane < MLA_ROPE), sin, 0.0)

    q_lat = p_ref[:, 0:q_rank].astype(F32)
    q = _dot(_rms(q_lat, qg_ref[...]).astype(BF16), wq_ref[...])
    kv_lat = p_ref[:, q_rank:q_rank + kv_rank].astype(F32)
    kv_n = _rms(kv_lat, kvg_ref[...]).astype(BF16)
    k_nope = _dot(kv_n, wk_ref[...])
    vt_ref[...] = _dot_nt(wvt_ref[...], kv_n).astype(BF16)
    kr_off = q_rank + kv_rank
    k_rope = _rope_tile(p_ref[:, kr_off:kr_off + LANE].astype(F32),
                        cos_m, sin_lo, sin_hi).astype(BF16)
    for h in range(MLA_HEADS):
        o = h * MLA_QK_PAD
        q_ref[:, o:o + MLA_NOPE] = (q[:, o:o + MLA_NOPE] * scale).astype(BF16)
        qr = _rope_tile(q[:, o + MLA_NOPE:o + MLA_QK_PAD], cos_m, sin_lo, sin_hi)
        q_ref[:, o + MLA_NOPE:o + MLA_QK_PAD] = (qr * scale).astype(BF16)
        k_ref[:, o:o + MLA_NOPE] = k_nope[:, h * MLA_NOPE:(h + 1) * MLA_NOPE].astype(BF16)
        k_ref[:, o + MLA_NOPE:o + MLA_QK_PAD] = k_rope


def _mla_prep(proj, positions, invf, qg, kvg, wq, wk, wvt, q_rank, kv_rank, tm=512):
    bsz, s, blk0 = proj.shape
    hq = MLA_HEADS * MLA_QK_PAD
    hv = MLA_HEADS * MLA_V
    scale = float((MLA_NOPE + MLA_ROPE) ** -0.5 * np.log2(np.e))
    kern = functools.partial(_mla_prep_kernel, q_rank=q_rank, kv_rank=kv_rank, scale=scale)

    def const(a):
        return pl.BlockSpec(a.shape, lambda b, i: (0, 0))

    return pl.pallas_call(
        kern,
        grid=(bsz, s // tm),
        in_specs=[pl.BlockSpec((None, tm, blk0), lambda b, i: (b, i, 0)),
                  pl.BlockSpec((None, tm, 1), lambda b, i: (b, i, 0)),
                  const(invf), const(qg), const(kvg), const(wq), const(wk), const(wvt)],
        out_specs=[pl.BlockSpec((None, tm, hq), lambda b, i: (b, i, 0)),
                   pl.BlockSpec((None, tm, hq), lambda b, i: (b, i, 0)),
                   pl.BlockSpec((None, hv, tm), lambda b, i: (b, 0, i))],
        out_shape=[jax.ShapeDtypeStruct((bsz, s, hq), BF16),
                   jax.ShapeDtypeStruct((bsz, s, hq), BF16),
                   jax.ShapeDtypeStruct((bsz, hv, s), BF16)],
        compiler_params=_cparams("parallel", "parallel"),
        name="mla_prep",
    )(proj, positions.reshape(bsz, s, 1), invf, qg, kvg, wq, wk, wvt)


def _attn_kernel(*refs, tq, tk, n_cast):
    q_ref, k_ref, vt_ref = refs[:3]
    o_ref = refs[3 + n_cast]
    sa_ref, sb_ref = refs[-2:]
    for src, dst in zip(refs[3:3 + n_cast], refs[4 + n_cast:4 + 2 * n_cast]):
        dst[...] = src[...].astype(dst.dtype)
    s_len = k_ref.shape[0]
    n_kv = s_len // tk
    n_q = s_len // tq
    sub = 8

    def scores(i, s_ref):
        q = q_ref[pl.ds(pl.multiple_of(i * tq, tq), tq), :]
        m8 = jnp.full((sub, tq), -jnp.inf, F32)
        for j in range(n_kv):
            st = _dot_nt(k_ref[j * tk:(j + 1) * tk, :], q)
            s_ref[j * tk:(j + 1) * tk, :] = st
            m8 = jnp.maximum(m8, jnp.max(st.reshape(tk // sub, sub, tq), axis=0))
            yield
        return jnp.broadcast_to(jnp.max(m8, axis=0, keepdims=True), (sub, tq))

    def values(i, s_ref, m8):
        l8 = jnp.zeros((sub, tq), F32)
        acc = jnp.zeros((MLA_V, tq), F32)
        for j in range(n_kv):
            st = s_ref[j * tk:(j + 1) * tk, :].reshape(tk // sub, sub, tq)
            p = jnp.exp2(st - m8[None])
            l8 = l8 + jnp.sum(p, axis=0)
            acc = acc + _dot(vt_ref[:, j * tk:(j + 1) * tk],
                             p.reshape(tk, tq).astype(BF16))
            yield
        l = jnp.sum(l8, axis=0, keepdims=True)
        o_ref[pl.ds(pl.multiple_of(i * tq, tq), tq), :] = (acc / l).T.astype(o_ref.dtype)

    def run(*gens):
        result = None
        live = list(gens)
        while live:
            for g in list(live):
                try:
                    next(g)
                except StopIteration as stop:
                    if g is gens[0]:
                        result = stop.value
                    live.remove(g)
        return result

    bufs = (sa_ref, sb_ref)
    group = ATTN_TILES_PER_TRIP

    def tiles(first, count, m):
        for u in range(count):
            m = run(scores(first + u + 1, bufs[(u + 1) % 2]),
                    values(first + u, bufs[u % 2], m))
        return m

    m = lax.fori_loop(0, n_q // group - 1, lambda k, m: tiles(group * k, group, m),
                      run(scores(0, bufs[0])))
    m = tiles(n_q - group, group - 1, m)
    run(values(n_q - 1, bufs[(group - 1) % 2], m))


def _attention(q, k, vt, f32_weights, tq=256, tk=512):
    bsz, s, _ = q.shape
    steps = bsz * MLA_HEADS
    kern = functools.partial(_attn_kernel, tq=tq, tk=tk, n_cast=len(f32_weights))

    def slab(w):
        return pl.BlockSpec((w.shape[0] // steps, w.shape[1]),
                            lambda b, h: (b * MLA_HEADS + h, 0))

    assert all(w.shape[0] % (8 * steps) == 0 for w in f32_weights)
    out = pl.pallas_call(
        kern,
        grid=(bsz, MLA_HEADS),
        in_specs=[pl.BlockSpec((None, s, MLA_QK_PAD), lambda b, h: (b, 0, h)),
                  pl.BlockSpec((None, s, MLA_QK_PAD), lambda b, h: (b, 0, h)),
                  pl.BlockSpec((None, MLA_V, s), lambda b, h: (b, h, 0))]
                 + [slab(w) for w in f32_weights],
        out_specs=[pl.BlockSpec((None, s, MLA_V), lambda b, h: (b, 0, h))]
                  + [slab(w) for w in f32_weights],
        out_shape=[jax.ShapeDtypeStruct((bsz, s, MLA_HEADS * MLA_V), BF16)]
                  + [jax.ShapeDtypeStruct(w.shape, BF16) for w in f32_weights],
        scratch_shapes=[pltpu.VMEM((s, tq), F32), pltpu.VMEM((s, tq), F32)],
        compiler_params=_cparams("parallel", "parallel"),
        name="attention",
    )(q, k, vt, *f32_weights)
    return out[0], out[1:]


def _split_bf16(x):
    hi = x.astype(BF16)
    return hi, (x - hi.astype(F32)).astype(BF16)


def _run_staggered(gens):
    live = list(enumerate(gens))
    tick = 0
    while live:
        for delay, g in list(live):
            if tick >= delay and next(g, StopIteration) is StopIteration:
                live.remove((delay, g))
        tick += 1


def _hgrn_exact_block(q, k, b, v_ref, qs, states, o_ref, k_s, b_s, v_s, *, forward):
    n, c, _ = q.shape
    k_s[...] = k.reshape(n * c, HGRN_DK)
    b_s[...] = b.reshape(n * c, HGRN_DK)
    v_s[...] = v_ref[...].astype(F32)
    t_idx = lax.broadcasted_iota(jnp.int32, (c, 1), 0)
    for j in range(n):
        def key_row(s, acc, j=j):
            row = pl.ds(j * c + s, 1)
            w = jnp.exp2(jnp.minimum(b[j] - b_s[row, :], 0.0))
            seen = (t_idx >= s) if forward else (t_idx <= s)
            a_col = jnp.sum(jnp.where(seen, q[j] * w * k_s[row, :], 0.0),
                            axis=1, keepdims=True)
            return acc + a_col * v_s[row, :]

        intra = lax.fori_loop(0, c, key_row, jnp.zeros((c, HGRN_DV), F32))
        inter = _dot_nt(qs[j], states[j])
        o_ref[j * c:(j + 1) * c, :] = (inter + intra).astype(o_ref.dtype)


def _hgrn_direction(q_ref, z_ref, v_ref, lb, st_ref, o_ref, exact_scratch, *, forward):
    c = HGRN_CHUNK
    n = q_ref.shape[0] // c
    row = lax.broadcasted_iota(jnp.int32, (c, c), 0)
    col = lax.broadcasted_iota(jnp.int32, (c, c), 1)
    causal = (col <= row) if forward else (col >= row)
    tri = causal.astype(BF16)
    edge = c - 1 if forward else 0

    chunks = [slice(j * c, (j + 1) * c) for j in range(n)]
    f = lb + (1.0 - lb) * jax.nn.sigmoid(z_ref[...].astype(F32))
    k = (1.0 - f).reshape(n, c, HGRN_DK)
    log_f = jnp.concatenate(_split_bf16(jnp.log2(f)), axis=1)
    q = q_ref[...].reshape(n, c, HGRN_DK)
    vt = v_ref[...].astype(F32).T.astype(BF16)
    yield
    b2 = jnp.concatenate([_dot(tri, log_f[r]) for r in chunks], axis=0)
    yield
    b = (b2[:, :HGRN_DK] + b2[:, HGRN_DK:]).reshape(n, c, HGRN_DK)
    b_edge = b[:, edge:edge + 1]
    kb = k.astype(BF16)
    qs = q * jnp.exp2(b).astype(BF16)
    ki = kb * jnp.exp2(-b).astype(BF16)
    kd = kb * jnp.exp2(b_edge - b).astype(BF16)
    decay = jnp.exp2(b_edge)
    spread = jnp.max(-b_edge)
    yield
    a = [_dot_nt(qs[j], ki[j]) for j in range(n)]
    upd = [_dot(vt[:, chunks[j]], kd[j]) for j in range(n)]
    yield
    a = [jnp.where(causal, x, 0.0).astype(BF16) for x in a]
    st = st_ref[...]
    states = [None] * n
    for j in (range(n) if forward else reversed(range(n))):
        states[j] = st.astype(BF16)
        st = st * decay[j] + upd[j]
    st_ref[...] = st
    yield
    outs = [_dot_nt(jnp.concatenate([a[j], qs[j]], axis=1),
                    jnp.concatenate([vt[:, chunks[j]], states[j]], axis=1))
            for j in range(n)]
    o_ref[...] = jnp.concatenate(outs, axis=0).astype(o_ref.dtype)
    yield
    yield

    @pl.when(spread > HGRN_SAFE_LOG2_SPREAD)
    def _():
        _hgrn_exact_block(q.astype(F32), k, b, v_ref, qs, states, o_ref, *exact_scratch,
                          forward=forward)
    yield


def _hgrn_kernel(qf_ref, zf_ref, vf_ref, qb_ref, zb_ref, vb_ref, lbf_ref, lbb_ref,
                 of_ref, ob_ref, stf_ref, stb_ref, k_s, b_s, v_s):
    @pl.when(pl.program_id(2) == 0)
    def _():
        stf_ref[...] = jnp.zeros_like(stf_ref)
        stb_ref[...] = jnp.zeros_like(stb_ref)

    def lower_bound(ref):
        l = ref[...]
        e = jnp.exp(l - jnp.max(l, axis=0, keepdims=True))
        return e[0:1, :] / jnp.sum(e, axis=0, keepdims=True)

    fwd = _hgrn_direction(qf_ref, zf_ref, vf_ref, lower_bound(lbf_ref), stf_ref, of_ref,
                          (k_s, b_s, v_s), forward=True)
    bwd = _hgrn_direction(qb_ref, zb_ref, vb_ref, lower_bound(lbb_ref), stb_ref, ob_ref,
                          (k_s, b_s, v_s), forward=False)
    _run_staggered([fwd, bwd])


def _hgrn(proj, lbf_logits, lbb_logits, col_q, col_ff, col_fb, col_v, tb=1024):
    bsz, _, s, _ = proj.shape
    nb = s // tb

    def fwd(col):
        return pl.BlockSpec((None, None, tb, LANE), lambda b, h, i: (b, col + h, i, 0))

    def bwd(col):
        return pl.BlockSpec((None, None, tb, LANE),
                            lambda b, h, i: (b, col + h, nb - 1 - i, 0))

    lb_spec = pl.BlockSpec((lbf_logits.shape[0], LANE), lambda b, h, i: (0, h))
    out_shape = jax.ShapeDtypeStruct((bsz, HGRN_HEADS, s, HGRN_DV), BF16)
    return pl.pallas_call(
        _hgrn_kernel,
        grid=(bsz, HGRN_HEADS, nb),
        in_specs=[fwd(col_q), fwd(col_ff), fwd(col_v),
                  bwd(col_q), bwd(col_fb), bwd(col_v), lb_spec, lb_spec],
        out_specs=[fwd(0), bwd(0)],
        out_shape=[out_shape, out_shape],
        scratch_shapes=[pltpu.VMEM((HGRN_DV, HGRN_DK), F32),
                        pltpu.VMEM((HGRN_DV, HGRN_DK), F32),
                        pltpu.VMEM((tb, HGRN_DK), F32), pltpu.VMEM((tb, HGRN_DK), F32),
                        pltpu.VMEM((tb, HGRN_DV), F32)],
        compiler_params=_cparams("parallel", "parallel", "arbitrary"),
        name="hgrn",
    )(proj, proj, proj, proj, proj, proj, lbf_logits, lbb_logits)


def _mix_out_kernel(attn_ref, of_ref, ob_ref, hg_ref, hgn_ref, w_ref, x_ref, mod_ref,
                    postg_ref, preg_ref, x1_ref, h2_ref):
    na = attn_ref.shape[-1]
    tm = attn_ref.shape[0]
    tiles = [slice(r, r + MIX_SUB_ROWS) for r in range(0, tm, MIX_SUB_ROWS)]
    recs = []
    for t in tiles:
        rec = []
        for h in range(HGRN_HEADS):
            o = of_ref[h, t, :].astype(F32) + ob_ref[h, t, :].astype(F32)
            hg = hg_ref[h, t, :].astype(F32)
            rec.append((_rms(o, hgn_ref[...]) * (hg * jax.nn.sigmoid(hg))).astype(BF16))
        recs.append(jnp.concatenate(rec, axis=-1))
    mixes = [_dot(attn_ref[t, :], w_ref[0:na, :]) + _dot(rec, w_ref[na:, :])
             for t, rec in zip(tiles, recs)]
    post_gain = mod_ref[2:3, :] * postg_ref[...]
    pre_gain = preg_ref[...] * (1.0 + mod_ref[4:5, :])
    for t, mix in zip(tiles, mixes):
        x1 = x_ref[t, :] + _rms(mix, post_gain)
        x1_ref[t, :] = x1
        h2_ref[t, :] = (_rms(x1, pre_gain) + mod_ref[3:4, :]).astype(BF16)


def _mix_out(attn, o_f, o_b, proj, col_hg, hgn_g, w_out, x, mod3, post_g, pre_g, tm=512):
    bsz, s, d = x.shape
    na = attn.shape[-1]
    heads = o_f.shape[1]

    def row(width):
        return pl.BlockSpec((None, tm, width), lambda b, i: (b, i, 0))

    def per_head(first):
        return pl.BlockSpec((None, heads, tm, LANE), lambda b, i: (b, first // heads, i, 0))

    def const(shape):
        return pl.BlockSpec(shape, lambda b, i: (0,) * len(shape))

    assert col_hg % heads == 0
    return pl.pallas_call(
        _mix_out_kernel,
        grid=(bsz, s // tm),
        in_specs=[row(na), per_head(0), per_head(0), per_head(col_hg),
                  const((1, HGRN_DV)), const(w_out.shape), row(d),
                  pl.BlockSpec((None, N_MOD, d), lambda b, i: (b, 0, 0)),
                  const((1, d)), const((1, d))],
        out_specs=[row(d), row(d)],
        out_shape=[jax.ShapeDtypeStruct((bsz, s, d), F32),
                   jax.ShapeDtypeStruct((bsz, s, d), BF16)],
        compiler_params=_cparams("parallel", "parallel"),
        name="mix_out",
    )(attn, o_f, o_b, proj, hgn_g, w_out, x, mod3, post_g, pre_g)


def _mlp_kernel(h_ref, wu_ref, wd_ref, x1_ref, mod_ref, g_ref, o_ref):
    j = pl.program_id(2)

    @pl.when(j == 0)
    def _():
        o_ref[...] = jnp.zeros_like(o_ref)

    u = jnp.maximum(_dot(h_ref[...], wu_ref[...]), 0.0)
    o_ref[...] += _dot((u * u).astype(BF16), wd_ref[...])

    @pl.when(j == pl.num_programs(2) - 1)
    def _():
        o_ref[...] = x1_ref[...] + mod_ref[5:6, :] * _rms(o_ref[...], g_ref[...])


def _mlp(h2, w_up, w_down, x1, mod3, g, tm=512, tf=1024):
    bsz, s, d = x1.shape
    dff = w_up.shape[1]
    return pl.pallas_call(
        _mlp_kernel,
        grid=(bsz, s // tm, dff // tf),
        in_specs=[pl.BlockSpec((None, tm, d), lambda b, i, j: (b, i, 0)),
                  pl.BlockSpec((d, tf), lambda b, i, j: (0, j)),
                  pl.BlockSpec((tf, d), lambda b, i, j: (j, 0)),
                  pl.BlockSpec((None, tm, d), lambda b, i, j: (b, i, 0)),
                  pl.BlockSpec((None, N_MOD, d), lambda b, i, j: (b, 0, 0)),
                  pl.BlockSpec((1, d), lambda b, i, j: (0, 0))],
        out_specs=pl.BlockSpec((None, tm, d), lambda b, i, j: (b, i, 0)),
        out_shape=jax.ShapeDtypeStruct((bsz, s, d), F32),
        compiler_params=_cparams("parallel", "parallel", "arbitrary"),
        name="mlp",
    )(h2, w_up, w_down, x1, mod3, g)


def _pack_w_uq(w):
    r = w.shape[0]
    w = w.reshape(r, MLA_HEADS, MLA_NOPE + MLA_ROPE)
    pad = jnp.zeros((r, MLA_HEADS, MLA_QK_PAD - MLA_NOPE - MLA_ROPE), w.dtype)
    return jnp.concatenate([w, pad], axis=-1).reshape(r, MLA_HEADS * MLA_QK_PAD).astype(BF16)


def _pack_w_ukv(w):
    r = w.shape[0]
    w = w.reshape(r, MLA_HEADS, MLA_NOPE + MLA_V)
    wk = w[:, :, :MLA_NOPE].reshape(r, -1).astype(BF16)
    wvt = w[:, :, MLA_NOPE:].reshape(r, -1).T.astype(BF16)
    return wk, wvt


def _rope_inv_freq():
    half = MLA_ROPE // 2
    f = ROPE_BASE ** (-jnp.arange(half, dtype=F32) / half)
    return jnp.concatenate([f, f, jnp.zeros((LANE - MLA_ROPE,), F32)]).reshape(1, LANE)


def kernel(x, c, positions, w_mod, b_mod, pre_mix_g, post_mix_g, pre_mlp_g, post_mlp_g,
           w_in, q_norm_g, kv_norm_g, w_uq, w_ukv, hgrn_norm_g, hgrn_lb_logits_fwd,
           hgrn_lb_logits_bwd, w_out, w_up, w_down):
    depth = w_mod.shape[0]
    assert depth == 1, "lower-bound slot selection is written for a single layer"
    bsz, s, d = x.shape
    q_rank = q_norm_g.shape[-1]
    kv_rank = kv_norm_g.shape[-1]
    invf = _rope_inv_freq()
    for layer in range(depth):
        mod3 = _mod(c, w_mod[layer], b_mod[layer]).reshape(bsz, N_MOD, d)
        lat, proj = _in_proj(x, mod3, pre_mix_g[layer].reshape(1, d),
                             w_in[layer].T.astype(BF16), q_rank + kv_rank + MLA_ROPE)
        wk, wvt = _pack_w_ukv(w_ukv[layer])
        q, k, vt = _mla_prep(lat, positions, invf, q_norm_g[layer].reshape(1, -1),
                             kv_norm_g[layer].reshape(1, -1), _pack_w_uq(w_uq[layer]),
                             wk, wvt, q_rank, kv_rank)
        attn, (w_out_b, w_up_b, w_down_b) = _attention(
            q, k, vt, (w_out[layer], w_up[layer], w_down[layer]))
        o_f, o_b = _hgrn(proj, hgrn_lb_logits_fwd, hgrn_lb_logits_bwd,
                         col_q=0, col_ff=8, col_fb=16, col_v=24)
        x1, h2 = _mix_out(attn, o_f, o_b, proj, 32, hgrn_norm_g[layer].reshape(1, -1),
                          w_out_b, x, mod3,
                          post_mix_g[layer].reshape(1, d), pre_mlp_g[layer].reshape(1, d))
        x = _mlp(h2, w_up_b, w_down_b, x1, mod3, post_mlp_g[layer].reshape(1, d))
    return x
```

```python
import functools

import jax
import jax.numpy as jnp
import numpy as np
from jax import lax
from jax.experimental import pallas as pl
from jax.experimental.pallas import tpu as pltpu

F32 = jnp.float32
BF16 = jnp.bfloat16

NORM_EPS = 1e-6
ROPE_BASE = 10000.0
N_MOD = 6
LANE = 128
VMEM_LIMIT = 56 * 1024 * 1024

MLA_HEADS = 8
MLA_NOPE = 128
MLA_ROPE = 64
MLA_V = 128
MLA_QK_PAD = 256
ROPE_PACK = LANE // (MLA_ROPE // 2)
ATTN_TILES_PER_TRIP = 4
MLA_LAT_PAD = 1024
HGRN_HEADS = 8
HGRN_DK = 128
HGRN_DV = 128
HGRN_CHUNK = 64
HGRN_SAFE_LOG2_SPREAD = 96.0
MLP_SUB_ROWS = 128
MIX_SUB_ROWS = 128


def _cparams(*sem):
    return pltpu.CompilerParams(dimension_semantics=sem, vmem_limit_bytes=VMEM_LIMIT)


def _dot(a, b):
    return jnp.dot(a, b, preferred_element_type=F32)


def _dot_nt(a, b):
    return lax.dot_general(a, b, (((1,), (1,)), ((), ())), preferred_element_type=F32)


def _dot_tn(a, b):
    return lax.dot_general(a, b, (((0,), (0,)), ((), ())), preferred_element_type=F32)


def _rms(x, g):
    return x * lax.rsqrt(jnp.mean(x * x, axis=-1, keepdims=True) + NORM_EPS) * g


def _mod_kernel(c_ref, w_ref, b_ref, o_ref):
    c = c_ref[...]
    cond = c * jax.nn.sigmoid(c)
    hi = cond.astype(BF16)
    lo = (cond - hi.astype(F32)).astype(BF16)
    w = w_ref[...].astype(BF16)
    o_ref[...] = _dot(hi, w) + _dot(lo, w) + b_ref[...]


def _mod(c, w_mod, b_mod, tn=1024):
    bsz, d = c.shape
    n = w_mod.shape[1]
    return pl.pallas_call(
        _mod_kernel,
        grid=(n // tn,),
        in_specs=[pl.BlockSpec((bsz, d), lambda j: (0, 0)),
                  pl.BlockSpec((d, tn), lambda j: (0, j)),
                  pl.BlockSpec((1, tn), lambda j: (0, j))],
        out_specs=pl.BlockSpec((bsz, tn), lambda j: (0, j)),
        out_shape=jax.ShapeDtypeStruct((bsz, n), F32),
        compiler_params=_cparams("parallel"),
        name="mod",
    )(c, w_mod, b_mod.reshape(1, n))


def _in_proj_kernel(x_ref, mod_ref, g_ref, wa_ref, wb_ref, oa_ref, ob_ref, h_ref):
    @pl.when(pl.program_id(2) == 0)
    def _():
        h = _rms(x_ref[...], g_ref[...]) * (1.0 + mod_ref[1:2, :]) + mod_ref[0:1, :]
        h = h.astype(BF16)
        h_ref[...] = h
        oa_ref[...] = _dot_nt(h, wa_ref[...]).astype(oa_ref.dtype)

    y = _dot_nt(h_ref[...], wb_ref[...]).astype(ob_ref.dtype)
    for c in range(ob_ref.shape[0]):
        ob_ref[c] = y[:, c * LANE:(c + 1) * LANE]


def _in_proj(x, mod3, g, wt, head, tm=1024, tn=1280):
    bsz, s, d = x.shape
    na, nb = MLA_LAT_PAD, wt.shape[0] - head
    row_align = int(np.gcd(head, tn))
    assert row_align % 16 == 0
    return pl.pallas_call(
        _in_proj_kernel,
        grid=(bsz, s // tm, nb // tn),
        in_specs=[pl.BlockSpec((None, tm, d), lambda b, i, j: (b, i, 0)),
                  pl.BlockSpec((None, N_MOD, d), lambda b, i, j: (b, 0, 0)),
                  pl.BlockSpec((1, d), lambda b, i, j: (0, 0)),
                  pl.BlockSpec((pl.Element(na), pl.Element(d)), lambda b, i, j: (0, 0)),
                  pl.BlockSpec((pl.Element(tn), pl.Element(d)),
                               lambda b, i, j: (pl.multiple_of(head + j * tn, row_align), 0))],
        out_specs=[pl.BlockSpec((None, tm, na), lambda b, i, j: (b, i, 0)),
                   pl.BlockSpec((None, tn // LANE, tm, LANE), lambda b, i, j: (b, j, i, 0))],
        out_shape=[jax.ShapeDtypeStruct((bsz, s, na), BF16),
                   jax.ShapeDtypeStruct((bsz, nb // LANE, s, LANE), BF16)],
        scratch_shapes=[pltpu.VMEM((tm, d), BF16)],
        compiler_params=_cparams("parallel", "parallel", "arbitrary"),
        name="in_proj",
    )(x, mod3, g, wt, wt)


def _rope_tile(t, cos_m, sin_lo, sin_hi):
    half = MLA_ROPE // 2
    return (t * cos_m + pltpu.roll(t, LANE - half, 1) * sin_lo
            + pltpu.roll(t, half, 1) * sin_hi)


def _rope_tables(pos_ref, invf_ref, tab_ref):
    half = MLA_ROPE // 2
    rows = pos_ref.shape[0]
    pos = pos_ref[...].astype(F32)
    lane = lax.broadcasted_iota(jnp.int32, (rows, LANE), 1)
    spread = pos[:, ROPE_PACK - 1:ROPE_PACK]
    for g in range(ROPE_PACK - 2, -1, -1):
        spread = jnp.where(lane < (g + 1) * half, pos[:, g:g + 1], spread)
    ang = spread * invf_ref[...]
    cos, sin = jnp.cos(ang), jnp.sin(ang)
    for g in range(ROPE_PACK):
        c = pltpu.roll(cos, LANE - g * half, 1) if g else cos
        s = pltpu.roll(sin, LANE - g * half, 1) if g else sin
        in_lo = lane < half
        in_hi = (lane >= half) & (lane < MLA_ROPE)
        dst = pl.ds(g, rows, stride=ROPE_PACK)
        tab_ref[0, dst, :] = jnp.where(in_lo, c, jnp.where(in_hi, pltpu.roll(c, half, 1), 0.0))
        tab_ref[1, dst, :] = jnp.where(in_lo, -s, 0.0)
        tab_ref[2, dst, :] = jnp.where(in_hi, pltpu.roll(s, half, 1), 0.0)


def _mla_prep_kernel(p_ref, pos_ref, invf_ref, qg_ref, kvg_ref, wq_ref, wk_ref, wvt_ref,
                     q_ref, k_ref, vt_ref, tab_ref, *, q_rank, kv_rank, scale):
    _rope_tables(pos_ref, invf_ref, tab_ref)
    cos_m, sin_lo, sin_hi = tab_ref[0], tab_ref[1], tab_ref[2]

    q_lat = p_ref[:, 0:q_rank].astype(F32)
    q = _dot(_rms(q_lat, qg_ref[...]).astype(BF16), wq_ref[...])
    kv_lat = p_ref[:, q_rank:q_rank + kv_rank].astype(F32)
    kv_n = _rms(kv_lat, kvg_ref[...]).astype(BF16)
    k_nope = _dot(kv_n, wk_ref[...])
    vt_ref[...] = _dot_nt(wvt_ref[...], kv_n).astype(BF16)
    kr_off = q_rank + kv_rank
    k_rope = _rope_tile(p_ref[:, kr_off:kr_off + LANE].astype(F32),
                        cos_m, sin_lo, sin_hi).astype(BF16)
    for h in range(MLA_HEADS):
        o = h * MLA_QK_PAD
        q_ref[:, o:o + MLA_NOPE] = (q[:, o:o + MLA_NOPE] * scale).astype(BF16)
        qr = _rope_tile(q[:, o + MLA_NOPE:o + MLA_QK_PAD], cos_m, sin_lo, sin_hi)
        q_ref[:, o + MLA_NOPE:o + MLA_QK_PAD] = (qr * scale).astype(BF16)
        k_ref[:, o:o + MLA_NOPE] = k_nope[:, h * MLA_NOPE:(h + 1) * MLA_NOPE].astype(BF16)
        k_ref[:, o + MLA_NOPE:o + MLA_QK_PAD] = k_rope


def _mla_prep(proj, positions, invf, qg, kvg, wq, wk, wvt, q_rank, kv_rank, tm=512):
    bsz, s, blk0 = proj.shape
    hq = MLA_HEADS * MLA_QK_PAD
    hv = MLA_HEADS * MLA_V
    scale = float((MLA_NOPE + MLA_ROPE) ** -0.5 * np.log2(np.e))
    kern = functools.partial(_mla_prep_kernel, q_rank=q_rank, kv_rank=kv_rank, scale=scale)

    def const(a):
        return pl.BlockSpec(a.shape, lambda b, i: (0, 0))

    return pl.pallas_call(
        kern,
        grid=(bsz, s // tm),
        in_specs=[pl.BlockSpec((None, tm, blk0), lambda b, i: (b, i, 0)),
                  pl.BlockSpec((None, tm // ROPE_PACK, ROPE_PACK), lambda b, i: (b, i, 0)),
                  const(invf), const(qg), const(kvg), const(wq), const(wk), const(wvt)],
        out_specs=[pl.BlockSpec((None, tm, hq), lambda b, i: (b, i, 0)),
                   pl.BlockSpec((None, tm, hq), lambda b, i: (b, i, 0)),
                   pl.BlockSpec((None, hv, tm), lambda b, i: (b, 0, i))],
        out_shape=[jax.ShapeDtypeStruct((bsz, s, hq), BF16),
                   jax.ShapeDtypeStruct((bsz, s, hq), BF16),
                   jax.ShapeDtypeStruct((bsz, hv, s), BF16)],
        scratch_shapes=[pltpu.VMEM((3, tm, LANE), F32)],
        compiler_params=_cparams("parallel", "parallel"),
        name="mla_prep",
    )(proj, positions.reshape(bsz, s // ROPE_PACK, ROPE_PACK), invf, qg, kvg, wq, wk, wvt)


def _attn_kernel(*refs, tq, tk, n_cast):
    q_ref, k_ref, vt_ref = refs[:3]
    o_ref = refs[3 + n_cast]
    sa_ref, sb_ref = refs[-2:]
    for src, dst in zip(refs[3:3 + n_cast], refs[4 + n_cast:4 + 2 * n_cast]):
        dst[...] = src[...].astype(dst.dtype)
    s_len = k_ref.shape[0]
    n_kv = s_len // tk
    n_q = s_len // tq
    sub = 8

    def scores(i, s_ref):
        q = q_ref[pl.ds(pl.multiple_of(i * tq, tq), tq), :]
        m8 = jnp.full((sub, tq), -jnp.inf, F32)
        for j in range(n_kv):
            st = _dot_nt(k_ref[j * tk:(j + 1) * tk, :], q)
            s_ref[j * tk:(j + 1) * tk, :] = st
            m8 = jnp.maximum(m8, jnp.max(st.reshape(tk // sub, sub, tq), axis=0))
            yield
        return jnp.broadcast_to(jnp.max(m8, axis=0, keepdims=True), (sub, tq))

    def values(i, s_ref, m8):
        l8 = jnp.zeros((sub, tq), F32)
        acc = jnp.zeros((MLA_V, tq), F32)
        for j in range(n_kv):
            st = s_ref[j * tk:(j + 1) * tk, :].reshape(tk // sub, sub, tq)
            p = jnp.exp2(st - m8[None])
            l8 = l8 + jnp.sum(p, axis=0)
            acc = acc + _dot(vt_ref[:, j * tk:(j + 1) * tk],
                             p.reshape(tk, tq).astype(BF16))
            yield
        l = jnp.sum(l8, axis=0, keepdims=True)
        o_ref[pl.ds(pl.multiple_of(i * tq, tq), tq), :] = (acc / l).T.astype(o_ref.dtype)

    def run(*gens):
        result = None
        live = list(gens)
        while live:
            for g in list(live):
                try:
                    next(g)
                except StopIteration as stop:
                    if g is gens[0]:
                        result = stop.value
                    live.remove(g)
        return result

    bufs = (sa_ref, sb_ref)
    group = ATTN_TILES_PER_TRIP

    def tiles(first, count, m):
        for u in range(count):
            m = run(scores(first + u + 1, bufs[(u + 1) % 2]),
                    values(first + u, bufs[u % 2], m))
        return m

    m = lax.fori_loop(0, n_q // group - 1, lambda k, m: tiles(group * k, group, m),
                      run(scores(0, bufs[0])))
    m = tiles(n_q - group, group - 1, m)
    run(values(n_q - 1, bufs[(group - 1) % 2], m))


def _attention(q, k, vt, f32_weights, tq=256, tk=512):
    bsz, s, _ = q.shape
    steps = bsz * MLA_HEADS
    kern = functools.partial(_attn_kernel, tq=tq, tk=tk, n_cast=len(f32_weights))

    def slab(w):
        return pl.BlockSpec((w.shape[0] // steps, w.shape[1]),
                            lambda b, h: (b * MLA_HEADS + h, 0))

    assert all(w.shape[0] % (8 * steps) == 0 for w in f32_weights)
    out = pl.pallas_call(
        kern,
        grid=(bsz, MLA_HEADS),
        in_specs=[pl.BlockSpec((None, s, MLA_QK_PAD), lambda b, h: (b, 0, h)),
                  pl.BlockSpec((None, s, MLA_QK_PAD), lambda b, h: (b, 0, h)),
                  pl.BlockSpec((None, MLA_V, s), lambda b, h: (b, h, 0))]
                 + [slab(w) for w in f32_weights],
        out_specs=[pl.BlockSpec((None, s, MLA_V), lambda b, h: (b, 0, h))]
                  + [slab(w) for w in f32_weights],
        out_shape=[jax.ShapeDtypeStruct((bsz, s, MLA_HEADS * MLA_V), BF16)]
                  + [jax.ShapeDtypeStruct(w.shape, BF16) for w in f32_weights],
        scratch_shapes=[pltpu.VMEM((s, tq), F32), pltpu.VMEM((s, tq), F32)],
        compiler_params=_cparams("parallel", "parallel"),
        name="attention",
    )(q, k, vt, *f32_weights)
    return out[0], out[1:]


def _split_bf16(x):
    hi = x.astype(BF16)
    return hi, (x - hi.astype(F32)).astype(BF16)


def _run_staggered(gens):
    live = list(enumerate(gens))
    tick = 0
    while live:
        for delay, g in list(live):
            if tick >= delay and next(g, StopIteration) is StopIteration:
                live.remove((delay, g))
        tick += 1


def _hgrn_exact_block(q, k, b, v_ref, qs, states, o_ref, k_s, b_s, v_s, *, forward):
    n, c, _ = q.shape
    k_s[...] = k.reshape(n * c, HGRN_DK)
    b_s[...] = b.reshape(n * c, HGRN_DK)
    v_s[...] = v_ref[...].astype(F32)
    t_idx = lax.broadcasted_iota(jnp.int32, (c, 1), 0)
    for j in range(n):
        def key_row(s, acc, j=j):
            row = pl.ds(j * c + s, 1)
            w = jnp.exp2(jnp.minimum(b[j] - b_s[row, :], 0.0))
            seen = (t_idx >= s) if forward else (t_idx <= s)
            a_col = jnp.sum(jnp.where(seen, q[j] * w * k_s[row, :], 0.0),
                            axis=1, keepdims=True)
            return acc + a_col * v_s[row, :]

        intra = lax.fori_loop(0, c, key_row, jnp.zeros((c, HGRN_DV), F32))
        inter = _dot_nt(qs[j], states[j])
        o_ref[j * c:(j + 1) * c, :] = (inter + intra).astype(o_ref.dtype)


def _hgrn_direction(q_ref, z_ref, v_ref, lb, st_ref, o_ref, exact_scratch, *, forward):
    c = HGRN_CHUNK
    n = q_ref.shape[0] // c
    row = lax.broadcasted_iota(jnp.int32, (c, c), 0)
    col = lax.broadcasted_iota(jnp.int32, (c, c), 1)
    causal = (col <= row) if forward else (col >= row)
    tri = causal.astype(BF16)
    edge = c - 1 if forward else 0

    chunks = [slice(j * c, (j + 1) * c) for j in range(n)]
    f = lb + (1.0 - lb) * jax.nn.sigmoid(z_ref[...].astype(F32))
    k = (1.0 - f).reshape(n, c, HGRN_DK)
    log_f = jnp.concatenate(_split_bf16(jnp.log2(f)), axis=1)
    q = q_ref[...].reshape(n, c, HGRN_DK)
    vt = v_ref[...].astype(F32).T.astype(BF16)
    yield
    b2 = jnp.concatenate([_dot(tri, log_f[r]) for r in chunks], axis=0)
    yield
    b = (b2[:, :HGRN_DK] + b2[:, HGRN_DK:]).reshape(n, c, HGRN_DK)
    b_edge = b[:, edge:edge + 1]
    kb = k.astype(BF16)
    qs = q * jnp.exp2(b).astype(BF16)
    ki = kb * jnp.exp2(-b).astype(BF16)
    kd = kb * jnp.exp2(b_edge - b).astype(BF16)
    decay = jnp.exp2(b_edge)
    spread = jnp.max(-b_edge)
    yield
    a = [_dot_nt(qs[j], ki[j]) for j in range(n)]
    upd = [_dot(vt[:, chunks[j]], kd[j]) for j in range(n)]
    yield
    a = [jnp.where(causal, x, 0.0).astype(BF16) for x in a]
    st = st_ref[...]
    states = [None] * n
    for j in (range(n) if forward else reversed(range(n))):
        states[j] = st.astype(BF16)
        st = st * decay[j] + upd[j]
    st_ref[...] = st
    yield
    outs = [_dot_nt(jnp.concatenate([a[j], qs[j]], axis=1),
                    jnp.concatenate([vt[:, chunks[j]], states[j]], axis=1))
            for j in range(n)]
    o_ref[...] = jnp.concatenate(outs, axis=0).astype(o_ref.dtype)
    yield
    yield

    @pl.when(spread > HGRN_SAFE_LOG2_SPREAD)
    def _():
        _hgrn_exact_block(q.astype(F32), k, b, v_ref, qs, states, o_ref, *exact_scratch,
                          forward=forward)
    yield


def _hgrn_kernel(qf_ref, zf_ref, vf_ref, qb_ref, zb_ref, vb_ref, lbf_ref, lbb_ref,
                 of_ref, ob_ref, stf_ref, stb_ref, k_s, b_s, v_s):
    @pl.when(pl.program_id(2) == 0)
    def _():
        stf_ref[...] = jnp.zeros_like(stf_ref)
        stb_ref[...] = jnp.zeros_like(stb_ref)

    def lower_bound(ref):
        l = ref[...]
        e = jnp.exp(l - jnp.max(l, axis=0, keepdims=True))
        return e[0:1, :] / jnp.sum(e, axis=0, keepdims=True)

    fwd = _hgrn_direction(qf_ref, zf_ref, vf_ref, lower_bound(lbf_ref), stf_ref, of_ref,
                          (k_s, b_s, v_s), forward=True)
    bwd = _hgrn_direction(qb_ref, zb_ref, vb_ref, lower_bound(lbb_ref), stb_ref, ob_ref,
                          (k_s, b_s, v_s), forward=False)
    _run_staggered([fwd, bwd])


def _hgrn(proj, lbf_logits, lbb_logits, col_q, col_ff, col_fb, col_v, tb=1024):
    bsz, _, s, _ = proj.shape
    nb = s // tb

    def fwd(col):
        return pl.BlockSpec((None, None, tb, LANE), lambda b, h, i: (b, col + h, i, 0))

    def bwd(col):
        return pl.BlockSpec((None, None, tb, LANE),
                            lambda b, h, i: (b, col + h, nb - 1 - i, 0))

    lb_spec = pl.BlockSpec((lbf_logits.shape[0], LANE), lambda b, h, i: (0, h))
    out_shape = jax.ShapeDtypeStruct((bsz, HGRN_HEADS, s, HGRN_DV), BF16)
    return pl.pallas_call(
        _hgrn_kernel,
        grid=(bsz, HGRN_HEADS, nb),
        in_specs=[fwd(col_q), fwd(col_ff), fwd(col_v),
                  bwd(col_q), bwd(col_fb), bwd(col_v), lb_spec, lb_spec],
        out_specs=[fwd(0), bwd(0)],
        out_shape=[out_shape, out_shape],
        scratch_shapes=[pltpu.VMEM((HGRN_DV, HGRN_DK), F32),
                        pltpu.VMEM((HGRN_DV, HGRN_DK), F32),
                        pltpu.VMEM((tb, HGRN_DK), F32), pltpu.VMEM((tb, HGRN_DK), F32),
                        pltpu.VMEM((tb, HGRN_DV), F32)],
        compiler_params=_cparams("parallel", "parallel", "arbitrary"),
        name="hgrn",
    )(proj, proj, proj, proj, proj, proj, lbf_logits, lbb_logits)


def _mix_out_kernel(attn_ref, of_ref, ob_ref, hg_ref, hgn_ref, w_ref, x_ref, mod_ref,
                    postg_ref, preg_ref, x1_ref, h2_ref):
    na = attn_ref.shape[-1]
    tm = attn_ref.shape[0]
    tiles = [slice(r, r + MIX_SUB_ROWS) for r in range(0, tm, MIX_SUB_ROWS)]
    recs = []
    for t in tiles:
        rec = []
        for h in range(HGRN_HEADS):
            o = of_ref[h, t, :].astype(F32) + ob_ref[h, t, :].astype(F32)
            hg = hg_ref[h, t, :].astype(F32)
            rec.append((_rms(o, hgn_ref[...]) * (hg * jax.nn.sigmoid(hg))).astype(BF16))
        recs.append(jnp.concatenate(rec, axis=-1))
    mixes = [_dot(attn_ref[t, :], w_ref[0:na, :]) + _dot(rec, w_ref[na:, :])
             for t, rec in zip(tiles, recs)]
    post_gain = mod_ref[2:3, :] * postg_ref[...]
    pre_gain = preg_ref[...] * (1.0 + mod_ref[4:5, :])
    for t, mix in zip(tiles, mixes):
        x1 = x_ref[t, :] + _rms(mix, post_gain)
        x1_ref[t, :] = x1
        h2_ref[t, :] = (_rms(x1, pre_gain) + mod_ref[3:4, :]).astype(BF16)


def _mix_out(attn, o_f, o_b, proj, col_hg, hgn_g, w_out, x, mod3, post_g, pre_g, tm=512):
    bsz, s, d = x.shape
    na = attn.shape[-1]
    heads = o_f.shape[1]

    def row(width):
        return pl.BlockSpec((None, tm, width), lambda b, i: (b, i, 0))

    def per_head(first):
        return pl.BlockSpec((None, heads, tm, LANE), lambda b, i: (b, first // heads, i, 0))

    def const(shape):
        return pl.BlockSpec(shape, lambda b, i: (0,) * len(shape))

    assert col_hg % heads == 0
    return pl.pallas_call(
        _mix_out_kernel,
        grid=(bsz, s // tm),
        in_specs=[row(na), per_head(0), per_head(0), per_head(col_hg),
                  const((1, HGRN_DV)), const(w_out.shape), row(d),
                  pl.BlockSpec((None, N_MOD, d), lambda b, i: (b, 0, 0)),
                  const((1, d)), const((1, d))],
        out_specs=[row(d), row(d)],
        out_shape=[jax.ShapeDtypeStruct((bsz, s, d), F32),
                   jax.ShapeDtypeStruct((bsz, s, d), BF16)],
        compiler_params=_cparams("parallel", "parallel"),
        name="mix_out",
    )(attn, o_f, o_b, proj, hgn_g, w_out, x, mod3, post_g, pre_g)


def _mlp_kernel(h_ref, wu_ref, wd_ref, x1_ref, mod_ref, g_ref, o_ref):
    j = pl.program_id(2)
    last = pl.num_programs(2) - 1

    @pl.when(j == 0)
    def _():
        o_ref[...] = jnp.zeros_like(o_ref)

    def activation():
        u = jnp.maximum(_dot(h_ref[...], wu_ref[...]), 0.0)
        return (u * u).astype(BF16)

    @pl.when(j < last)
    def _():
        o_ref[...] += _dot(activation(), wd_ref[...])

    @pl.when(j == last)
    def _():
        a = activation()
        gain = mod_ref[5:6, :] * g_ref[...]
        for r in range(0, o_ref.shape[0], MLP_SUB_ROWS):
            t = slice(r, r + MLP_SUB_ROWS)
            y = o_ref[t, :] + _dot(a[t, :], wd_ref[...])
            o_ref[t, :] = x1_ref[t, :] + _rms(y, gain)


def _mlp(h2, w_up, w_down, x1, mod3, g, tm=512, tf=1024):
    bsz, s, d = x1.shape
    dff = w_up.shape[1]
    return pl.pallas_call(
        _mlp_kernel,
        grid=(bsz, s // tm, dff // tf),
        in_specs=[pl.BlockSpec((None, tm, d), lambda b, i, j: (b, i, 0)),
                  pl.BlockSpec((d, tf), lambda b, i, j: (0, j)),
                  pl.BlockSpec((tf, d), lambda b, i, j: (j, 0)),
                  pl.BlockSpec((None, tm, d), lambda b, i, j: (b, i, 0)),
                  pl.BlockSpec((None, N_MOD, d), lambda b, i, j: (b, 0, 0)),
                  pl.BlockSpec((1, d), lambda b, i, j: (0, 0))],
        out_specs=pl.BlockSpec((None, tm, d), lambda b, i, j: (b, i, 0)),
        out_shape=jax.ShapeDtypeStruct((bsz, s, d), F32),
        compiler_params=_cparams("parallel", "parallel", "arbitrary"),
        name="mlp",
    )(h2, w_up, w_down, x1, mod3, g)


def _pack_w_uq(w):
    r = w.shape[0]
    w = w.reshape(r, MLA_HEADS, MLA_NOPE + MLA_ROPE)
    pad = jnp.zeros((r, MLA_HEADS, MLA_QK_PAD - MLA_NOPE - MLA_ROPE), w.dtype)
    return jnp.concatenate([w, pad], axis=-1).reshape(r, MLA_HEADS * MLA_QK_PAD).astype(BF16)


def _pack_w_ukv(w):
    r = w.shape[0]
    w = w.reshape(r, MLA_HEADS, MLA_NOPE + MLA_V)
    wk = w[:, :, :MLA_NOPE].reshape(r, -1).astype(BF16)
    wvt = w[:, :, MLA_NOPE:].reshape(r, -1).T.astype(BF16)
    return wk, wvt


def _rope_inv_freq():
    half = MLA_ROPE // 2
    f = ROPE_BASE ** (-jnp.arange(half, dtype=F32) / half)
    return jnp.tile(f, ROPE_PACK).reshape(1, LANE)


def kernel(x, c, positions, w_mod, b_mod, pre_mix_g, post_mix_g, pre_mlp_g, post_mlp_g,
           w_in, q_norm_g, kv_norm_g, w_uq, w_ukv, hgrn_norm_g, hgrn_lb_logits_fwd,
           hgrn_lb_logits_bwd, w_out, w_up, w_down):
    depth = w_mod.shape[0]
    assert depth == 1, "lower-bound slot selection is written for a single layer"
    bsz, s, d = x.shape
    q_rank = q_norm_g.shape[-1]
    kv_rank = kv_norm_g.shape[-1]
    invf = _rope_inv_freq()
    for layer in range(depth):
        mod3 = _mod(c, w_mod[layer], b_mod[layer]).reshape(bsz, N_MOD, d)
        lat, proj = _in_proj(x, mod3, pre_mix_g[layer].reshape(1, d),
                             w_in[layer].T.astype(BF16), q_rank + kv_rank + MLA_ROPE)
        wk, wvt = _pack_w_ukv(w_ukv[layer])
        q, k, vt = _mla_prep(lat, positions, invf, q_norm_g[layer].reshape(1, -1),
                             kv_norm_g[layer].reshape(1, -1), _pack_w_uq(w_uq[layer]),
                             wk, wvt, q_rank, kv_rank)
        attn, (w_out_b, w_up_b, w_down_b) = _attention(
            q, k, vt, (w_out[layer], w_up[layer], w_down[layer]))
        o_f, o_b = _hgrn(proj, hgrn_lb_logits_fwd, hgrn_lb_logits_bwd,
                         col_q=0, col_ff=8, col_fb=16, col_v=24)
        x1, h2 = _mix_out(attn, o_f, o_b, proj, 32, hgrn_norm_g[layer].reshape(1, -1),
                          w_out_b, x, mod3,
                          post_mix_g[layer].reshape(1, d), pre_mlp_g[layer].reshape(1, d))
        x = _mlp(h2, w_up_b, w_down_b, x1, mod3, post_mlp_g[layer].reshape(1, d))
    return x
```

```python
import functools

import jax
import jax.numpy as jnp
import numpy as np
from jax import lax
from jax.experimental import pallas as pl
from jax.experimental.pallas import tpu as pltpu

F32 = jnp.float32
BF16 = jnp.bfloat16

NORM_EPS = 1e-6
ROPE_BASE = 10000.0
N_MOD = 6
LANE = 128
VMEM_LIMIT = 56 * 1024 * 1024

MLA_HEADS = 8
MLA_NOPE = 128
MLA_ROPE = 64
MLA_V = 128
MLA_QK_PAD = 256
ROPE_PACK = LANE // (MLA_ROPE // 2)
ATTN_TILES_PER_TRIP = 4
MLA_LAT_PAD = 1024
HGRN_HEADS = 8
HGRN_DK = 128
HGRN_DV = 128
HGRN_CHUNK = 64
HGRN_HEADS_PER_STEP = 2
HGRN_SAFE_LOG2_SPREAD = 96.0
MLP_SUB_ROWS = 128
MIX_SUB_ROWS = 128


def _cparams(*sem):
    return pltpu.CompilerParams(dimension_semantics=sem, vmem_limit_bytes=VMEM_LIMIT)


def _dot(a, b):
    return jnp.dot(a, b, preferred_element_type=F32)


def _dot_nt(a, b):
    return lax.dot_general(a, b, (((1,), (1,)), ((), ())), preferred_element_type=F32)


def _dot_tn(a, b):
    return lax.dot_general(a, b, (((0,), (0,)), ((), ())), preferred_element_type=F32)


def _rms(x, g):
    return x * lax.rsqrt(jnp.mean(x * x, axis=-1, keepdims=True) + NORM_EPS) * g


def _mod_kernel(c_ref, w_ref, b_ref, o_ref):
    c = c_ref[...]
    cond = c * jax.nn.sigmoid(c)
    hi = cond.astype(BF16)
    lo = (cond - hi.astype(F32)).astype(BF16)
    w = w_ref[...].astype(BF16)
    o_ref[...] = _dot(hi, w) + _dot(lo, w) + b_ref[...]


def _mod(c, w_mod, b_mod, tn=1024):
    bsz, d = c.shape
    n = w_mod.shape[1]
    return pl.pallas_call(
        _mod_kernel,
        grid=(n // tn,),
        in_specs=[pl.BlockSpec((bsz, d), lambda j: (0, 0)),
                  pl.BlockSpec((d, tn), lambda j: (0, j)),
                  pl.BlockSpec((1, tn), lambda j: (0, j))],
        out_specs=pl.BlockSpec((bsz, tn), lambda j: (0, j)),
        out_shape=jax.ShapeDtypeStruct((bsz, n), F32),
        compiler_params=_cparams("parallel"),
        name="mod",
    )(c, w_mod, b_mod.reshape(1, n))


def _in_proj_kernel(x_ref, mod_ref, g_ref, wa_ref, wb_ref, oa_ref, ob_ref, h_ref):
    @pl.when(pl.program_id(2) == 0)
    def _():
        h = _rms(x_ref[...], g_ref[...]) * (1.0 + mod_ref[1:2, :]) + mod_ref[0:1, :]
        h = h.astype(BF16)
        h_ref[...] = h
        oa_ref[...] = _dot_nt(h, wa_ref[...]).astype(oa_ref.dtype)

    y = _dot_nt(h_ref[...], wb_ref[...]).astype(ob_ref.dtype)
    for c in range(ob_ref.shape[0]):
        ob_ref[c] = y[:, c * LANE:(c + 1) * LANE]


def _in_proj(x, mod3, g, wt, head, tm=1024, tn=1280):
    bsz, s, d = x.shape
    na, nb = MLA_LAT_PAD, wt.shape[0] - head
    row_align = int(np.gcd(head, tn))
    assert row_align % 16 == 0
    return pl.pallas_call(
        _in_proj_kernel,
        grid=(bsz, s // tm, nb // tn),
        in_specs=[pl.BlockSpec((None, tm, d), lambda b, i, j: (b, i, 0)),
                  pl.BlockSpec((None, N_MOD, d), lambda b, i, j: (b, 0, 0)),
                  pl.BlockSpec((1, d), lambda b, i, j: (0, 0)),
                  pl.BlockSpec((pl.Element(na), pl.Element(d)), lambda b, i, j: (0, 0)),
                  pl.BlockSpec((pl.Element(tn), pl.Element(d)),
                               lambda b, i, j: (pl.multiple_of(head + j * tn, row_align), 0))],
        out_specs=[pl.BlockSpec((None, tm, na), lambda b, i, j: (b, i, 0)),
                   pl.BlockSpec((None, tn // LANE, tm, LANE), lambda b, i, j: (b, j, i, 0))],
        out_shape=[jax.ShapeDtypeStruct((bsz, s, na), BF16),
                   jax.ShapeDtypeStruct((bsz, nb // LANE, s, LANE), BF16)],
        scratch_shapes=[pltpu.VMEM((tm, d), BF16)],
        compiler_params=_cparams("parallel", "parallel", "arbitrary"),
        name="in_proj",
    )(x, mod3, g, wt, wt)


def _rope_tile(t, cos_m, sin_lo, sin_hi):
    half = MLA_ROPE // 2
    return (t * cos_m + pltpu.roll(t, LANE - half, 1) * sin_lo
            + pltpu.roll(t, half, 1) * sin_hi)


def _rope_tables(pos_ref, invf_ref, tab_ref):
    half = MLA_ROPE // 2
    rows = pos_ref.shape[0]
    pos = pos_ref[...].astype(F32)
    lane = lax.broadcasted_iota(jnp.int32, (rows, LANE), 1)
    spread = pos[:, ROPE_PACK - 1:ROPE_PACK]
    for g in range(ROPE_PACK - 2, -1, -1):
        spread = jnp.where(lane < (g + 1) * half, pos[:, g:g + 1], spread)
    ang = spread * invf_ref[...]
    cos, sin = jnp.cos(ang), jnp.sin(ang)
    for g in range(ROPE_PACK):
        c = pltpu.roll(cos, LANE - g * half, 1) if g else cos
        s = pltpu.roll(sin, LANE - g * half, 1) if g else sin
        in_lo = lane < half
        in_hi = (lane >= half) & (lane < MLA_ROPE)
        dst = pl.ds(g, rows, stride=ROPE_PACK)
        tab_ref[0, dst, :] = jnp.where(in_lo, c, jnp.where(in_hi, pltpu.roll(c, half, 1), 0.0))
        tab_ref[1, dst, :] = jnp.where(in_lo, -s, 0.0)
        tab_ref[2, dst, :] = jnp.where(in_hi, pltpu.roll(s, half, 1), 0.0)


def _mla_prep_kernel(p_ref, pos_ref, invf_ref, qg_ref, kvg_ref, wq_ref, wk_ref, wvt_ref,
                     q_ref, k_ref, vt_ref, tab_ref, *, q_rank, kv_rank, scale):
    _rope_tables(pos_ref, invf_ref, tab_ref)
    cos_m, sin_lo, sin_hi = tab_ref[0], tab_ref[1], tab_ref[2]

    q_lat = p_ref[:, 0:q_rank].astype(F32)
    q = _dot(_rms(q_lat, qg_ref[...]).astype(BF16), wq_ref[...])
    kv_lat = p_ref[:, q_rank:q_rank + kv_rank].astype(F32)
    kv_n = _rms(kv_lat, kvg_ref[...]).astype(BF16)
    k_nope = _dot(kv_n, wk_ref[...])
    vt_ref[...] = _dot_nt(wvt_ref[...], kv_n).astype(BF16)
    kr_off = q_rank + kv_rank
    k_rope = _rope_tile(p_ref[:, kr_off:kr_off + LANE].astype(F32),
                        cos_m, sin_lo, sin_hi).astype(BF16)
    for h in range(MLA_HEADS):
        o = h * MLA_QK_PAD
        q_ref[:, o:o + MLA_NOPE] = (q[:, o:o + MLA_NOPE] * scale).astype(BF16)
        qr = _rope_tile(q[:, o + MLA_NOPE:o + MLA_QK_PAD], cos_m, sin_lo, sin_hi)
        q_ref[:, o + MLA_NOPE:o + MLA_QK_PAD] = (qr * scale).astype(BF16)
        k_ref[:, o:o + MLA_NOPE] = k_nope[:, h * MLA_NOPE:(h + 1) * MLA_NOPE].astype(BF16)
        k_ref[:, o + MLA_NOPE:o + MLA_QK_PAD] = k_rope


def _mla_prep(proj, positions, invf, qg, kvg, wq, wk, wvt, q_rank, kv_rank, tm=512):
    bsz, s, blk0 = proj.shape
    hq = MLA_HEADS * MLA_QK_PAD
    hv = MLA_HEADS * MLA_V
    scale = float((MLA_NOPE + MLA_ROPE) ** -0.5 * np.log2(np.e))
    kern = functools.partial(_mla_prep_kernel, q_rank=q_rank, kv_rank=kv_rank, scale=scale)

    def const(a):
        return pl.BlockSpec(a.shape, lambda b, i: (0, 0))

    return pl.pallas_call(
        kern,
        grid=(bsz, s // tm),
        in_specs=[pl.BlockSpec((None, tm, blk0), lambda b, i: (b, i, 0)),
                  pl.BlockSpec((None, tm // ROPE_PACK, ROPE_PACK), lambda b, i: (b, i, 0)),
                  const(invf), const(qg), const(kvg), const(wq), const(wk), const(wvt)],
        out_specs=[pl.BlockSpec((None, tm, hq), lambda b, i: (b, i, 0)),
                   pl.BlockSpec((None, tm, hq), lambda b, i: (b, i, 0)),
                   pl.BlockSpec((None, hv, tm), lambda b, i: (b, 0, i))],
        out_shape=[jax.ShapeDtypeStruct((bsz, s, hq), BF16),
                   jax.ShapeDtypeStruct((bsz, s, hq), BF16),
                   jax.ShapeDtypeStruct((bsz, hv, s), BF16)],
        scratch_shapes=[pltpu.VMEM((3, tm, LANE), F32)],
        compiler_params=_cparams("parallel", "parallel"),
        name="mla_prep",
    )(proj, positions.reshape(bsz, s // ROPE_PACK, ROPE_PACK), invf, qg, kvg, wq, wk, wvt)


def _attn_kernel(*refs, tq, tk, n_cast):
    q_ref, k_ref, vt_ref = refs[:3]
    o_ref = refs[3 + n_cast]
    sa_ref, sb_ref = refs[-2:]
    for src, dst in zip(refs[3:3 + n_cast], refs[4 + n_cast:4 + 2 * n_cast]):
        dst[...] = src[...].astype(dst.dtype)
    s_len = k_ref.shape[0]
    n_kv = s_len // tk
    n_q = s_len // tq
    sub = 8

    def scores(i, s_ref):
        q = q_ref[pl.ds(pl.multiple_of(i * tq, tq), tq), :]
        m8 = jnp.full((sub, tq), -jnp.inf, F32)
        for j in range(n_kv):
            st = _dot_nt(k_ref[j * tk:(j + 1) * tk, :], q)
            s_ref[j * tk:(j + 1) * tk, :] = st
            m8 = jnp.maximum(m8, jnp.max(st.reshape(tk // sub, sub, tq), axis=0))
            yield
        return jnp.broadcast_to(jnp.max(m8, axis=0, keepdims=True), (sub, tq))

    def values(i, s_ref, m8):
        l8 = jnp.zeros((sub, tq), F32)
        acc = jnp.zeros((MLA_V, tq), F32)
        for j in range(n_kv):
            st = s_ref[j * tk:(j + 1) * tk, :].reshape(tk // sub, sub, tq)
            p = jnp.exp2(st - m8[None])
            l8 = l8 + jnp.sum(p, axis=0)
            acc = acc + _dot(vt_ref[:, j * tk:(j + 1) * tk],
                             p.reshape(tk, tq).astype(BF16))
            yield
        l = jnp.sum(l8, axis=0, keepdims=True)
        o_ref[pl.ds(pl.multiple_of(i * tq, tq), tq), :] = (acc / l).T.astype(o_ref.dtype)

    def run(*gens):
        result = None
        live = list(gens)
        while live:
            for g in list(live):
                try:
                    next(g)
                except StopIteration as stop:
                    if g is gens[0]:
                        result = stop.value
                    live.remove(g)
        return result

    bufs = (sa_ref, sb_ref)
    group = ATTN_TILES_PER_TRIP

    def tiles(first, count, m):
        for u in range(count):
            m = run(scores(first + u + 1, bufs[(u + 1) % 2]),
                    values(first + u, bufs[u % 2], m))
        return m

    m = lax.fori_loop(0, n_q // group - 1, lambda k, m: tiles(group * k, group, m),
                      run(scores(0, bufs[0])))
    m = tiles(n_q - group, group - 1, m)
    run(values(n_q - 1, bufs[(group - 1) % 2], m))


def _attention(q, k, vt, f32_weights, tq=256, tk=512):
    bsz, s, _ = q.shape
    steps = bsz * MLA_HEADS
    kern = functools.partial(_attn_kernel, tq=tq, tk=tk, n_cast=len(f32_weights))

    def slab(w):
        return pl.BlockSpec((w.shape[0] // steps, w.shape[1]),
                            lambda b, h: (b * MLA_HEADS + h, 0))

    assert all(w.shape[0] % (8 * steps) == 0 for w in f32_weights)
    out = pl.pallas_call(
        kern,
        grid=(bsz, MLA_HEADS),
        in_specs=[pl.BlockSpec((None, s, MLA_QK_PAD), lambda b, h: (b, 0, h)),
                  pl.BlockSpec((None, s, MLA_QK_PAD), lambda b, h: (b, 0, h)),
                  pl.BlockSpec((None, MLA_V, s), lambda b, h: (b, h, 0))]
                 + [slab(w) for w in f32_weights],
        out_specs=[pl.BlockSpec((None, s, MLA_V), lambda b, h: (b, 0, h))]
                  + [slab(w) for w in f32_weights],
        out_shape=[jax.ShapeDtypeStruct((bsz, s, MLA_HEADS * MLA_V), BF16)]
                  + [jax.ShapeDtypeStruct(w.shape, BF16) for w in f32_weights],
        scratch_shapes=[pltpu.VMEM((s, tq), F32), pltpu.VMEM((s, tq), F32)],
        compiler_params=_cparams("parallel", "parallel"),
        name="attention",
    )(q, k, vt, *f32_weights)
    return out[0], out[1:]


def _run_staggered(gens):
    live = list(enumerate(gens))
    tick = 0
    while live:
        for delay, g in list(live):
            if tick >= delay and next(g, StopIteration) is StopIteration:
                live.remove((delay, g))
        tick += 1


def _hgrn_exact_block(q, k, b, v_ref, qs, states, o_ref, k_s, b_s, v_s, *, forward):
    n, c, _ = q.shape
    k_s[...] = k.reshape(n * c, HGRN_DK)
    b_s[...] = b.reshape(n * c, HGRN_DK)
    v_s[...] = v_ref[...].astype(F32)
    t_idx = lax.broadcasted_iota(jnp.int32, (c, 1), 0)
    for j in range(n):
        def key_row(s, acc, j=j):
            row = pl.ds(j * c + s, 1)
            w = jnp.exp2(jnp.minimum(b[j] - b_s[row, :], 0.0))
            seen = (t_idx >= s) if forward else (t_idx <= s)
            a_col = jnp.sum(jnp.where(seen, q[j] * w * k_s[row, :], 0.0),
                            axis=1, keepdims=True)
            return acc + a_col * v_s[row, :]

        intra = lax.fori_loop(0, c, key_row, jnp.zeros((c, HGRN_DV), F32))
        inter = _dot_nt(qs[j], states[j])
        o_ref[j * c:(j + 1) * c, :] = (inter + intra).astype(o_ref.dtype)


def _hgrn_direction(q_ref, z_ref, v_ref, lb, st_ref, o_ref, exact_scratch, *, forward, lag):
    c = HGRN_CHUNK
    n = q_ref.shape[0] // c
    row = lax.broadcasted_iota(jnp.int32, (c, c), 0)
    col = lax.broadcasted_iota(jnp.int32, (c, c), 1)
    causal = (col <= row) if forward else (col >= row)
    tri = causal.astype(BF16)
    edge = c - 1 if forward else 0

    chunks = [slice(j * c, (j + 1) * c) for j in range(n)]
    f = lb + (1.0 - lb) * jax.nn.sigmoid(z_ref[...].astype(F32))
    k = (1.0 - f).reshape(n, c, HGRN_DK)
    log_f = jnp.log2(f).astype(BF16)
    q = q_ref[...].reshape(n, c, HGRN_DK)
    vt = v_ref[...].astype(F32).T.astype(BF16)
    yield
    b = jnp.concatenate([_dot(tri, log_f[r]) for r in chunks], axis=0)
    yield
    b = b.reshape(n, c, HGRN_DK)
    b_edge = b[:, edge:edge + 1]
    kb = k.astype(BF16)
    qs = q * jnp.exp2(b).astype(BF16)
    ki = kb * jnp.exp2(-b).astype(BF16)
    kd = kb * jnp.exp2(b_edge - b).astype(BF16)
    decay = jnp.exp2(b_edge)
    spread = jnp.max(-b_edge)
    yield
    a = [_dot_nt(qs[j], ki[j]) for j in range(n)]
    upd = [_dot(vt[:, chunks[j]], kd[j]) for j in range(n)]
    yield
    a = [jnp.where(causal, x, 0.0).astype(BF16) for x in a]
    st = st_ref[...]
    states = [None] * n
    for j in (range(n) if forward else reversed(range(n))):
        states[j] = st.astype(BF16)
        st = st * decay[j] + upd[j]
    st_ref[...] = st
    yield
    outs = [_dot_nt(jnp.concatenate([a[j], qs[j]], axis=1),
                    jnp.concatenate([vt[:, chunks[j]], states[j]], axis=1))
            for j in range(n)]
    o_ref[...] = jnp.concatenate(outs, axis=0).astype(o_ref.dtype)
    yield
    for _ in range(lag):
        yield

    @pl.when(spread > HGRN_SAFE_LOG2_SPREAD)
    def _():
        _hgrn_exact_block(q.astype(F32), k, b, v_ref, qs, states, o_ref, *exact_scratch,
                          forward=forward)
    yield


def _hgrn_kernel(qf_ref, zf_ref, vf_ref, qb_ref, zb_ref, vb_ref, lbf_ref, lbb_ref,
                 of_ref, ob_ref, stf_ref, stb_ref, k_s, b_s, v_s):
    @pl.when(pl.program_id(2) == 0)
    def _():
        stf_ref[...] = jnp.zeros_like(stf_ref)
        stb_ref[...] = jnp.zeros_like(stb_ref)

    def lower_bound(ref, head):
        l = ref[:, head * LANE:(head + 1) * LANE]
        e = jnp.exp(l - jnp.max(l, axis=0, keepdims=True))
        return e[0:1, :] / jnp.sum(e, axis=0, keepdims=True)

    scans = []
    lag = 2 * qf_ref.shape[0] - 1
    for h in range(qf_ref.shape[0]):
        scans.append(_hgrn_direction(qf_ref.at[h], zf_ref.at[h], vf_ref.at[h],
                                     lower_bound(lbf_ref, h), stf_ref.at[h], of_ref.at[h],
                                     (k_s, b_s, v_s), forward=True, lag=lag))
        scans.append(_hgrn_direction(qb_ref.at[h], zb_ref.at[h], vb_ref.at[h],
                                     lower_bound(lbb_ref, h), stb_ref.at[h], ob_ref.at[h],
                                     (k_s, b_s, v_s), forward=False, lag=lag))
    _run_staggered(scans)


def _hgrn(proj, lbf_logits, lbb_logits, col_q, col_ff, col_fb, col_v, tb=1024):
    bsz, _, s, _ = proj.shape
    nb = s // tb
    hps = HGRN_HEADS_PER_STEP

    def fwd(col):
        return pl.BlockSpec((None, hps, tb, LANE), lambda b, h, i: (b, col // hps + h, i, 0))

    def bwd(col):
        return pl.BlockSpec((None, hps, tb, LANE),
                            lambda b, h, i: (b, col // hps + h, nb - 1 - i, 0))

    assert all(col % hps == 0 for col in (col_q, col_ff, col_fb, col_v))
    lb_spec = pl.BlockSpec((lbf_logits.shape[0], hps * LANE), lambda b, h, i: (0, h))
    out_shape = jax.ShapeDtypeStruct((bsz, HGRN_HEADS, s, HGRN_DV), BF16)
    return pl.pallas_call(
        _hgrn_kernel,
        grid=(bsz, HGRN_HEADS // hps, nb),
        in_specs=[fwd(col_q), fwd(col_ff), fwd(col_v),
                  bwd(col_q), bwd(col_fb), bwd(col_v), lb_spec, lb_spec],
        out_specs=[fwd(0), bwd(0)],
        out_shape=[out_shape, out_shape],
        scratch_shapes=[pltpu.VMEM((hps, HGRN_DV, HGRN_DK), F32),
                        pltpu.VMEM((hps, HGRN_DV, HGRN_DK), F32),
                        pltpu.VMEM((tb, HGRN_DK), F32), pltpu.VMEM((tb, HGRN_DK), F32),
                        pltpu.VMEM((tb, HGRN_DV), F32)],
        compiler_params=_cparams("parallel", "parallel", "arbitrary"),
        name="hgrn",
    )(proj, proj, proj, proj, proj, proj, lbf_logits, lbb_logits)


def _mix_out_kernel(attn_ref, of_ref, ob_ref, hg_ref, hgn_ref, w_ref, x_ref, mod_ref,
                    postg_ref, preg_ref, x1_ref, h2_ref):
    na = attn_ref.shape[-1]
    tm = attn_ref.shape[0]
    tiles = [slice(r, r + MIX_SUB_ROWS) for r in range(0, tm, MIX_SUB_ROWS)]
    recs = []
    for t in tiles:
        rec = []
        for h in range(HGRN_HEADS):
            o = of_ref[h, t, :].astype(F32) + ob_ref[h, t, :].astype(F32)
            hg = hg_ref[h, t, :].astype(F32)
            rec.append((_rms(o, hgn_ref[...]) * (hg * jax.nn.sigmoid(hg))).astype(BF16))
        recs.append(jnp.concatenate(rec, axis=-1))
    mixes = [_dot(attn_ref[t, :], w_ref[0:na, :]) + _dot(rec, w_ref[na:, :])
             for t, rec in zip(tiles, recs)]
    post_gain = mod_ref[2:3, :] * postg_ref[...]
    pre_gain = preg_ref[...] * (1.0 + mod_ref[4:5, :])
    for t, mix in zip(tiles, mixes):
        x1 = x_ref[t, :] + _rms(mix, post_gain)
        x1_ref[t, :] = x1
        h2_ref[t, :] = (_rms(x1, pre_gain) + mod_ref[3:4, :]).astype(BF16)


def _mix_out(attn, o_f, o_b, proj, col_hg, hgn_g, w_out, x, mod3, post_g, pre_g, tm=512):
    bsz, s, d = x.shape
    na = attn.shape[-1]
    heads = o_f.shape[1]

    def row(width):
        return pl.BlockSpec((None, tm, width), lambda b, i: (b, i, 0))

    def per_head(first):
        return pl.BlockSpec((None, heads, tm, LANE), lambda b, i: (b, first // heads, i, 0))

    def const(shape):
        return pl.BlockSpec(shape, lambda b, i: (0,) * len(shape))

    assert col_hg % heads == 0
    return pl.pallas_call(
        _mix_out_kernel,
        grid=(bsz, s // tm),
        in_specs=[row(na), per_head(0), per_head(0), per_head(col_hg),
                  const((1, HGRN_DV)), const(w_out.shape), row(d),
                  pl.BlockSpec((None, N_MOD, d), lambda b, i: (b, 0, 0)),
                  const((1, d)), const((1, d))],
        out_specs=[row(d), row(d)],
        out_shape=[jax.ShapeDtypeStruct((bsz, s, d), F32),
                   jax.ShapeDtypeStruct((bsz, s, d), BF16)],
        compiler_params=_cparams("parallel", "parallel"),
        name="mix_out",
    )(attn, o_f, o_b, proj, hgn_g, w_out, x, mod3, post_g, pre_g)


def _mlp_kernel(h_ref, wu_ref, wd_ref, x1_ref, mod_ref, g_ref, o_ref):
    j = pl.program_id(2)
    last = pl.num_programs(2) - 1

    @pl.when(j == 0)
    def _():
        o_ref[...] = jnp.zeros_like(o_ref)

    def activation():
        u = jnp.maximum(_dot(h_ref[...], wu_ref[...]), 0.0)
        return (u * u).astype(BF16)

    @pl.when(j < last)
    def _():
        o_ref[...] += _dot(activation(), wd_ref[...])

    @pl.when(j == last)
    def _():
        a = activation()
        gain = mod_ref[5:6, :] * g_ref[...]
        for r in range(0, o_ref.shape[0], MLP_SUB_ROWS):
            t = slice(r, r + MLP_SUB_ROWS)
            y = o_ref[t, :] + _dot(a[t, :], wd_ref[...])
            o_ref[t, :] = x1_ref[t, :] + _rms(y, gain)


def _mlp(h2, w_up, w_down, x1, mod3, g, tm=512, tf=1024):
    bsz, s, d = x1.shape
    dff = w_up.shape[1]
    return pl.pallas_call(
        _mlp_kernel,
        grid=(bsz, s // tm, dff // tf),
        in_specs=[pl.BlockSpec((None, tm, d), lambda b, i, j: (b, i, 0)),
                  pl.BlockSpec((d, tf), lambda b, i, j: (0, j)),
                  pl.BlockSpec((tf, d), lambda b, i, j: (j, 0)),
                  pl.BlockSpec((None, tm, d), lambda b, i, j: (b, i, 0)),
                  pl.BlockSpec((None, N_MOD, d), lambda b, i, j: (b, 0, 0)),
                  pl.BlockSpec((1, d), lambda b, i, j: (0, 0))],
        out_specs=pl.BlockSpec((None, tm, d), lambda b, i, j: (b, i, 0)),
        out_shape=jax.ShapeDtypeStruct((bsz, s, d), F32),
        compiler_params=_cparams("parallel", "parallel", "arbitrary"),
        name="mlp",
    )(h2, w_up, w_down, x1, mod3, g)


def _pack_w_uq(w):
    r = w.shape[0]
    w = w.reshape(r, MLA_HEADS, MLA_NOPE + MLA_ROPE)
    pad = jnp.zeros((r, MLA_HEADS, MLA_QK_PAD - MLA_NOPE - MLA_ROPE), w.dtype)
    return jnp.concatenate([w, pad], axis=-1).reshape(r, MLA_HEADS * MLA_QK_PAD).astype(BF16)


def _pack_w_ukv(w):
    r = w.shape[0]
    w = w.reshape(r, MLA_HEADS, MLA_NOPE + MLA_V)
    wk = w[:, :, :MLA_NOPE].reshape(r, -1).astype(BF16)
    wvt = w[:, :, MLA_NOPE:].reshape(r, -1).T.astype(BF16)
    return wk, wvt


def _rope_inv_freq():
    half = MLA_ROPE // 2
    f = ROPE_BASE ** (-jnp.arange(half, dtype=F32) / half)
    return jnp.tile(f, ROPE_PACK).reshape(1, LANE)


def kernel(x, c, positions, w_mod, b_mod, pre_mix_g, post_mix_g, pre_mlp_g, post_mlp_g,
           w_in, q_norm_g, kv_norm_g, w_uq, w_ukv, hgrn_norm_g, hgrn_lb_logits_fwd,
           hgrn_lb_logits_bwd, w_out, w_up, w_down):
    depth = w_mod.shape[0]
    assert depth == 1, "lower-bound slot selection is written for a single layer"
    bsz, s, d = x.shape
    q_rank = q_norm_g.shape[-1]
    kv_rank = kv_norm_g.shape[-1]
    invf = _rope_inv_freq()
    for layer in range(depth):
        mod3 = _mod(c, w_mod[layer], b_mod[layer]).reshape(bsz, N_MOD, d)
        lat, proj = _in_proj(x, mod3, pre_mix_g[layer].reshape(1, d),
                             w_in[layer].T.astype(BF16), q_rank + kv_rank + MLA_ROPE)
        wk, wvt = _pack_w_ukv(w_ukv[layer])
        q, k, vt = _mla_prep(lat, positions, invf, q_norm_g[layer].reshape(1, -1),
                             kv_norm_g[layer].reshape(1, -1), _pack_w_uq(w_uq[layer]),
                             wk, wvt, q_rank, kv_rank)
        attn, (w_out_b, w_up_b, w_down_b) = _attention(
            q, k, vt, (w_out[layer], w_up[layer], w_down[layer]))
        o_f, o_b = _hgrn(proj, hgrn_lb_logits_fwd, hgrn_lb_logits_bwd,
                         col_q=0, col_ff=8, col_fb=16, col_v=24)
        x1, h2 = _mix_out(attn, o_f, o_b, proj, 32, hgrn_norm_g[layer].reshape(1, -1),
                          w_out_b, x, mod3,
                          post_mix_g[layer].reshape(1, d), pre_mlp_g[layer].reshape(1, d))
        x = _mlp(h2, w_up_b, w_down_b, x1, mod3, post_mlp_g[layer].reshape(1, d))
    return x
```

```python
import functools

import jax
import jax.numpy as jnp
import numpy as np
from jax import lax
from jax.experimental import pallas as pl
from jax.experimental.pallas import tpu as pltpu

F32 = jnp.float32
BF16 = jnp.bfloat16

NORM_EPS = 1e-6
ROPE_BASE = 10000.0
N_MOD = 6
LANE = 128
VMEM_LIMIT = 56 * 1024 * 1024

MLA_HEADS = 8
MLA_NOPE = 128
MLA_ROPE = 64
MLA_V = 128
MLA_QK_PAD = 256
ROPE_PACK = LANE // (MLA_ROPE // 2)
ATTN_TILES_PER_TRIP = 4
MLA_LAT_PAD = 1024
HGRN_HEADS = 8
HGRN_DK = 128
HGRN_DV = 128
HGRN_CHUNK = 64
HGRN_HEADS_PER_STEP = 2
HGRN_SAFE_LOG2_SPREAD = 96.0
GATE_SUB_ROWS = 256
MLP_SUB_ROWS = 128
MIX_SUB_ROWS = 128


def _cparams(*sem):
    return pltpu.CompilerParams(dimension_semantics=sem, vmem_limit_bytes=VMEM_LIMIT)


def _dot(a, b):
    return jnp.dot(a, b, preferred_element_type=F32)


def _dot_nt(a, b):
    return lax.dot_general(a, b, (((1,), (1,)), ((), ())), preferred_element_type=F32)


def _dot_tn(a, b):
    return lax.dot_general(a, b, (((0,), (0,)), ((), ())), preferred_element_type=F32)


def _rms(x, g):
    return x * lax.rsqrt(jnp.mean(x * x, axis=-1, keepdims=True) + NORM_EPS) * g


def _mod_kernel(c_ref, w_ref, b_ref, o_ref):
    c = c_ref[...]
    cond = c * jax.nn.sigmoid(c)
    hi = cond.astype(BF16)
    lo = (cond - hi.astype(F32)).astype(BF16)
    w = w_ref[...].astype(BF16)
    o_ref[...] = _dot(hi, w) + _dot(lo, w) + b_ref[...]


def _mod(c, w_mod, b_mod, tn=1024):
    bsz, d = c.shape
    n = w_mod.shape[1]
    return pl.pallas_call(
        _mod_kernel,
        grid=(n // tn,),
        in_specs=[pl.BlockSpec((bsz, d), lambda j: (0, 0)),
                  pl.BlockSpec((d, tn), lambda j: (0, j)),
                  pl.BlockSpec((1, tn), lambda j: (0, j))],
        out_specs=pl.BlockSpec((bsz, tn), lambda j: (0, j)),
        out_shape=jax.ShapeDtypeStruct((bsz, n), F32),
        compiler_params=_cparams("parallel"),
        name="mod",
    )(c, w_mod, b_mod.reshape(1, n))


def _in_proj_kernel(x_ref, mod_ref, g_ref, wa_ref, wb_ref, lb_ref, oa_ref, ob_ref, h_ref,
                    *, gate_tiles):
    j = pl.program_id(2)

    @pl.when(j == 0)
    def _():
        h = _rms(x_ref[...], g_ref[...]) * (1.0 + mod_ref[1:2, :]) + mod_ref[0:1, :]
        h = h.astype(BF16)
        h_ref[...] = h
        oa_ref[...] = _dot_nt(h, wa_ref[...]).astype(oa_ref.dtype)

    def store(y, rows=slice(None)):
        y = y.astype(ob_ref.dtype)
        for c in range(ob_ref.shape[0]):
            ob_ref[c, rows, :] = y[:, c * LANE:(c + 1) * LANE]

    is_gate = (j >= gate_tiles[0]) & (j < gate_tiles[1])

    @pl.when(jnp.logical_not(is_gate))
    def _():
        store(_dot_nt(h_ref[...], wb_ref[...]))

    @pl.when(is_gate)
    def _():
        l = lb_ref[...]
        e = jnp.exp(l - jnp.max(l, axis=0, keepdims=True))
        lb = e[0:1, :] / jnp.sum(e, axis=0, keepdims=True)
        for r in range(0, h_ref.shape[0], GATE_SUB_ROWS):
            rows = slice(r, r + GATE_SUB_ROWS)
            z = _dot_nt(h_ref[rows, :], wb_ref[...])
            store(jnp.log2(lb + (1.0 - lb) * jax.nn.sigmoid(z)), rows)


def _in_proj(x, mod3, g, wt, head, lb_logits, tm=1024, tn=1024):
    bsz, s, d = x.shape
    na, nb = MLA_LAT_PAD, wt.shape[0] - head
    row_align = int(np.gcd(head, tn))
    assert row_align % 16 == 0
    n_gate = lb_logits.shape[0]
    kern = functools.partial(_in_proj_kernel, gate_tiles=(1, 1 + n_gate))
    return pl.pallas_call(
        kern,
        grid=(bsz, s // tm, nb // tn),
        in_specs=[pl.BlockSpec((None, tm, d), lambda b, i, j: (b, i, 0)),
                  pl.BlockSpec((None, N_MOD, d), lambda b, i, j: (b, 0, 0)),
                  pl.BlockSpec((1, d), lambda b, i, j: (0, 0)),
                  pl.BlockSpec((pl.Element(na), pl.Element(d)), lambda b, i, j: (0, 0)),
                  pl.BlockSpec((pl.Element(tn), pl.Element(d)),
                               lambda b, i, j: (pl.multiple_of(head + j * tn, row_align), 0)),
                  pl.BlockSpec((None,) + lb_logits.shape[1:],
                               lambda b, i, j: (jnp.clip(j - 1, 0, n_gate - 1), 0, 0))],
        out_specs=[pl.BlockSpec((None, tm, na), lambda b, i, j: (b, i, 0)),
                   pl.BlockSpec((None, tn // LANE, tm, LANE), lambda b, i, j: (b, j, i, 0))],
        out_shape=[jax.ShapeDtypeStruct((bsz, s, na), BF16),
                   jax.ShapeDtypeStruct((bsz, nb // LANE, s, LANE), BF16)],
        scratch_shapes=[pltpu.VMEM((tm, d), BF16)],
        compiler_params=_cparams("parallel", "parallel", "arbitrary"),
        name="in_proj",
    )(x, mod3, g, wt, wt, lb_logits)


def _rope_tile(t, cos_m, sin_lo, sin_hi):
    half = MLA_ROPE // 2
    return (t * cos_m + pltpu.roll(t, LANE - half, 1) * sin_lo
            + pltpu.roll(t, half, 1) * sin_hi)


def _rope_tables(pos_ref, invf_ref, tab_ref):
    half = MLA_ROPE // 2
    rows = pos_ref.shape[0]
    pos = pos_ref[...].astype(F32)
    lane = lax.broadcasted_iota(jnp.int32, (rows, LANE), 1)
    spread = pos[:, ROPE_PACK - 1:ROPE_PACK]
    for g in range(ROPE_PACK - 2, -1, -1):
        spread = jnp.where(lane < (g + 1) * half, pos[:, g:g + 1], spread)
    ang = spread * invf_ref[...]
    cos, sin = jnp.cos(ang), jnp.sin(ang)
    for g in range(ROPE_PACK):
        c = pltpu.roll(cos, LANE - g * half, 1) if g else cos
        s = pltpu.roll(sin, LANE - g * half, 1) if g else sin
        in_lo = lane < half
        in_hi = (lane >= half) & (lane < MLA_ROPE)
        dst = pl.ds(g, rows, stride=ROPE_PACK)
        tab_ref[0, dst, :] = jnp.where(in_lo, c, jnp.where(in_hi, pltpu.roll(c, half, 1), 0.0))
        tab_ref[1, dst, :] = jnp.where(in_lo, -s, 0.0)
        tab_ref[2, dst, :] = jnp.where(in_hi, pltpu.roll(s, half, 1), 0.0)


def _mla_prep_kernel(p_ref, pos_ref, invf_ref, qg_ref, kvg_ref, wq_ref, wk_ref, wvt_ref,
                     q_ref, k_ref, vt_ref, tab_ref, *, q_rank, kv_rank, scale):
    _rope_tables(pos_ref, invf_ref, tab_ref)
    cos_m, sin_lo, sin_hi = tab_ref[0], tab_ref[1], tab_ref[2]

    q_lat = p_ref[:, 0:q_rank].astype(F32)
    q = _dot(_rms(q_lat, qg_ref[...]).astype(BF16), wq_ref[...])
    kv_lat = p_ref[:, q_rank:q_rank + kv_rank].astype(F32)
    kv_n = _rms(kv_lat, kvg_ref[...]).astype(BF16)
    k_nope = _dot(kv_n, wk_ref[...])
    vt_ref[...] = _dot_nt(wvt_ref[...], kv_n).astype(BF16)
    kr_off = q_rank + kv_rank
    k_rope = _rope_tile(p_ref[:, kr_off:kr_off + LANE].astype(F32),
                        cos_m, sin_lo, sin_hi).astype(BF16)
    for h in range(MLA_HEADS):
        o = h * MLA_QK_PAD
        q_ref[:, o:o + MLA_NOPE] = (q[:, o:o + MLA_NOPE] * scale).astype(BF16)
        qr = _rope_tile(q[:, o + MLA_NOPE:o + MLA_QK_PAD], cos_m, sin_lo, sin_hi)
        q_ref[:, o + MLA_NOPE:o + MLA_QK_PAD] = (qr * scale).astype(BF16)
        k_ref[:, o:o + MLA_NOPE] = k_nope[:, h * MLA_NOPE:(h + 1) * MLA_NOPE].astype(BF16)
        k_ref[:, o + MLA_NOPE:o + MLA_QK_PAD] = k_rope


def _mla_prep(proj, positions, invf, qg, kvg, wq, wk, wvt, q_rank, kv_rank, tm=512):
    bsz, s, blk0 = proj.shape
    hq = MLA_HEADS * MLA_QK_PAD
    hv = MLA_HEADS * MLA_V
    scale = float((MLA_NOPE + MLA_ROPE) ** -0.5 * np.log2(np.e))
    kern = functools.partial(_mla_prep_kernel, q_rank=q_rank, kv_rank=kv_rank, scale=scale)

    def const(a):
        return pl.BlockSpec(a.shape, lambda b, i: (0, 0))

    return pl.pallas_call(
        kern,
        grid=(bsz, s // tm),
        in_specs=[pl.BlockSpec((None, tm, blk0), lambda b, i: (b, i, 0)),
                  pl.BlockSpec((None, tm // ROPE_PACK, ROPE_PACK), lambda b, i: (b, i, 0)),
                  const(invf), const(qg), const(kvg), const(wq), const(wk), const(wvt)],
        out_specs=[pl.BlockSpec((None, tm, hq), lambda b, i: (b, i, 0)),
                   pl.BlockSpec((None, tm, hq), lambda b, i: (b, i, 0)),
                   pl.BlockSpec((None, hv, tm), lambda b, i: (b, 0, i))],
        out_shape=[jax.ShapeDtypeStruct((bsz, s, hq), BF16),
                   jax.ShapeDtypeStruct((bsz, s, hq), BF16),
                   jax.ShapeDtypeStruct((bsz, hv, s), BF16)],
        scratch_shapes=[pltpu.VMEM((3, tm, LANE), F32)],
        compiler_params=_cparams("parallel", "parallel"),
        name="mla_prep",
    )(proj, positions.reshape(bsz, s // ROPE_PACK, ROPE_PACK), invf, qg, kvg, wq, wk, wvt)


def _attn_kernel(*refs, tq, tk, n_cast):
    q_ref, k_ref, vt_ref = refs[:3]
    o_ref = refs[3 + n_cast]
    sa_ref, sb_ref = refs[-2:]
    for src, dst in zip(refs[3:3 + n_cast], refs[4 + n_cast:4 + 2 * n_cast]):
        dst[...] = src[...].astype(dst.dtype)
    s_len = k_ref.shape[0]
    n_kv = s_len // tk
    n_q = s_len // tq
    sub = 8

    def scores(i, s_ref):
        q = q_ref[pl.ds(pl.multiple_of(i * tq, tq), tq), :]
        m8 = jnp.full((sub, tq), -jnp.inf, F32)
        for j in range(n_kv):
            st = _dot_nt(k_ref[j * tk:(j + 1) * tk, :], q)
            s_ref[j * tk:(j + 1) * tk, :] = st
            m8 = jnp.maximum(m8, jnp.max(st.reshape(tk // sub, sub, tq), axis=0))
            yield
        return jnp.broadcast_to(jnp.max(m8, axis=0, keepdims=True), (sub, tq))

    def values(i, s_ref, m8):
        l8 = jnp.zeros((sub, tq), F32)
        acc = jnp.zeros((MLA_V, tq), F32)
        for j in range(n_kv):
            st = s_ref[j * tk:(j + 1) * tk, :].reshape(tk // sub, sub, tq)
            p = jnp.exp2(st - m8[None])
            l8 = l8 + jnp.sum(p, axis=0)
            acc = acc + _dot(vt_ref[:, j * tk:(j + 1) * tk],
                             p.reshape(tk, tq).astype(BF16))
            yield
        l = jnp.sum(l8, axis=0, keepdims=True)
        o_ref[pl.ds(pl.multiple_of(i * tq, tq), tq), :] = (acc / l).T.astype(o_ref.dtype)

    def run(*gens):
        result = None
        live = list(gens)
        while live:
            for g in list(live):
                try:
                    next(g)
                except StopIteration as stop:
                    if g is gens[0]:
                        result = stop.value
                    live.remove(g)
        return result

    bufs = (sa_ref, sb_ref)
    group = ATTN_TILES_PER_TRIP

    def tiles(first, count, m):
        for u in range(count):
            m = run(scores(first + u + 1, bufs[(u + 1) % 2]),
                    values(first + u, bufs[u % 2], m))
        return m

    m = lax.fori_loop(0, n_q // group - 1, lambda k, m: tiles(group * k, group, m),
                      run(scores(0, bufs[0])))
    m = tiles(n_q - group, group - 1, m)
    run(values(n_q - 1, bufs[(group - 1) % 2], m))


def _attention(q, k, vt, f32_weights, tq=256, tk=512):
    bsz, s, _ = q.shape
    steps = bsz * MLA_HEADS
    kern = functools.partial(_attn_kernel, tq=tq, tk=tk, n_cast=len(f32_weights))

    def slab(w):
        return pl.BlockSpec((w.shape[0] // steps, w.shape[1]),
                            lambda b, h: (b * MLA_HEADS + h, 0))

    assert all(w.shape[0] % (8 * steps) == 0 for w in f32_weights)
    out = pl.pallas_call(
        kern,
        grid=(bsz, MLA_HEADS),
        in_specs=[pl.BlockSpec((None, s, MLA_QK_PAD), lambda b, h: (b, 0, h)),
                  pl.BlockSpec((None, s, MLA_QK_PAD), lambda b, h: (b, 0, h)),
                  pl.BlockSpec((None, MLA_V, s), lambda b, h: (b, h, 0))]
                 + [slab(w) for w in f32_weights],
        out_specs=[pl.BlockSpec((None, s, MLA_V), lambda b, h: (b, 0, h))]
                  + [slab(w) for w in f32_weights],
        out_shape=[jax.ShapeDtypeStruct((bsz, s, MLA_HEADS * MLA_V), BF16)]
                  + [jax.ShapeDtypeStruct(w.shape, BF16) for w in f32_weights],
        scratch_shapes=[pltpu.VMEM((s, tq), F32), pltpu.VMEM((s, tq), F32)],
        compiler_params=_cparams("parallel", "parallel"),
        name="attention",
    )(q, k, vt, *f32_weights)
    return out[0], out[1:]


def _run_staggered(gens):
    live = list(enumerate(gens))
    tick = 0
    while live:
        for delay, g in list(live):
            if tick >= delay and next(g, StopIteration) is StopIteration:
                live.remove((delay, g))
        tick += 1


def _hgrn_exact_block(q, k, b, v_ref, qs, states, o_ref, k_s, b_s, v_s, *, forward):
    n, c, _ = q.shape
    k_s[...] = k.reshape(n * c, HGRN_DK)
    b_s[...] = b.reshape(n * c, HGRN_DK)
    v_s[...] = v_ref[...].astype(F32)
    t_idx = lax.broadcasted_iota(jnp.int32, (c, 1), 0)
    for j in range(n):
        def key_row(s, acc, j=j):
            row = pl.ds(j * c + s, 1)
            w = jnp.exp2(jnp.minimum(b[j] - b_s[row, :], 0.0))
            seen = (t_idx >= s) if forward else (t_idx <= s)
            a_col = jnp.sum(jnp.where(seen, q[j] * w * k_s[row, :], 0.0),
                            axis=1, keepdims=True)
            return acc + a_col * v_s[row, :]

        intra = lax.fori_loop(0, c, key_row, jnp.zeros((c, HGRN_DV), F32))
        inter = _dot_nt(qs[j], states[j])
        o_ref[j * c:(j + 1) * c, :] = (inter + intra).astype(o_ref.dtype)


def _hgrn_direction(q_ref, lf_ref, v_ref, st_ref, o_ref, exact_scratch, *, forward, lag):
    c = HGRN_CHUNK
    n = q_ref.shape[0] // c
    row = lax.broadcasted_iota(jnp.int32, (c, c), 0)
    col = lax.broadcasted_iota(jnp.int32, (c, c), 1)
    causal = (col <= row) if forward else (col >= row)
    tri = causal.astype(BF16)
    edge = c - 1 if forward else 0

    chunks = [slice(j * c, (j + 1) * c) for j in range(n)]
    log_f = lf_ref[...]
    k = (1.0 - jnp.exp2(log_f.astype(F32))).reshape(n, c, HGRN_DK)
    q = q_ref[...].reshape(n, c, HGRN_DK)
    vt = v_ref[...].astype(F32).T.astype(BF16)
    yield
    b = jnp.concatenate([_dot(tri, log_f[r]) for r in chunks], axis=0)
    yield
    b = b.reshape(n, c, HGRN_DK)
    b_edge = b[:, edge:edge + 1]
    kb = k.astype(BF16)
    qs = q * jnp.exp2(b).astype(BF16)
    ki = kb * jnp.exp2(-b).astype(BF16)
    kd = kb * jnp.exp2(b_edge - b).astype(BF16)
    decay = jnp.exp2(b_edge)
    spread = jnp.max(-b_edge)
    yield
    a = [_dot_nt(qs[j], ki[j]) for j in range(n)]
    upd = [_dot(vt[:, chunks[j]], kd[j]) for j in range(n)]
    yield
    a = [jnp.where(causal, x, 0.0).astype(BF16) for x in a]
    st = st_ref[...]
    states = [None] * n
    for j in (range(n) if forward else reversed(range(n))):
        states[j] = st.astype(BF16)
        st = st * decay[j] + upd[j]
    st_ref[...] = st
    yield
    outs = [_dot_nt(jnp.concatenate([a[j], qs[j]], axis=1),
                    jnp.concatenate([vt[:, chunks[j]], states[j]], axis=1))
            for j in range(n)]
    o_ref[...] = jnp.concatenate(outs, axis=0).astype(o_ref.dtype)
    yield
    for _ in range(lag):
        yield

    @pl.when(spread > HGRN_SAFE_LOG2_SPREAD)
    def _():
        _hgrn_exact_block(q.astype(F32), k, b, v_ref, qs, states, o_ref, *exact_scratch,
                          forward=forward)
    yield


def _hgrn_kernel(qf_ref, lff_ref, vf_ref, qb_ref, lfb_ref, vb_ref,
                 of_ref, ob_ref, stf_ref, stb_ref, k_s, b_s, v_s):
    @pl.when(pl.program_id(2) == 0)
    def _():
        stf_ref[...] = jnp.zeros_like(stf_ref)
        stb_ref[...] = jnp.zeros_like(stb_ref)

    scans = []
    lag = 2 * qf_ref.shape[0] - 1
    for h in range(qf_ref.shape[0]):
        scans.append(_hgrn_direction(qf_ref.at[h], lff_ref.at[h], vf_ref.at[h],
                                     stf_ref.at[h], of_ref.at[h],
                                     (k_s, b_s, v_s), forward=True, lag=lag))
        scans.append(_hgrn_direction(qb_ref.at[h], lfb_ref.at[h], vb_ref.at[h],
                                     stb_ref.at[h], ob_ref.at[h],
                                     (k_s, b_s, v_s), forward=False, lag=lag))
    _run_staggered(scans)


def _hgrn(proj, col_q, col_ff, col_fb, col_v, tb=1024):
    bsz, _, s, _ = proj.shape
    nb = s // tb
    hps = HGRN_HEADS_PER_STEP

    def fwd(col):
        return pl.BlockSpec((None, hps, tb, LANE), lambda b, h, i: (b, col // hps + h, i, 0))

    def bwd(col):
        return pl.BlockSpec((None, hps, tb, LANE),
                            lambda b, h, i: (b, col // hps + h, nb - 1 - i, 0))

    assert all(col % hps == 0 for col in (col_q, col_ff, col_fb, col_v))
    out_shape = jax.ShapeDtypeStruct((bsz, HGRN_HEADS, s, HGRN_DV), BF16)
    return pl.pallas_call(
        _hgrn_kernel,
        grid=(bsz, HGRN_HEADS // hps, nb),
        in_specs=[fwd(col_q), fwd(col_ff), fwd(col_v),
                  bwd(col_q), bwd(col_fb), bwd(col_v)],
        out_specs=[fwd(0), bwd(0)],
        out_shape=[out_shape, out_shape],
        scratch_shapes=[pltpu.VMEM((hps, HGRN_DV, HGRN_DK), F32),
                        pltpu.VMEM((hps, HGRN_DV, HGRN_DK), F32),
                        pltpu.VMEM((tb, HGRN_DK), F32), pltpu.VMEM((tb, HGRN_DK), F32),
                        pltpu.VMEM((tb, HGRN_DV), F32)],
        compiler_params=_cparams("parallel", "parallel", "arbitrary"),
        name="hgrn",
    )(proj, proj, proj, proj, proj, proj)


def _mix_out_kernel(attn_ref, of_ref, ob_ref, hg_ref, hgn_ref, w_ref, x_ref, mod_ref,
                    postg_ref, preg_ref, x1_ref, h2_ref):
    na = attn_ref.shape[-1]
    tm = attn_ref.shape[0]
    tiles = [slice(r, r + MIX_SUB_ROWS) for r in range(0, tm, MIX_SUB_ROWS)]
    recs = []
    for t in tiles:
        rec = []
        for h in range(HGRN_HEADS):
            o = of_ref[h, t, :].astype(F32) + ob_ref[h, t, :].astype(F32)
            hg = hg_ref[h, t, :].astype(F32)
            rec.append((_rms(o, hgn_ref[...]) * (hg * jax.nn.sigmoid(hg))).astype(BF16))
        recs.append(jnp.concatenate(rec, axis=-1))
    mixes = [_dot(attn_ref[t, :], w_ref[0:na, :]) + _dot(rec, w_ref[na:, :])
             for t, rec in zip(tiles, recs)]
    post_gain = mod_ref[2:3, :] * postg_ref[...]
    pre_gain = preg_ref[...] * (1.0 + mod_ref[4:5, :])
    for t, mix in zip(tiles, mixes):
        x1 = x_ref[t, :] + _rms(mix, post_gain)
        x1_ref[t, :] = x1
        h2_ref[t, :] = (_rms(x1, pre_gain) + mod_ref[3:4, :]).astype(BF16)


def _mix_out(attn, o_f, o_b, proj, col_hg, hgn_g, w_out, x, mod3, post_g, pre_g, tm=512):
    bsz, s, d = x.shape
    na = attn.shape[-1]
    heads = o_f.shape[1]

    def row(width):
        return pl.BlockSpec((None, tm, width), lambda b, i: (b, i, 0))

    def per_head(first):
        return pl.BlockSpec((None, heads, tm, LANE), lambda b, i: (b, first // heads, i, 0))

    def const(shape):
        return pl.BlockSpec(shape, lambda b, i: (0,) * len(shape))

    assert col_hg % heads == 0
    return pl.pallas_call(
        _mix_out_kernel,
        grid=(bsz, s // tm),
        in_specs=[row(na), per_head(0), per_head(0), per_head(col_hg),
                  const((1, HGRN_DV)), const(w_out.shape), row(d),
                  pl.BlockSpec((None, N_MOD, d), lambda b, i: (b, 0, 0)),
                  const((1, d)), const((1, d))],
        out_specs=[row(d), row(d)],
        out_shape=[jax.ShapeDtypeStruct((bsz, s, d), F32),
                   jax.ShapeDtypeStruct((bsz, s, d), BF16)],
        compiler_params=_cparams("parallel", "parallel"),
        name="mix_out",
    )(attn, o_f, o_b, proj, hgn_g, w_out, x, mod3, post_g, pre_g)


def _mlp_kernel(h_ref, wu_ref, wd_ref, x1_ref, mod_ref, g_ref, o_ref):
    j = pl.program_id(2)
    last = pl.num_programs(2) - 1

    @pl.when(j == 0)
    def _():
        o_ref[...] = jnp.zeros_like(o_ref)

    def activation():
        u = jnp.maximum(_dot(h_ref[...], wu_ref[...]), 0.0)
        return (u * u).astype(BF16)

    @pl.when(j < last)
    def _():
        o_ref[...] += _dot(activation(), wd_ref[...])

    @pl.when(j == last)
    def _():
        a = activation()
        gain = mod_ref[5:6, :] * g_ref[...]
        for r in range(0, o_ref.shape[0], MLP_SUB_ROWS):
            t = slice(r, r + MLP_SUB_ROWS)
            y = o_ref[t, :] + _dot(a[t, :], wd_ref[...])
            o_ref[t, :] = x1_ref[t, :] + _rms(y, gain)


def _mlp(h2, w_up, w_down, x1, mod3, g, tm=512, tf=1024):
    bsz, s, d = x1.shape
    dff = w_up.shape[1]
    return pl.pallas_call(
        _mlp_kernel,
        grid=(bsz, s // tm, dff // tf),
        in_specs=[pl.BlockSpec((None, tm, d), lambda b, i, j: (b, i, 0)),
                  pl.BlockSpec((d, tf), lambda b, i, j: (0, j)),
                  pl.BlockSpec((tf, d), lambda b, i, j: (j, 0)),
                  pl.BlockSpec((None, tm, d), lambda b, i, j: (b, i, 0)),
                  pl.BlockSpec((None, N_MOD, d), lambda b, i, j: (b, 0, 0)),
                  pl.BlockSpec((1, d), lambda b, i, j: (0, 0))],
        out_specs=pl.BlockSpec((None, tm, d), lambda b, i, j: (b, i, 0)),
        out_shape=jax.ShapeDtypeStruct((bsz, s, d), F32),
        compiler_params=_cparams("parallel", "parallel", "arbitrary"),
        name="mlp",
    )(h2, w_up, w_down, x1, mod3, g)


def _pack_w_uq(w):
    r = w.shape[0]
    w = w.reshape(r, MLA_HEADS, MLA_NOPE + MLA_ROPE)
    pad = jnp.zeros((r, MLA_HEADS, MLA_QK_PAD - MLA_NOPE - MLA_ROPE), w.dtype)
    return jnp.concatenate([w, pad], axis=-1).reshape(r, MLA_HEADS * MLA_QK_PAD).astype(BF16)


def _pack_w_ukv(w):
    r = w.shape[0]
    w = w.reshape(r, MLA_HEADS, MLA_NOPE + MLA_V)
    wk = w[:, :, :MLA_NOPE].reshape(r, -1).astype(BF16)
    wvt = w[:, :, MLA_NOPE:].reshape(r, -1).T.astype(BF16)
    return wk, wvt


def _rope_inv_freq():
    half = MLA_ROPE // 2
    f = ROPE_BASE ** (-jnp.arange(half, dtype=F32) / half)
    return jnp.tile(f, ROPE_PACK).reshape(1, LANE)


def kernel(x, c, positions, w_mod, b_mod, pre_mix_g, post_mix_g, pre_mlp_g, post_mlp_g,
           w_in, q_norm_g, kv_norm_g, w_uq, w_ukv, hgrn_norm_g, hgrn_lb_logits_fwd,
           hgrn_lb_logits_bwd, w_out, w_up, w_down):
    depth = w_mod.shape[0]
    assert depth == 1, "lower-bound slot selection is written for a single layer"
    bsz, s, d = x.shape
    q_rank = q_norm_g.shape[-1]
    kv_rank = kv_norm_g.shape[-1]
    invf = _rope_inv_freq()
    for layer in range(depth):
        mod3 = _mod(c, w_mod[layer], b_mod[layer]).reshape(bsz, N_MOD, d)
        lat, proj = _in_proj(x, mod3, pre_mix_g[layer].reshape(1, d),
                             w_in[layer].T.astype(BF16), q_rank + kv_rank + MLA_ROPE,
                             jnp.stack([hgrn_lb_logits_fwd, hgrn_lb_logits_bwd]))
        wk, wvt = _pack_w_ukv(w_ukv[layer])
        q, k, vt = _mla_prep(lat, positions, invf, q_norm_g[layer].reshape(1, -1),
                             kv_norm_g[layer].reshape(1, -1), _pack_w_uq(w_uq[layer]),
                             wk, wvt, q_rank, kv_rank)
        attn, (w_out_b, w_up_b, w_down_b) = _attention(
            q, k, vt, (w_out[layer], w_up[layer], w_down[layer]))
        o_f, o_b = _hgrn(proj, col_q=0, col_ff=8, col_fb=16, col_v=24)
        x1, h2 = _mix_out(attn, o_f, o_b, proj, 32, hgrn_norm_g[layer].reshape(1, -1),
                          w_out_b, x, mod3,
                          post_mix_g[layer].reshape(1, d), pre_mlp_g[layer].reshape(1, d))
        x = _mlp(h2, w_up_b, w_down_b, x1, mod3, post_mlp_g[layer].reshape(1, d))
    return x
```

```python
import functools

import jax
import jax.numpy as jnp
import numpy as np
from jax import lax
from jax.experimental import pallas as pl
from jax.experimental.pallas import tpu as pltpu

F32 = jnp.float32
BF16 = jnp.bfloat16

NORM_EPS = 1e-6
ROPE_BASE = 10000.0
N_MOD = 6
LANE = 128
VMEM_LIMIT = 56 * 1024 * 1024

MLA_HEADS = 8
MLA_NOPE = 128
MLA_ROPE = 64
MLA_V = 128
MLA_QK_PAD = 256
ROPE_PACK = LANE // (MLA_ROPE // 2)
ATTN_TILES_PER_TRIP = 4
MLA_LAT_PAD = 1024
HGRN_HEADS = 8
HGRN_DK = 128
HGRN_DV = 128
HGRN_CHUNK = 64
HGRN_HEADS_PER_STEP = 2
HGRN_SAFE_LOG2_SPREAD = 96.0
NORM_SUB_ROWS = 256
MLP_SUB_ROWS = 256
MIX_SUB_ROWS = 128


def _cparams(*sem):
    return pltpu.CompilerParams(dimension_semantics=sem, vmem_limit_bytes=VMEM_LIMIT)


def _dot(a, b):
    return jnp.dot(a, b, preferred_element_type=F32)


def _dot_nt(a, b):
    return lax.dot_general(a, b, (((1,), (1,)), ((), ())), preferred_element_type=F32)


def _rms(x, g):
    return x * lax.rsqrt(jnp.mean(x * x, axis=-1, keepdims=True) + NORM_EPS) * g


def _mod_kernel(c_ref, w_ref, b_ref, o_ref):
    c = c_ref[...]
    cond = c * jax.nn.sigmoid(c)
    hi = cond.astype(BF16)
    lo = (cond - hi.astype(F32)).astype(BF16)
    w = w_ref[...].astype(BF16)
    o_ref[...] = _dot(hi, w) + _dot(lo, w) + b_ref[...]


def _mod(c, w_mod, b_mod, tn=1024):
    bsz, d = c.shape
    n = w_mod.shape[1]
    return pl.pallas_call(
        _mod_kernel,
        grid=(n // tn,),
        in_specs=[pl.BlockSpec((bsz, d), lambda j: (0, 0)),
                  pl.BlockSpec((d, tn), lambda j: (0, j)),
                  pl.BlockSpec((1, tn), lambda j: (0, j))],
        out_specs=pl.BlockSpec((bsz, tn), lambda j: (0, j)),
        out_shape=jax.ShapeDtypeStruct((bsz, n), F32),
        compiler_params=_cparams("parallel"),
        name="mod",
    )(c, w_mod, b_mod.reshape(1, n))


def _in_proj_kernel(x_ref, mod_ref, g_ref, wa_ref, wb_ref, oa_ref, ob_ref, h_ref):
    @pl.when(pl.program_id(2) == 0)
    def _():
        gain = g_ref[...] * (1.0 + mod_ref[1:2, :])
        for r in range(0, x_ref.shape[0], NORM_SUB_ROWS):
            t = slice(r, r + NORM_SUB_ROWS)
            h = (_rms(x_ref[t, :], gain) + mod_ref[0:1, :]).astype(BF16)
            h_ref[t, :] = h
            oa_ref[t, :] = _dot_nt(h, wa_ref[...]).astype(oa_ref.dtype)

    y = _dot_nt(h_ref[...], wb_ref[...]).astype(ob_ref.dtype)
    for c in range(ob_ref.shape[0]):
        ob_ref[c] = y[:, c * LANE:(c + 1) * LANE]


def _in_proj(x, mod3, g, wt, head, tm=1024, tn=1280):
    bsz, s, d = x.shape
    na, nb = MLA_LAT_PAD, wt.shape[0] - head
    row_align = int(np.gcd(head, tn))
    assert row_align % 16 == 0
    return pl.pallas_call(
        _in_proj_kernel,
        grid=(bsz, s // tm, nb // tn),
        in_specs=[pl.BlockSpec((None, tm, d), lambda b, i, j: (b, i, 0)),
                  pl.BlockSpec((None, N_MOD, d), lambda b, i, j: (b, 0, 0)),
                  pl.BlockSpec((1, d), lambda b, i, j: (0, 0)),
                  pl.BlockSpec((pl.Element(na), pl.Element(d)), lambda b, i, j: (0, 0)),
                  pl.BlockSpec((pl.Element(tn), pl.Element(d)),
                               lambda b, i, j: (pl.multiple_of(head + j * tn, row_align), 0))],
        out_specs=[pl.BlockSpec((None, tm, na), lambda b, i, j: (b, i, 0)),
                   pl.BlockSpec((None, tn // LANE, tm, LANE), lambda b, i, j: (b, j, i, 0))],
        out_shape=[jax.ShapeDtypeStruct((bsz, s, na), BF16),
                   jax.ShapeDtypeStruct((bsz, nb // LANE, s, LANE), BF16)],
        scratch_shapes=[pltpu.VMEM((tm, d), BF16)],
        compiler_params=_cparams("parallel", "parallel", "arbitrary"),
        name="in_proj",
    )(x, mod3, g, wt, wt)


def _rope_tile(t, cos_m, sin_lo, sin_hi):
    half = MLA_ROPE // 2
    return (t * cos_m + pltpu.roll(t, LANE - half, 1) * sin_lo
            + pltpu.roll(t, half, 1) * sin_hi)


def _rope_tables(pos_ref, invf_ref, tab_ref):
    half = MLA_ROPE // 2
    rows = pos_ref.shape[0]
    pos = pos_ref[...].astype(F32)
    lane = lax.broadcasted_iota(jnp.int32, (rows, LANE), 1)
    spread = pos[:, ROPE_PACK - 1:ROPE_PACK]
    for g in range(ROPE_PACK - 2, -1, -1):
        spread = jnp.where(lane < (g + 1) * half, pos[:, g:g + 1], spread)
    ang = spread * invf_ref[...]
    cos, sin = jnp.cos(ang), jnp.sin(ang)
    for g in range(ROPE_PACK):
        c = pltpu.roll(cos, LANE - g * half, 1) if g else cos
        s = pltpu.roll(sin, LANE - g * half, 1) if g else sin
        in_lo = lane < half
        in_hi = (lane >= half) & (lane < MLA_ROPE)
        dst = pl.ds(g, rows, stride=ROPE_PACK)
        tab_ref[0, dst, :] = jnp.where(in_lo, c, jnp.where(in_hi, pltpu.roll(c, half, 1), 0.0))
        tab_ref[1, dst, :] = jnp.where(in_lo, -s, 0.0)
        tab_ref[2, dst, :] = jnp.where(in_hi, pltpu.roll(s, half, 1), 0.0)


def _mla_prep_kernel(p_ref, pos_ref, invf_ref, qg_ref, kvg_ref, wq_ref, wk_ref, wvt_ref,
                     q_ref, k_ref, vt_ref, tab_ref, *, q_rank, kv_rank, scale):
    _rope_tables(pos_ref, invf_ref, tab_ref)
    cos_m, sin_lo, sin_hi = tab_ref[0], tab_ref[1], tab_ref[2]

    q_lat = p_ref[:, 0:q_rank].astype(F32)
    q = _dot(_rms(q_lat, qg_ref[...]).astype(BF16), wq_ref[...])
    kv_lat = p_ref[:, q_rank:q_rank + kv_rank].astype(F32)
    kv_n = _rms(kv_lat, kvg_ref[...]).astype(BF16)
    k_nope = _dot(kv_n, wk_ref[...])
    vt_ref[...] = _dot_nt(wvt_ref[...], kv_n).astype(BF16)
    kr_off = q_rank + kv_rank
    k_rope = _rope_tile(p_ref[:, kr_off:kr_off + LANE].astype(F32),
                        cos_m, sin_lo, sin_hi).astype(BF16)
    for h in range(MLA_HEADS):
        o = h * MLA_QK_PAD
        q_ref[:, o:o + MLA_NOPE] = (q[:, o:o + MLA_NOPE] * scale).astype(BF16)
        qr = _rope_tile(q[:, o + MLA_NOPE:o + MLA_QK_PAD], cos_m, sin_lo, sin_hi)
        q_ref[:, o + MLA_NOPE:o + MLA_QK_PAD] = (qr * scale).astype(BF16)
        k_ref[:, o:o + MLA_NOPE] = k_nope[:, h * MLA_NOPE:(h + 1) * MLA_NOPE].astype(BF16)
        k_ref[:, o + MLA_NOPE:o + MLA_QK_PAD] = k_rope


def _mla_prep(proj, positions, invf, qg, kvg, wq, wk, wvt, q_rank, kv_rank, tm=512):
    bsz, s, blk0 = proj.shape
    hq = MLA_HEADS * MLA_QK_PAD
    hv = MLA_HEADS * MLA_V
    scale = float((MLA_NOPE + MLA_ROPE) ** -0.5 * np.log2(np.e))
    kern = functools.partial(_mla_prep_kernel, q_rank=q_rank, kv_rank=kv_rank, scale=scale)

    def const(a):
        return pl.BlockSpec(a.shape, lambda b, i: (0, 0))

    return pl.pallas_call(
        kern,
        grid=(bsz, s // tm),
        in_specs=[pl.BlockSpec((None, tm, blk0), lambda b, i: (b, i, 0)),
                  pl.BlockSpec((None, tm // ROPE_PACK, ROPE_PACK), lambda b, i: (b, i, 0)),
                  const(invf), const(qg), const(kvg), const(wq), const(wk), const(wvt)],
        out_specs=[pl.BlockSpec((None, tm, hq), lambda b, i: (b, i, 0)),
                   pl.BlockSpec((None, tm, hq), lambda b, i: (b, i, 0)),
                   pl.BlockSpec((None, hv, tm), lambda b, i: (b, 0, i))],
        out_shape=[jax.ShapeDtypeStruct((bsz, s, hq), BF16),
                   jax.ShapeDtypeStruct((bsz, s, hq), BF16),
                   jax.ShapeDtypeStruct((bsz, hv, s), BF16)],
        scratch_shapes=[pltpu.VMEM((3, tm, LANE), F32)],
        compiler_params=_cparams("parallel", "parallel"),
        name="mla_prep",
    )(proj, positions.reshape(bsz, s // ROPE_PACK, ROPE_PACK), invf, qg, kvg, wq, wk, wvt)


def _attn_kernel(*refs, tq, tk, n_cast):
    q_ref, k_ref, vt_ref = refs[:3]
    o_ref = refs[3 + n_cast]
    sa_ref, sb_ref = refs[-2:]
    for src, dst in zip(refs[3:3 + n_cast], refs[4 + n_cast:4 + 2 * n_cast]):
        dst[...] = src[...].astype(dst.dtype)
    s_len = k_ref.shape[0]
    n_kv = s_len // tk
    n_q = s_len // tq
    sub = 8

    def scores(i, s_ref):
        q = q_ref[pl.ds(pl.multiple_of(i * tq, tq), tq), :]
        m8 = jnp.full((sub, tq), -jnp.inf, F32)
        for j in range(n_kv):
            st = _dot_nt(k_ref[j * tk:(j + 1) * tk, :], q)
            s_ref[j * tk:(j + 1) * tk, :] = st
            m8 = jnp.maximum(m8, jnp.max(st.reshape(tk // sub, sub, tq), axis=0))
            yield
        return jnp.broadcast_to(jnp.max(m8, axis=0, keepdims=True), (sub, tq))

    def values(i, s_ref, m8):
        l8 = jnp.zeros((sub, tq), F32)
        acc = jnp.zeros((MLA_V, tq), F32)
        for j in range(n_kv):
            st = s_ref[j * tk:(j + 1) * tk, :].reshape(tk // sub, sub, tq)
            p = jnp.exp2(st - m8[None])
            l8 = l8 + jnp.sum(p, axis=0)
            acc = acc + _dot(vt_ref[:, j * tk:(j + 1) * tk],
                             p.reshape(tk, tq).astype(BF16))
            yield
        l = jnp.sum(l8, axis=0, keepdims=True)
        o_ref[pl.ds(pl.multiple_of(i * tq, tq), tq), :] = (acc / l).T.astype(o_ref.dtype)

    def run(*gens):
        result = None
        live = list(gens)
        while live:
            for g in list(live):
                try:
                    next(g)
                except StopIteration as stop:
                    if g is gens[0]:
                        result = stop.value
                    live.remove(g)
        return result

    bufs = (sa_ref, sb_ref)
    group = ATTN_TILES_PER_TRIP

    def tiles(first, count, m):
        for u in range(count):
            m = run(scores(first + u + 1, bufs[(u + 1) % 2]),
                    values(first + u, bufs[u % 2], m))
        return m

    m = lax.fori_loop(0, n_q // group - 1, lambda k, m: tiles(group * k, group, m),
                      run(scores(0, bufs[0])))
    m = tiles(n_q - group, group - 1, m)
    run(values(n_q - 1, bufs[(group - 1) % 2], m))


def _attention(q, k, vt, f32_weights, tq=256, tk=1024):
    bsz, s, _ = q.shape
    steps = bsz * MLA_HEADS
    kern = functools.partial(_attn_kernel, tq=tq, tk=tk, n_cast=len(f32_weights))

    def slab(w):
        return pl.BlockSpec((w.shape[0] // steps, w.shape[1]),
                            lambda b, h: (b * MLA_HEADS + h, 0))

    assert all(w.shape[0] % (8 * steps) == 0 for w in f32_weights)
    out = pl.pallas_call(
        kern,
        grid=(bsz, MLA_HEADS),
        in_specs=[pl.BlockSpec((None, s, MLA_QK_PAD), lambda b, h: (b, 0, h)),
                  pl.BlockSpec((None, s, MLA_QK_PAD), lambda b, h: (b, 0, h)),
                  pl.BlockSpec((None, MLA_V, s), lambda b, h: (b, h, 0))]
                 + [slab(w) for w in f32_weights],
        out_specs=[pl.BlockSpec((None, s, MLA_V), lambda b, h: (b, 0, h))]
                  + [slab(w) for w in f32_weights],
        out_shape=[jax.ShapeDtypeStruct((bsz, s, MLA_HEADS * MLA_V), BF16)]
                  + [jax.ShapeDtypeStruct(w.shape, BF16) for w in f32_weights],
        scratch_shapes=[pltpu.VMEM((s, tq), F32), pltpu.VMEM((s, tq), F32)],
        compiler_params=_cparams("parallel", "parallel"),
        name="attention",
    )(q, k, vt, *f32_weights)
    return out[0], out[1:]


def _run_staggered(gens):
    live = list(enumerate(gens))
    tick = 0
    while live:
        for delay, g in list(live):
            if tick >= delay and next(g, StopIteration) is StopIteration:
                live.remove((delay, g))
        tick += 1


def _hgrn_exact_block(q, k, b, v_ref, qs, states, o_ref, k_s, b_s, v_s, *, forward):
    n, c, _ = q.shape
    k_s[...] = k.reshape(n * c, HGRN_DK)
    b_s[...] = b.reshape(n * c, HGRN_DK)
    v_s[...] = v_ref[...].astype(F32)
    t_idx = lax.broadcasted_iota(jnp.int32, (c, 1), 0)
    for j in range(n):
        def key_row(s, acc, j=j):
            row = pl.ds(j * c + s, 1)
            w = jnp.exp2(jnp.minimum(b[j] - b_s[row, :], 0.0))
            seen = (t_idx >= s) if forward else (t_idx <= s)
            a_col = jnp.sum(jnp.where(seen, q[j] * w * k_s[row, :], 0.0),
                            axis=1, keepdims=True)
            return acc + a_col * v_s[row, :]

        intra = lax.fori_loop(0, c, key_row, jnp.zeros((c, HGRN_DV), F32))
        inter = _dot_nt(qs[j], states[j])
        o_ref[j * c:(j + 1) * c, :] = (inter + intra).astype(o_ref.dtype)


def _hgrn_direction(q_ref, z_ref, v_ref, lb, st_ref, o_ref, exact_scratch, *, forward, lag):
    c = HGRN_CHUNK
    n = q_ref.shape[0] // c
    row = lax.broadcasted_iota(jnp.int32, (c, c), 0)
    col = lax.broadcasted_iota(jnp.int32, (c, c), 1)
    causal = (col <= row) if forward else (col >= row)
    tri = causal.astype(BF16)
    edge = c - 1 if forward else 0

    chunks = [slice(j * c, (j + 1) * c) for j in range(n)]
    f = lb + (1.0 - lb) * jax.nn.sigmoid(z_ref[...].astype(F32))
    k = (1.0 - f).reshape(n, c, HGRN_DK)
    log_f = jnp.log2(f).astype(BF16)
    q = q_ref[...].reshape(n, c, HGRN_DK)
    vt = v_ref[...].astype(F32).T.astype(BF16)
    yield
    b = jnp.concatenate([_dot(tri, log_f[r]) for r in chunks], axis=0)
    yield
    b = b.reshape(n, c, HGRN_DK)
    b_edge = b[:, edge:edge + 1]
    kb = k.astype(BF16)
    qs = q * jnp.exp2(b).astype(BF16)
    ki = kb * jnp.exp2(-b).astype(BF16)
    kd = kb * jnp.exp2(b_edge - b).astype(BF16)
    decay = jnp.exp2(b_edge)
    spread = jnp.max(-b_edge)
    yield
    a = [_dot_nt(qs[j], ki[j]) for j in range(n)]
    upd = [_dot(vt[:, chunks[j]], kd[j]) for j in range(n)]
    yield
    a = [jnp.where(causal, x, 0.0).astype(BF16) for x in a]
    st = st_ref[...]
    states = [None] * n
    for j in (range(n) if forward else reversed(range(n))):
        states[j] = st.astype(BF16)
        st = st * decay[j] + upd[j]
    st_ref[...] = st
    yield
    outs = [_dot_nt(jnp.concatenate([a[j], qs[j]], axis=1),
                    jnp.concatenate([vt[:, chunks[j]], states[j]], axis=1))
            for j in range(n)]
    o_ref[...] = jnp.concatenate(outs, axis=0).astype(o_ref.dtype)
    yield
    for _ in range(lag):
        yield

    @pl.when(spread > HGRN_SAFE_LOG2_SPREAD)
    def _():
        _hgrn_exact_block(q.astype(F32), k, b, v_ref, qs, states, o_ref, *exact_scratch,
                          forward=forward)
    yield


def _hgrn_kernel(qf_ref, zf_ref, vf_ref, qb_ref, zb_ref, vb_ref, lbf_ref, lbb_ref,
                 of_ref, ob_ref, stf_ref, stb_ref, k_s, b_s, v_s):
    @pl.when(pl.program_id(2) == 0)
    def _():
        stf_ref[...] = jnp.zeros_like(stf_ref)
        stb_ref[...] = jnp.zeros_like(stb_ref)

    def lower_bound(ref, head):
        l = ref[:, head * LANE:(head + 1) * LANE]
        e = jnp.exp(l - jnp.max(l, axis=0, keepdims=True))
        return e[0:1, :] / jnp.sum(e, axis=0, keepdims=True)

    scans = []
    lag = 2 * qf_ref.shape[0] - 1
    for h in range(qf_ref.shape[0]):
        scans.append(_hgrn_direction(qf_ref.at[h], zf_ref.at[h], vf_ref.at[h],
                                     lower_bound(lbf_ref, h), stf_ref.at[h], of_ref.at[h],
                                     (k_s, b_s, v_s), forward=True, lag=lag))
        scans.append(_hgrn_direction(qb_ref.at[h], zb_ref.at[h], vb_ref.at[h],
                                     lower_bound(lbb_ref, h), stb_ref.at[h], ob_ref.at[h],
                                     (k_s, b_s, v_s), forward=False, lag=lag))
    _run_staggered(scans)


def _hgrn(proj, lbf_logits, lbb_logits, col_q, col_ff, col_fb, col_v, tb=1024):
    bsz, _, s, _ = proj.shape
    nb = s // tb
    hps = HGRN_HEADS_PER_STEP

    def fwd(col):
        return pl.BlockSpec((None, hps, tb, LANE), lambda b, h, i: (b, col // hps + h, i, 0))

    def bwd(col):
        return pl.BlockSpec((None, hps, tb, LANE),
                            lambda b, h, i: (b, col // hps + h, nb - 1 - i, 0))

    assert all(col % hps == 0 for col in (col_q, col_ff, col_fb, col_v))
    lb_spec = pl.BlockSpec((lbf_logits.shape[0], hps * LANE), lambda b, h, i: (0, h))
    out_shape = jax.ShapeDtypeStruct((bsz, HGRN_HEADS, s, HGRN_DV), BF16)
    return pl.pallas_call(
        _hgrn_kernel,
        grid=(bsz, HGRN_HEADS // hps, nb),
        in_specs=[fwd(col_q), fwd(col_ff), fwd(col_v),
                  bwd(col_q), bwd(col_fb), bwd(col_v), lb_spec, lb_spec],
        out_specs=[fwd(0), bwd(0)],
        out_shape=[out_shape, out_shape],
        scratch_shapes=[pltpu.VMEM((hps, HGRN_DV, HGRN_DK), F32),
                        pltpu.VMEM((hps, HGRN_DV, HGRN_DK), F32),
                        pltpu.VMEM((tb, HGRN_DK), F32), pltpu.VMEM((tb, HGRN_DK), F32),
                        pltpu.VMEM((tb, HGRN_DV), F32)],
        compiler_params=_cparams("parallel", "parallel", "arbitrary"),
        name="hgrn",
    )(proj, proj, proj, proj, proj, proj, lbf_logits, lbb_logits)


def _mix_out_kernel(attn_ref, of_ref, ob_ref, hg_ref, hgn_ref, w_ref, x_ref, mod_ref,
                    postg_ref, preg_ref, x1_ref, h2_ref):
    na = attn_ref.shape[-1]
    tm = attn_ref.shape[0]
    tiles = [slice(r, r + MIX_SUB_ROWS) for r in range(0, tm, MIX_SUB_ROWS)]
    recs = []
    for t in tiles:
        rec = []
        for h in range(HGRN_HEADS):
            o = of_ref[h, t, :].astype(F32) + ob_ref[h, t, :].astype(F32)
            hg = hg_ref[h, t, :].astype(F32)
            rec.append((_rms(o, hgn_ref[...]) * (hg * jax.nn.sigmoid(hg))).astype(BF16))
        recs.append(jnp.concatenate(rec, axis=-1))
    mixes = [_dot(attn_ref[t, :], w_ref[0:na, :]) + _dot(rec, w_ref[na:, :])
             for t, rec in zip(tiles, recs)]
    post_gain = mod_ref[2:3, :] * postg_ref[...]
    pre_gain = preg_ref[...] * (1.0 + mod_ref[4:5, :])
    for t, mix in zip(tiles, mixes):
        x1 = x_ref[t, :] + _rms(mix, post_gain)
        x1_ref[t, :] = x1
        h2_ref[t, :] = (_rms(x1, pre_gain) + mod_ref[3:4, :]).astype(BF16)


def _mix_out(attn, o_f, o_b, proj, col_hg, hgn_g, w_out, x, mod3, post_g, pre_g, tm=512):
    bsz, s, d = x.shape
    na = attn.shape[-1]
    heads = o_f.shape[1]

    def row(width):
        return pl.BlockSpec((None, tm, width), lambda b, i: (b, i, 0))

    def per_head(first):
        return pl.BlockSpec((None, heads, tm, LANE), lambda b, i: (b, first // heads, i, 0))

    def const(shape):
        return pl.BlockSpec(shape, lambda b, i: (0,) * len(shape))

    assert col_hg % heads == 0
    return pl.pallas_call(
        _mix_out_kernel,
        grid=(bsz, s // tm),
        in_specs=[row(na), per_head(0), per_head(0), per_head(col_hg),
                  const((1, HGRN_DV)), const(w_out.shape), row(d),
                  pl.BlockSpec((None, N_MOD, d), lambda b, i: (b, 0, 0)),
                  const((1, d)), const((1, d))],
        out_specs=[row(d), row(d)],
        out_shape=[jax.ShapeDtypeStruct((bsz, s, d), F32),
                   jax.ShapeDtypeStruct((bsz, s, d), BF16)],
        compiler_params=_cparams("parallel", "parallel"),
        name="mix_out",
    )(attn, o_f, o_b, proj, hgn_g, w_out, x, mod3, post_g, pre_g)


def _mlp_kernel(h_ref, wu_ref, wd_ref, x1_ref, mod_ref, g_ref, o_ref):
    j = pl.program_id(2)
    last = pl.num_programs(2) - 1

    @pl.when(j == 0)
    def _():
        o_ref[...] = jnp.zeros_like(o_ref)

    def activation():
        u = jnp.maximum(_dot(h_ref[...], wu_ref[...]), 0.0)
        return (u * u).astype(BF16)

    @pl.when(j < last)
    def _():
        o_ref[...] += _dot(activation(), wd_ref[...])

    @pl.when(j == last)
    def _():
        a = activation()
        gain = mod_ref[5:6, :] * g_ref[...]
        for r in range(0, o_ref.shape[0], MLP_SUB_ROWS):
            t = slice(r, r + MLP_SUB_ROWS)
            y = o_ref[t, :] + _dot(a[t, :], wd_ref[...])
            o_ref[t, :] = x1_ref[t, :] + _rms(y, gain)


def _mlp(h2, w_up, w_down, x1, mod3, g, tm=512, tf=1024):
    bsz, s, d = x1.shape
    dff = w_up.shape[1]
    return pl.pallas_call(
        _mlp_kernel,
        grid=(bsz, s // tm, dff // tf),
        in_specs=[pl.BlockSpec((None, tm, d), lambda b, i, j: (b, i, 0)),
                  pl.BlockSpec((d, tf), lambda b, i, j: (0, j)),
                  pl.BlockSpec((tf, d), lambda b, i, j: (j, 0)),
                  pl.BlockSpec((None, tm, d), lambda b, i, j: (b, i, 0)),
                  pl.BlockSpec((None, N_MOD, d), lambda b, i, j: (b, 0, 0)),
                  pl.BlockSpec((1, d), lambda b, i, j: (0, 0))],
        out_specs=pl.BlockSpec((None, tm, d), lambda b, i, j: (b, i, 0)),
        out_shape=jax.ShapeDtypeStruct((bsz, s, d), F32),
        compiler_params=_cparams("parallel", "parallel", "arbitrary"),
        name="mlp",
    )(h2, w_up, w_down, x1, mod3, g)


def _pack_w_uq(w):
    r = w.shape[0]
    w = w.reshape(r, MLA_HEADS, MLA_NOPE + MLA_ROPE)
    pad = jnp.zeros((r, MLA_HEADS, MLA_QK_PAD - MLA_NOPE - MLA_ROPE), w.dtype)
    return jnp.concatenate([w, pad], axis=-1).reshape(r, MLA_HEADS * MLA_QK_PAD).astype(BF16)


def _pack_w_ukv(w):
    r = w.shape[0]
    w = w.reshape(r, MLA_HEADS, MLA_NOPE + MLA_V)
    wk = w[:, :, :MLA_NOPE].reshape(r, -1).astype(BF16)
    wvt = w[:, :, MLA_NOPE:].reshape(r, -1).T.astype(BF16)
    return wk, wvt


def _rope_inv_freq():
    half = MLA_ROPE // 2
    f = ROPE_BASE ** (-jnp.arange(half, dtype=F32) / half)
    return jnp.tile(f, ROPE_PACK).reshape(1, LANE)


def kernel(x, c, positions, w_mod, b_mod, pre_mix_g, post_mix_g, pre_mlp_g, post_mlp_g,
           w_in, q_norm_g, kv_norm_g, w_uq, w_ukv, hgrn_norm_g, hgrn_lb_logits_fwd,
           hgrn_lb_logits_bwd, w_out, w_up, w_down):
    depth = w_mod.shape[0]
    assert depth == 1, "lower-bound slot selection is written for a single layer"
    bsz, s, d = x.shape
    q_rank = q_norm_g.shape[-1]
    kv_rank = kv_norm_g.shape[-1]
    invf = _rope_inv_freq()
    for layer in range(depth):
        mod3 = _mod(c, w_mod[layer], b_mod[layer]).reshape(bsz, N_MOD, d)
        lat, proj = _in_proj(x, mod3, pre_mix_g[layer].reshape(1, d),
                             w_in[layer].T.astype(BF16), q_rank + kv_rank + MLA_ROPE)
        wk, wvt = _pack_w_ukv(w_ukv[layer])
        q, k, vt = _mla_prep(lat, positions, invf, q_norm_g[layer].reshape(1, -1),
                             kv_norm_g[layer].reshape(1, -1), _pack_w_uq(w_uq[layer]),
                             wk, wvt, q_rank, kv_rank)
        attn, (w_out_b, w_up_b, w_down_b) = _attention(
            q, k, vt, (w_out[layer], w_up[layer], w_down[layer]))
        o_f, o_b = _hgrn(proj, hgrn_lb_logits_fwd, hgrn_lb_logits_bwd,
                         col_q=0, col_ff=8, col_fb=16, col_v=24)
        x1, h2 = _mix_out(attn, o_f, o_b, proj, 32, hgrn_norm_g[layer].reshape(1, -1),
                          w_out_b, x, mod3,
                          post_mix_g[layer].reshape(1, d), pre_mlp_g[layer].reshape(1, d))
        x = _mlp(h2, w_up_b, w_down_b, x1, mod3, post_mlp_g[layer].reshape(1, d))
    return x
```

```python
import functools

import jax
import jax.numpy as jnp
import numpy as np
from jax import lax
from jax.experimental import pallas as pl
from jax.experimental.pallas import tpu as pltpu

F32 = jnp.float32
BF16 = jnp.bfloat16

NORM_EPS = 1e-6
ROPE_BASE = 10000.0
N_MOD = 6
LANE = 128
VMEM_LIMIT = 56 * 1024 * 1024

MLA_HEADS = 8
MLA_NOPE = 128
MLA_ROPE = 64
MLA_V = 128
MLA_QK_PAD = 256
ROPE_PACK = LANE // (MLA_ROPE // 2)
ATTN_TILES_PER_TRIP = 4
MLA_LAT_PAD = 1024
HGRN_HEADS = 8
HGRN_DK = 128
HGRN_DV = 128
HGRN_CHUNK = 64
HGRN_HEADS_PER_STEP = 2
HGRN_SAFE_LOG2_SPREAD = 96.0
NORM_SUB_ROWS = 256
MLP_SUB_ROWS = 256
MIX_SUB_ROWS = 128


def _cparams(*sem):
    return pltpu.CompilerParams(dimension_semantics=sem, vmem_limit_bytes=VMEM_LIMIT)


def _dot(a, b):
    return jnp.dot(a, b, preferred_element_type=F32)


def _dot_nt(a, b):
    return lax.dot_general(a, b, (((1,), (1,)), ((), ())), preferred_element_type=F32)


def _rms(x, g):
    return x * lax.rsqrt(jnp.mean(x * x, axis=-1, keepdims=True) + NORM_EPS) * g


def _mod_kernel(c_ref, w_ref, b_ref, o_ref):
    c = c_ref[...]
    cond = c * jax.nn.sigmoid(c)
    hi = cond.astype(BF16)
    lo = (cond - hi.astype(F32)).astype(BF16)
    w = w_ref[...].astype(BF16)
    o_ref[...] = _dot(hi, w) + _dot(lo, w) + b_ref[...]


def _mod(c, w_mod, b_mod, tn=1024):
    bsz, d = c.shape
    n = w_mod.shape[1]
    return pl.pallas_call(
        _mod_kernel,
        grid=(n // tn,),
        in_specs=[pl.BlockSpec((bsz, d), lambda j: (0, 0)),
                  pl.BlockSpec((d, tn), lambda j: (0, j)),
                  pl.BlockSpec((1, tn), lambda j: (0, j))],
        out_specs=pl.BlockSpec((bsz, tn), lambda j: (0, j)),
        out_shape=jax.ShapeDtypeStruct((bsz, n), F32),
        compiler_params=_cparams("parallel"),
        name="mod",
    )(c, w_mod, b_mod.reshape(1, n))


def _in_proj_kernel(x_ref, mod_ref, g_ref, wa_ref, wb_ref, oa_ref, ob_ref, h_ref):
    @pl.when(pl.program_id(2) == 0)
    def _():
        gain = g_ref[...] * (1.0 + mod_ref[1:2, :])
        for r in range(0, x_ref.shape[0], NORM_SUB_ROWS):
            t = slice(r, r + NORM_SUB_ROWS)
            h = (_rms(x_ref[t, :], gain) + mod_ref[0:1, :]).astype(BF16)
            h_ref[t, :] = h
            oa_ref[t, :] = _dot_nt(h, wa_ref[...]).astype(oa_ref.dtype)

    y = _dot_nt(h_ref[...], wb_ref[...]).astype(ob_ref.dtype)
    for c in range(ob_ref.shape[0]):
        ob_ref[c] = y[:, c * LANE:(c + 1) * LANE]


def _in_proj(x, mod3, g, wt, head, tm=1024, tn=1280):
    bsz, s, d = x.shape
    na, nb = MLA_LAT_PAD, wt.shape[0] - head
    row_align = int(np.gcd(head, tn))
    assert row_align % 16 == 0
    return pl.pallas_call(
        _in_proj_kernel,
        grid=(bsz, s // tm, nb // tn),
        in_specs=[pl.BlockSpec((None, tm, d), lambda b, i, j: (b, i, 0)),
                  pl.BlockSpec((None, N_MOD, d), lambda b, i, j: (b, 0, 0)),
                  pl.BlockSpec((1, d), lambda b, i, j: (0, 0)),
                  pl.BlockSpec((pl.Element(na), pl.Element(d)), lambda b, i, j: (0, 0)),
                  pl.BlockSpec((pl.Element(tn), pl.Element(d)),
                               lambda b, i, j: (pl.multiple_of(head + j * tn, row_align), 0))],
        out_specs=[pl.BlockSpec((None, tm, na), lambda b, i, j: (b, i, 0)),
                   pl.BlockSpec((None, tn // LANE, tm, LANE), lambda b, i, j: (b, j, i, 0))],
        out_shape=[jax.ShapeDtypeStruct((bsz, s, na), BF16),
                   jax.ShapeDtypeStruct((bsz, nb // LANE, s, LANE), BF16)],
        scratch_shapes=[pltpu.VMEM((tm, d), BF16)],
        compiler_params=_cparams("parallel", "parallel", "arbitrary"),
        name="in_proj",
    )(x, mod3, g, wt, wt)


def _rope_tile(t, cos_m, sin_lo, sin_hi):
    half = MLA_ROPE // 2
    return (t * cos_m + pltpu.roll(t, LANE - half, 1) * sin_lo
            + pltpu.roll(t, half, 1) * sin_hi)


def _rope_tables(pos_ref, invf_ref, tab_ref):
    half = MLA_ROPE // 2
    rows = pos_ref.shape[0]
    pos = pos_ref[...].astype(F32)
    lane = lax.broadcasted_iota(jnp.int32, (rows, LANE), 1)
    spread = pos[:, ROPE_PACK - 1:ROPE_PACK]
    for g in range(ROPE_PACK - 2, -1, -1):
        spread = jnp.where(lane < (g + 1) * half, pos[:, g:g + 1], spread)
    ang = spread * invf_ref[...]
    cos, sin = jnp.cos(ang), jnp.sin(ang)
    for g in range(ROPE_PACK):
        c = pltpu.roll(cos, LANE - g * half, 1) if g else cos
        s = pltpu.roll(sin, LANE - g * half, 1) if g else sin
        in_lo = lane < half
        in_hi = (lane >= half) & (lane < MLA_ROPE)
        dst = pl.ds(g, rows, stride=ROPE_PACK)
        tab_ref[0, dst, :] = jnp.where(in_lo, c, jnp.where(in_hi, pltpu.roll(c, half, 1), 0.0))
        tab_ref[1, dst, :] = jnp.where(in_lo, -s, 0.0)
        tab_ref[2, dst, :] = jnp.where(in_hi, pltpu.roll(s, half, 1), 0.0)


def _mla_prep_kernel(p_ref, pos_ref, invf_ref, qg_ref, kvg_ref, wq_ref, wk_ref, wvt_ref,
                     q_ref, k_ref, vt_ref, tab_ref, *, q_rank, kv_rank, scale):
    _rope_tables(pos_ref, invf_ref, tab_ref)
    cos_m, sin_lo, sin_hi = tab_ref[0], tab_ref[1], tab_ref[2]

    q_lat = p_ref[:, 0:q_rank].astype(F32)
    q = _dot(_rms(q_lat, qg_ref[...]).astype(BF16), wq_ref[...])
    kv_lat = p_ref[:, q_rank:q_rank + kv_rank].astype(F32)
    kv_n = _rms(kv_lat, kvg_ref[...]).astype(BF16)
    k_nope = _dot(kv_n, wk_ref[...])
    vt_ref[...] = _dot_nt(wvt_ref[...], kv_n).astype(BF16)
    kr_off = q_rank + kv_rank
    k_rope = _rope_tile(p_ref[:, kr_off:kr_off + LANE].astype(F32),
                        cos_m, sin_lo, sin_hi).astype(BF16)
    for h in range(MLA_HEADS):
        o = h * MLA_QK_PAD
        q_ref[:, o:o + MLA_NOPE] = (q[:, o:o + MLA_NOPE] * scale).astype(BF16)
        qr = _rope_tile(q[:, o + MLA_NOPE:o + MLA_QK_PAD], cos_m, sin_lo, sin_hi)
        q_ref[:, o + MLA_NOPE:o + MLA_QK_PAD] = (qr * scale).astype(BF16)
        k_ref[:, o:o + MLA_NOPE] = k_nope[:, h * MLA_NOPE:(h + 1) * MLA_NOPE].astype(BF16)
        k_ref[:, o + MLA_NOPE:o + MLA_QK_PAD] = k_rope


def _mla_prep(proj, positions, invf, qg, kvg, wq, wk, wvt, q_rank, kv_rank, tm=512):
    bsz, s, blk0 = proj.shape
    hq = MLA_HEADS * MLA_QK_PAD
    hv = MLA_HEADS * MLA_V
    scale = float((MLA_NOPE + MLA_ROPE) ** -0.5 * np.log2(np.e))
    kern = functools.partial(_mla_prep_kernel, q_rank=q_rank, kv_rank=kv_rank, scale=scale)

    def const(a):
        return pl.BlockSpec(a.shape, lambda b, i: (0, 0))

    return pl.pallas_call(
        kern,
        grid=(bsz, s // tm),
        in_specs=[pl.BlockSpec((None, tm, blk0), lambda b, i: (b, i, 0)),
                  pl.BlockSpec((None, tm // ROPE_PACK, ROPE_PACK), lambda b, i: (b, i, 0)),
                  const(invf), const(qg), const(kvg), const(wq), const(wk), const(wvt)],
        out_specs=[pl.BlockSpec((None, tm, hq), lambda b, i: (b, i, 0)),
                   pl.BlockSpec((None, tm, hq), lambda b, i: (b, i, 0)),
                   pl.BlockSpec((None, hv, tm), lambda b, i: (b, 0, i))],
        out_shape=[jax.ShapeDtypeStruct((bsz, s, hq), BF16),
                   jax.ShapeDtypeStruct((bsz, s, hq), BF16),
                   jax.ShapeDtypeStruct((bsz, hv, s), BF16)],
        scratch_shapes=[pltpu.VMEM((3, tm, LANE), F32)],
        compiler_params=_cparams("parallel", "parallel"),
        name="mla_prep",
    )(proj, positions.reshape(bsz, s // ROPE_PACK, ROPE_PACK), invf, qg, kvg, wq, wk, wvt)


def _attn_kernel(*refs, tq, tk, n_cast):
    q_ref, k_ref, vt_ref = refs[:3]
    o_ref = refs[3 + n_cast]
    sa_ref, sb_ref = refs[-2:]
    for src, dst in zip(refs[3:3 + n_cast], refs[4 + n_cast:4 + 2 * n_cast]):
        dst[...] = src[...].astype(dst.dtype)
    s_len = k_ref.shape[0]
    n_kv = s_len // tk
    n_q = s_len // tq
    sub = 8

    def scores(i, s_ref):
        q = q_ref[pl.ds(pl.multiple_of(i * tq, tq), tq), :]
        m8 = jnp.full((sub, tq), -jnp.inf, F32)
        for j in range(n_kv):
            st = _dot_nt(k_ref[j * tk:(j + 1) * tk, :], q)
            s_ref[j * tk:(j + 1) * tk, :] = st
            m8 = jnp.maximum(m8, jnp.max(st.reshape(tk // sub, sub, tq), axis=0))
            yield
        return jnp.broadcast_to(jnp.max(m8, axis=0, keepdims=True), (sub, tq))

    def values(i, s_ref, m8):
        l8 = jnp.zeros((sub, tq), F32)
        acc = jnp.zeros((MLA_V, tq), F32)
        for j in range(n_kv):
            st = s_ref[j * tk:(j + 1) * tk, :].reshape(tk // sub, sub, tq)
            p = jnp.exp2(st - m8[None])
            l8 = l8 + jnp.sum(p, axis=0)
            acc = acc + _dot(vt_ref[:, j * tk:(j + 1) * tk],
                             p.reshape(tk, tq).astype(BF16))
            yield
        l = jnp.sum(l8, axis=0, keepdims=True)
        o_ref[pl.ds(pl.multiple_of(i * tq, tq), tq), :] = (acc / l).T.astype(o_ref.dtype)

    def run(*gens):
        result = None
        live = list(gens)
        while live:
            for g in list(live):
                try:
                    next(g)
                except StopIteration as stop:
                    if g is gens[0]:
                        result = stop.value
                    live.remove(g)
        return result

    bufs = (sa_ref, sb_ref)
    group = ATTN_TILES_PER_TRIP

    def tiles(first, count, m):
        for u in range(count):
            m = run(scores(first + u + 1, bufs[(u + 1) % 2]),
                    values(first + u, bufs[u % 2], m))
        return m

    m = lax.fori_loop(0, n_q // group - 1, lambda k, m: tiles(group * k, group, m),
                      run(scores(0, bufs[0])))
    m = tiles(n_q - group, group - 1, m)
    run(values(n_q - 1, bufs[(group - 1) % 2], m))


def _attention(q, k, vt, f32_weights, tq=256, tk=2048):
    bsz, s, _ = q.shape
    steps = bsz * MLA_HEADS
    kern = functools.partial(_attn_kernel, tq=tq, tk=tk, n_cast=len(f32_weights))

    def slab(w):
        return pl.BlockSpec((w.shape[0] // steps, w.shape[1]),
                            lambda b, h: (b * MLA_HEADS + h, 0))

    assert all(w.shape[0] % (8 * steps) == 0 for w in f32_weights)
    out = pl.pallas_call(
        kern,
        grid=(bsz, MLA_HEADS),
        in_specs=[pl.BlockSpec((None, s, MLA_QK_PAD), lambda b, h: (b, 0, h)),
                  pl.BlockSpec((None, s, MLA_QK_PAD), lambda b, h: (b, 0, h)),
                  pl.BlockSpec((None, MLA_V, s), lambda b, h: (b, h, 0))]
                 + [slab(w) for w in f32_weights],
        out_specs=[pl.BlockSpec((None, s, MLA_V), lambda b, h: (b, 0, h))]
                  + [slab(w) for w in f32_weights],
        out_shape=[jax.ShapeDtypeStruct((bsz, s, MLA_HEADS * MLA_V), BF16)]
                  + [jax.ShapeDtypeStruct(w.shape, BF16) for w in f32_weights],
        scratch_shapes=[pltpu.VMEM((s, tq), F32), pltpu.VMEM((s, tq), F32)],
        compiler_params=_cparams("parallel", "parallel"),
        name="attention",
    )(q, k, vt, *f32_weights)
    return out[0], out[1:]


def _run_staggered(gens):
    live = list(enumerate(gens))
    tick = 0
    while live:
        for delay, g in list(live):
            if tick >= delay and next(g, StopIteration) is StopIteration:
                live.remove((delay, g))
        tick += 1


def _hgrn_exact_block(q, k, b, v_ref, qs, states, o_ref, k_s, b_s, v_s, *, forward):
    n, c, _ = q.shape
    k_s[...] = k.reshape(n * c, HGRN_DK)
    b_s[...] = b.reshape(n * c, HGRN_DK)
    v_s[...] = v_ref[...].astype(F32)
    t_idx = lax.broadcasted_iota(jnp.int32, (c, 1), 0)
    for j in range(n):
        def key_row(s, acc, j=j):
            row = pl.ds(j * c + s, 1)
            w = jnp.exp2(jnp.minimum(b[j] - b_s[row, :], 0.0))
            seen = (t_idx >= s) if forward else (t_idx <= s)
            a_col = jnp.sum(jnp.where(seen, q[j] * w * k_s[row, :], 0.0),
                            axis=1, keepdims=True)
            return acc + a_col * v_s[row, :]

        intra = lax.fori_loop(0, c, key_row, jnp.zeros((c, HGRN_DV), F32))
        inter = _dot_nt(qs[j], states[j])
        o_ref[j * c:(j + 1) * c, :] = (inter + intra).astype(o_ref.dtype)


def _hgrn_direction(q_ref, z_ref, v_ref, lb, st_ref, o_ref, exact_scratch, *, forward, lag):
    c = HGRN_CHUNK
    n = q_ref.shape[0] // c
    row = lax.broadcasted_iota(jnp.int32, (c, c), 0)
    col = lax.broadcasted_iota(jnp.int32, (c, c), 1)
    causal = (col <= row) if forward else (col >= row)
    tri = causal.astype(BF16)
    edge = c - 1 if forward else 0

    chunks = [slice(j * c, (j + 1) * c) for j in range(n)]
    f = lb + (1.0 - lb) * jax.nn.sigmoid(z_ref[...].astype(F32))
    k = (1.0 - f).reshape(n, c, HGRN_DK)
    log_f = jnp.log2(f).astype(BF16)
    q = q_ref[...].reshape(n, c, HGRN_DK)
    vt = v_ref[...].astype(F32).T.astype(BF16)
    yield
    b = jnp.concatenate([_dot(tri, log_f[r]) for r in chunks], axis=0)
    yield
    b = b.reshape(n, c, HGRN_DK)
    b_edge = b[:, edge:edge + 1]
    kb = k.astype(BF16)
    qs = q * jnp.exp2(b).astype(BF16)
    ki = kb * jnp.exp2(-b).astype(BF16)
    kd = kb * jnp.exp2(b_edge - b).astype(BF16)
    decay = jnp.exp2(b_edge)
    spread = jnp.max(-b_edge)
    yield
    a = [_dot_nt(qs[j], ki[j]) for j in range(n)]
    upd = [_dot(vt[:, chunks[j]], kd[j]) for j in range(n)]
    yield
    a = [jnp.where(causal, x, 0.0).astype(BF16) for x in a]
    st = st_ref[...]
    states = [None] * n
    for j in (range(n) if forward else reversed(range(n))):
        states[j] = st.astype(BF16)
        st = st * decay[j] + upd[j]
    st_ref[...] = st
    yield
    outs = [_dot_nt(jnp.concatenate([a[j], qs[j]], axis=1),
                    jnp.concatenate([vt[:, chunks[j]], states[j]], axis=1))
            for j in range(n)]
    o_ref[...] = jnp.concatenate(outs, axis=0).astype(o_ref.dtype)
    yield
    for _ in range(lag):
        yield

    @pl.when(spread > HGRN_SAFE_LOG2_SPREAD)
    def _():
        _hgrn_exact_block(q.astype(F32), k, b, v_ref, qs, states, o_ref, *exact_scratch,
                          forward=forward)
    yield


def _hgrn_kernel(qf_ref, zf_ref, vf_ref, qb_ref, zb_ref, vb_ref, lbf_ref, lbb_ref,
                 of_ref, ob_ref, stf_ref, stb_ref, k_s, b_s, v_s):
    @pl.when(pl.program_id(2) == 0)
    def _():
        stf_ref[...] = jnp.zeros_like(stf_ref)
        stb_ref[...] = jnp.zeros_like(stb_ref)

    def lower_bound(ref, head):
        l = ref[:, head * LANE:(head + 1) * LANE]
        e = jnp.exp(l - jnp.max(l, axis=0, keepdims=True))
        return e[0:1, :] / jnp.sum(e, axis=0, keepdims=True)

    scans = []
    lag = 2 * qf_ref.shape[0] - 1
    for h in range(qf_ref.shape[0]):
        scans.append(_hgrn_direction(qf_ref.at[h], zf_ref.at[h], vf_ref.at[h],
                                     lower_bound(lbf_ref, h), stf_ref.at[h], of_ref.at[h],
                                     (k_s, b_s, v_s), forward=True, lag=lag))
        scans.append(_hgrn_direction(qb_ref.at[h], zb_ref.at[h], vb_ref.at[h],
                                     lower_bound(lbb_ref, h), stb_ref.at[h], ob_ref.at[h],
                                     (k_s, b_s, v_s), forward=False, lag=lag))
    _run_staggered(scans)


def _hgrn(proj, lbf_logits, lbb_logits, col_q, col_ff, col_fb, col_v, tb=1024):
    bsz, _, s, _ = proj.shape
    nb = s // tb
    hps = HGRN_HEADS_PER_STEP

    def fwd(col):
        return pl.BlockSpec((None, hps, tb, LANE), lambda b, h, i: (b, col // hps + h, i, 0))

    def bwd(col):
        return pl.BlockSpec((None, hps, tb, LANE),
                            lambda b, h, i: (b, col // hps + h, nb - 1 - i, 0))

    assert all(col % hps == 0 for col in (col_q, col_ff, col_fb, col_v))
    lb_spec = pl.BlockSpec((lbf_logits.shape[0], hps * LANE), lambda b, h, i: (0, h))
    out_shape = jax.ShapeDtypeStruct((bsz, HGRN_HEADS, s, HGRN_DV), BF16)
    return pl.pallas_call(
        _hgrn_kernel,
        grid=(bsz, HGRN_HEADS // hps, nb),
        in_specs=[fwd(col_q), fwd(col_ff), fwd(col_v),
                  bwd(col_q), bwd(col_fb), bwd(col_v), lb_spec, lb_spec],
        out_specs=[fwd(0), bwd(0)],
        out_shape=[out_shape, out_shape],
        scratch_shapes=[pltpu.VMEM((hps, HGRN_DV, HGRN_DK), F32),
                        pltpu.VMEM((hps, HGRN_DV, HGRN_DK), F32),
                        pltpu.VMEM((tb, HGRN_DK), F32), pltpu.VMEM((tb, HGRN_DK), F32),
                        pltpu.VMEM((tb, HGRN_DV), F32)],
        compiler_params=_cparams("parallel", "parallel", "arbitrary"),
        name="hgrn",
    )(proj, proj, proj, proj, proj, proj, lbf_logits, lbb_logits)


def _mix_out_kernel(attn_ref, of_ref, ob_ref, hg_ref, hgn_ref, w_ref, x_ref, mod_ref,
                    postg_ref, preg_ref, x1_ref, h2_ref):
    na = attn_ref.shape[-1]
    tm = attn_ref.shape[0]
    tiles = [slice(r, r + MIX_SUB_ROWS) for r in range(0, tm, MIX_SUB_ROWS)]
    recs = []
    for t in tiles:
        rec = []
        for h in range(HGRN_HEADS):
            o = of_ref[h, t, :].astype(F32) + ob_ref[h, t, :].astype(F32)
            hg = hg_ref[h, t, :].astype(F32)
            rec.append((_rms(o, hgn_ref[...]) * (hg * jax.nn.sigmoid(hg))).astype(BF16))
        recs.append(jnp.concatenate(rec, axis=-1))
    mixes = [_dot(attn_ref[t, :], w_ref[0:na, :]) + _dot(rec, w_ref[na:, :])
             for t, rec in zip(tiles, recs)]
    post_gain = mod_ref[2:3, :] * postg_ref[...]
    pre_gain = preg_ref[...] * (1.0 + mod_ref[4:5, :])
    for t, mix in zip(tiles, mixes):
        x1 = x_ref[t, :] + _rms(mix, post_gain)
        x1_ref[t, :] = x1
        h2_ref[t, :] = (_rms(x1, pre_gain) + mod_ref[3:4, :]).astype(BF16)


def _mix_out(attn, o_f, o_b, proj, col_hg, hgn_g, w_out, x, mod3, post_g, pre_g, tm=512):
    bsz, s, d = x.shape
    na = attn.shape[-1]
    heads = o_f.shape[1]

    def row(width):
        return pl.BlockSpec((None, tm, width), lambda b, i: (b, i, 0))

    def per_head(first):
        return pl.BlockSpec((None, heads, tm, LANE), lambda b, i: (b, first // heads, i, 0))

    def const(shape):
        return pl.BlockSpec(shape, lambda b, i: (0,) * len(shape))

    assert col_hg % heads == 0
    return pl.pallas_call(
        _mix_out_kernel,
        grid=(bsz, s // tm),
        in_specs=[row(na), per_head(0), per_head(0), per_head(col_hg),
                  const((1, HGRN_DV)), const(w_out.shape), row(d),
                  pl.BlockSpec((None, N_MOD, d), lambda b, i: (b, 0, 0)),
                  const((1, d)), const((1, d))],
        out_specs=[row(d), row(d)],
        out_shape=[jax.ShapeDtypeStruct((bsz, s, d), F32),
                   jax.ShapeDtypeStruct((bsz, s, d), BF16)],
        compiler_params=_cparams("parallel", "parallel"),
        name="mix_out",
    )(attn, o_f, o_b, proj, hgn_g, w_out, x, mod3, post_g, pre_g)


def _mlp_kernel(h_ref, wu_ref, wd_ref, x1_ref, mod_ref, g_ref, o_ref):
    j = pl.program_id(2)
    last = pl.num_programs(2) - 1

    @pl.when(j == 0)
    def _():
        o_ref[...] = jnp.zeros_like(o_ref)

    def activation():
        u = jnp.maximum(_dot(h_ref[...], wu_ref[...]), 0.0)
        return (u * u).astype(BF16)

    @pl.when(j < last)
    def _():
        o_ref[...] += _dot(activation(), wd_ref[...])

    @pl.when(j == last)
    def _():
        a = activation()
        gain = mod_ref[5:6, :] * g_ref[...]
        for r in range(0, o_ref.shape[0], MLP_SUB_ROWS):
            t = slice(r, r + MLP_SUB_ROWS)
            y = o_ref[t, :] + _dot(a[t, :], wd_ref[...])
            o_ref[t, :] = x1_ref[t, :] + _rms(y, gain)


def _mlp(h2, w_up, w_down, x1, mod3, g, tm=512, tf=1024):
    bsz, s, d = x1.shape
    dff = w_up.shape[1]
    return pl.pallas_call(
        _mlp_kernel,
        grid=(bsz, s // tm, dff // tf),
        in_specs=[pl.BlockSpec((None, tm, d), lambda b, i, j: (b, i, 0)),
                  pl.BlockSpec((d, tf), lambda b, i, j: (0, j)),
                  pl.BlockSpec((tf, d), lambda b, i, j: (j, 0)),
                  pl.BlockSpec((None, tm, d), lambda b, i, j: (b, i, 0)),
                  pl.BlockSpec((None, N_MOD, d), lambda b, i, j: (b, 0, 0)),
                  pl.BlockSpec((1, d), lambda b, i, j: (0, 0))],
        out_specs=pl.BlockSpec((None, tm, d), lambda b, i, j: (b, i, 0)),
        out_shape=jax.ShapeDtypeStruct((bsz, s, d), F32),
        compiler_params=_cparams("parallel", "parallel", "arbitrary"),
        name="mlp",
    )(h2, w_up, w_down, x1, mod3, g)


def _pack_w_uq(w):
    r = w.shape[0]
    w = w.reshape(r, MLA_HEADS, MLA_NOPE + MLA_ROPE)
    pad = jnp.zeros((r, MLA_HEADS, MLA_QK_PAD - MLA_NOPE - MLA_ROPE), w.dtype)
    return jnp.concatenate([w, pad], axis=-1).reshape(r, MLA_HEADS * MLA_QK_PAD).astype(BF16)


def _pack_w_ukv(w):
    r = w.shape[0]
    w = w.reshape(r, MLA_HEADS, MLA_NOPE + MLA_V)
    wk = w[:, :, :MLA_NOPE].reshape(r, -1).astype(BF16)
    wvt = w[:, :, MLA_NOPE:].reshape(r, -1).T.astype(BF16)
    return wk, wvt


def _rope_inv_freq():
    half = MLA_ROPE // 2
    f = ROPE_BASE ** (-jnp.arange(half, dtype=F32) / half)
    return jnp.tile(f, ROPE_PACK).reshape(1, LANE)


def kernel(x, c, positions, w_mod, b_mod, pre_mix_g, post_mix_g, pre_mlp_g, post_mlp_g,
           w_in, q_norm_g, kv_norm_g, w_uq, w_ukv, hgrn_norm_g, hgrn_lb_logits_fwd,
           hgrn_lb_logits_bwd, w_out, w_up, w_down):
    depth = w_mod.shape[0]
    assert depth == 1, "lower-bound slot selection is written for a single layer"
    bsz, s, d = x.shape
    q_rank = q_norm_g.shape[-1]
    kv_rank = kv_norm_g.shape[-1]
    invf = _rope_inv_freq()
    for layer in range(depth):
        mod3 = _mod(c, w_mod[layer], b_mod[layer]).reshape(bsz, N_MOD, d)
        lat, proj = _in_proj(x, mod3, pre_mix_g[layer].reshape(1, d),
                             w_in[layer].T.astype(BF16), q_rank + kv_rank + MLA_ROPE)
        wk, wvt = _pack_w_ukv(w_ukv[layer])
        q, k, vt = _mla_prep(lat, positions, invf, q_norm_g[layer].reshape(1, -1),
                             kv_norm_g[layer].reshape(1, -1), _pack_w_uq(w_uq[layer]),
                             wk, wvt, q_rank, kv_rank)
        attn, (w_out_b, w_up_b, w_down_b) = _attention(
            q, k, vt, (w_out[layer], w_up[layer], w_down[layer]))
        o_f, o_b = _hgrn(proj, hgrn_lb_logits_fwd, hgrn_lb_logits_bwd,
                         col_q=0, col_ff=8, col_fb=16, col_v=24)
        x1, h2 = _mix_out(attn, o_f, o_b, proj, 32, hgrn_norm_g[layer].reshape(1, -1),
                          w_out_b, x, mod3,
                          post_mix_g[layer].reshape(1, d), pre_mlp_g[layer].reshape(1, d))
        x = _mlp(h2, w_up_b, w_down_b, x1, mod3, post_mlp_g[layer].reshape(1, d))
    return x
```

```python
import functools

import jax
import jax.numpy as jnp
import numpy as np
from jax import lax
from jax.experimental import pallas as pl
from jax.experimental.pallas import tpu as pltpu

F32 = jnp.float32
BF16 = jnp.bfloat16

NORM_EPS = 1e-6
ROPE_BASE = 10000.0
N_MOD = 6
LANE = 128
VMEM_LIMIT = 56 * 1024 * 1024

MLA_HEADS = 8
MLA_NOPE = 128
MLA_ROPE = 64
MLA_V = 128
MLA_QK_PAD = 256
ROPE_PACK = LANE // (MLA_ROPE // 2)
ATTN_TILES_PER_TRIP = 4
MLA_LAT_PAD = 1024
HGRN_HEADS = 8
HGRN_DK = 128
HGRN_DV = 128
HGRN_CHUNK = 128
HGRN_HEADS_PER_STEP = 2
HGRN_SAFE_LOG2_SPREAD = 96.0
NORM_SUB_ROWS = 256
MLP_SUB_ROWS = 256
MIX_SUB_ROWS = 128


def _cparams(*sem):
    return pltpu.CompilerParams(dimension_semantics=sem, vmem_limit_bytes=VMEM_LIMIT)


def _dot(a, b):
    return jnp.dot(a, b, preferred_element_type=F32)


def _dot_nt(a, b):
    return lax.dot_general(a, b, (((1,), (1,)), ((), ())), preferred_element_type=F32)


def _rms(x, g):
    return x * lax.rsqrt(jnp.mean(x * x, axis=-1, keepdims=True) + NORM_EPS) * g


def _mod_kernel(c_ref, w_ref, b_ref, o_ref):
    c = c_ref[...]
    cond = c * jax.nn.sigmoid(c)
    hi = cond.astype(BF16)
    lo = (cond - hi.astype(F32)).astype(BF16)
    w = w_ref[...].astype(BF16)
    o_ref[...] = _dot(hi, w) + _dot(lo, w) + b_ref[...]


def _mod(c, w_mod, b_mod, tn=1024):
    bsz, d = c.shape
    n = w_mod.shape[1]
    return pl.pallas_call(
        _mod_kernel,
        grid=(n // tn,),
        in_specs=[pl.BlockSpec((bsz, d), lambda j: (0, 0)),
                  pl.BlockSpec((d, tn), lambda j: (0, j)),
                  pl.BlockSpec((1, tn), lambda j: (0, j))],
        out_specs=pl.BlockSpec((bsz, tn), lambda j: (0, j)),
        out_shape=jax.ShapeDtypeStruct((bsz, n), F32),
        compiler_params=_cparams("parallel"),
        name="mod",
    )(c, w_mod, b_mod.reshape(1, n))


def _in_proj_kernel(x_ref, mod_ref, g_ref, wa_ref, wb_ref, oa_ref, ob_ref, h_ref):
    @pl.when(pl.program_id(2) == 0)
    def _():
        gain = g_ref[...] * (1.0 + mod_ref[1:2, :])
        for r in range(0, x_ref.shape[0], NORM_SUB_ROWS):
            t = slice(r, r + NORM_SUB_ROWS)
            h = (_rms(x_ref[t, :], gain) + mod_ref[0:1, :]).astype(BF16)
            h_ref[t, :] = h
            oa_ref[t, :] = _dot_nt(h, wa_ref[...]).astype(oa_ref.dtype)

    y = _dot_nt(h_ref[...], wb_ref[...]).astype(ob_ref.dtype)
    for c in range(ob_ref.shape[0]):
        ob_ref[c] = y[:, c * LANE:(c + 1) * LANE]


def _in_proj(x, mod3, g, wt, head, tm=1024, tn=1280):
    bsz, s, d = x.shape
    na, nb = MLA_LAT_PAD, wt.shape[0] - head
    row_align = int(np.gcd(head, tn))
    assert row_align % 16 == 0
    return pl.pallas_call(
        _in_proj_kernel,
        grid=(bsz, s // tm, nb // tn),
        in_specs=[pl.BlockSpec((None, tm, d), lambda b, i, j: (b, i, 0)),
                  pl.BlockSpec((None, N_MOD, d), lambda b, i, j: (b, 0, 0)),
                  pl.BlockSpec((1, d), lambda b, i, j: (0, 0)),
                  pl.BlockSpec((pl.Element(na), pl.Element(d)), lambda b, i, j: (0, 0)),
                  pl.BlockSpec((pl.Element(tn), pl.Element(d)),
                               lambda b, i, j: (pl.multiple_of(head + j * tn, row_align), 0))],
        out_specs=[pl.BlockSpec((None, tm, na), lambda b, i, j: (b, i, 0)),
                   pl.BlockSpec((None, tn // LANE, tm, LANE), lambda b, i, j: (b, j, i, 0))],
        out_shape=[jax.ShapeDtypeStruct((bsz, s, na), BF16),
                   jax.ShapeDtypeStruct((bsz, nb // LANE, s, LANE), BF16)],
        scratch_shapes=[pltpu.VMEM((tm, d), BF16)],
        compiler_params=_cparams("parallel", "parallel", "arbitrary"),
        name="in_proj",
    )(x, mod3, g, wt, wt)


def _rope_tile(t, cos_m, sin_lo, sin_hi):
    half = MLA_ROPE // 2
    return (t * cos_m + pltpu.roll(t, LANE - half, 1) * sin_lo
            + pltpu.roll(t, half, 1) * sin_hi)


def _rope_tables(pos_ref, invf_ref, tab_ref):
    half = MLA_ROPE // 2
    rows = pos_ref.shape[0]
    pos = pos_ref[...].astype(F32)
    lane = lax.broadcasted_iota(jnp.int32, (rows, LANE), 1)
    spread = pos[:, ROPE_PACK - 1:ROPE_PACK]
    for g in range(ROPE_PACK - 2, -1, -1):
        spread = jnp.where(lane < (g + 1) * half, pos[:, g:g + 1], spread)
    ang = spread * invf_ref[...]
    cos, sin = jnp.cos(ang), jnp.sin(ang)
    for g in range(ROPE_PACK):
        c = pltpu.roll(cos, LANE - g * half, 1) if g else cos
        s = pltpu.roll(sin, LANE - g * half, 1) if g else sin
        in_lo = lane < half
        in_hi = (lane >= half) & (lane < MLA_ROPE)
        dst = pl.ds(g, rows, stride=ROPE_PACK)
        tab_ref[0, dst, :] = jnp.where(in_lo, c, jnp.where(in_hi, pltpu.roll(c, half, 1), 0.0))
        tab_ref[1, dst, :] = jnp.where(in_lo, -s, 0.0)
        tab_ref[2, dst, :] = jnp.where(in_hi, pltpu.roll(s, half, 1), 0.0)


def _mla_prep_kernel(p_ref, pos_ref, invf_ref, qg_ref, kvg_ref, wq_ref, wk_ref, wvt_ref,
                     q_ref, k_ref, vt_ref, tab_ref, *, q_rank, kv_rank, scale):
    _rope_tables(pos_ref, invf_ref, tab_ref)
    cos_m, sin_lo, sin_hi = tab_ref[0], tab_ref[1], tab_ref[2]

    q_lat = p_ref[:, 0:q_rank].astype(F32)
    q = _dot(_rms(q_lat, qg_ref[...]).astype(BF16), wq_ref[...])
    kv_lat = p_ref[:, q_rank:q_rank + kv_rank].astype(F32)
    kv_n = _rms(kv_lat, kvg_ref[...]).astype(BF16)
    k_nope = _dot(kv_n, wk_ref[...])
    vt_ref[...] = _dot_nt(wvt_ref[...], kv_n).astype(BF16)
    kr_off = q_rank + kv_rank
    k_rope = _rope_tile(p_ref[:, kr_off:kr_off + LANE].astype(F32),
                        cos_m, sin_lo, sin_hi).astype(BF16)
    for h in range(MLA_HEADS):
        o = h * MLA_QK_PAD
        q_ref[:, o:o + MLA_NOPE] = (q[:, o:o + MLA_NOPE] * scale).astype(BF16)
        qr = _rope_tile(q[:, o + MLA_NOPE:o + MLA_QK_PAD], cos_m, sin_lo, sin_hi)
        q_ref[:, o + MLA_NOPE:o + MLA_QK_PAD] = (qr * scale).astype(BF16)
        k_ref[:, o:o + MLA_NOPE] = k_nope[:, h * MLA_NOPE:(h + 1) * MLA_NOPE].astype(BF16)
        k_ref[:, o + MLA_NOPE:o + MLA_QK_PAD] = k_rope


def _mla_prep(proj, positions, invf, qg, kvg, wq, wk, wvt, q_rank, kv_rank, tm=512):
    bsz, s, blk0 = proj.shape
    hq = MLA_HEADS * MLA_QK_PAD
    hv = MLA_HEADS * MLA_V
    scale = float((MLA_NOPE + MLA_ROPE) ** -0.5 * np.log2(np.e))
    kern = functools.partial(_mla_prep_kernel, q_rank=q_rank, kv_rank=kv_rank, scale=scale)

    def const(a):
        return pl.BlockSpec(a.shape, lambda b, i: (0, 0))

    return pl.pallas_call(
        kern,
        grid=(bsz, s // tm),
        in_specs=[pl.BlockSpec((None, tm, blk0), lambda b, i: (b, i, 0)),
                  pl.BlockSpec((None, tm // ROPE_PACK, ROPE_PACK), lambda b, i: (b, i, 0)),
                  const(invf), const(qg), const(kvg), const(wq), const(wk), const(wvt)],
        out_specs=[pl.BlockSpec((None, tm, hq), lambda b, i: (b, i, 0)),
                   pl.BlockSpec((None, tm, hq), lambda b, i: (b, i, 0)),
                   pl.BlockSpec((None, hv, tm), lambda b, i: (b, 0, i))],
        out_shape=[jax.ShapeDtypeStruct((bsz, s, hq), BF16),
                   jax.ShapeDtypeStruct((bsz, s, hq), BF16),
                   jax.ShapeDtypeStruct((bsz, hv, s), BF16)],
        scratch_shapes=[pltpu.VMEM((3, tm, LANE), F32)],
        compiler_params=_cparams("parallel", "parallel"),
        name="mla_prep",
    )(proj, positions.reshape(bsz, s // ROPE_PACK, ROPE_PACK), invf, qg, kvg, wq, wk, wvt)


def _attn_kernel(*refs, tq, tk, n_cast):
    q_ref, k_ref, vt_ref = refs[:3]
    o_ref = refs[3 + n_cast]
    sa_ref, sb_ref = refs[-2:]
    for src, dst in zip(refs[3:3 + n_cast], refs[4 + n_cast:4 + 2 * n_cast]):
        dst[...] = src[...].astype(dst.dtype)
    s_len = k_ref.shape[0]
    n_kv = s_len // tk
    n_q = s_len // tq
    sub = 8

    def scores(i, s_ref):
        q = q_ref[pl.ds(pl.multiple_of(i * tq, tq), tq), :]
        m8 = jnp.full((sub, tq), -jnp.inf, F32)
        for j in range(n_kv):
            st = _dot_nt(k_ref[j * tk:(j + 1) * tk, :], q)
            s_ref[j * tk:(j + 1) * tk, :] = st
            m8 = jnp.maximum(m8, jnp.max(st.reshape(tk // sub, sub, tq), axis=0))
            yield
        return jnp.broadcast_to(jnp.max(m8, axis=0, keepdims=True), (sub, tq))

    def values(i, s_ref, m8):
        l8 = jnp.zeros((sub, tq), F32)
        acc = jnp.zeros((MLA_V, tq), F32)
        for j in range(n_kv):
            st = s_ref[j * tk:(j + 1) * tk, :].reshape(tk // sub, sub, tq)
            p = jnp.exp2(st - m8[None])
            l8 = l8 + jnp.sum(p, axis=0)
            acc = acc + _dot(vt_ref[:, j * tk:(j + 1) * tk],
                             p.reshape(tk, tq).astype(BF16))
            yield
        l = jnp.sum(l8, axis=0, keepdims=True)
        o_ref[pl.ds(pl.multiple_of(i * tq, tq), tq), :] = (acc / l).T.astype(o_ref.dtype)

    def run(*gens):
        result = None
        live = list(gens)
        while live:
            for g in list(live):
                try:
                    next(g)
                except StopIteration as stop:
                    if g is gens[0]:
                        result = stop.value
                    live.remove(g)
        return result

    bufs = (sa_ref, sb_ref)
    group = ATTN_TILES_PER_TRIP

    def tiles(first, count, m):
        for u in range(count):
            m = run(scores(first + u + 1, bufs[(u + 1) % 2]),
                    values(first + u, bufs[u % 2], m))
        return m

    m = lax.fori_loop(0, n_q // group - 1, lambda k, m: tiles(group * k, group, m),
                      run(scores(0, bufs[0])))
    m = tiles(n_q - group, group - 1, m)
    run(values(n_q - 1, bufs[(group - 1) % 2], m))


def _attention(q, k, vt, f32_weights, tq=256, tk=2048):
    bsz, s, _ = q.shape
    steps = bsz * MLA_HEADS
    kern = functools.partial(_attn_kernel, tq=tq, tk=tk, n_cast=len(f32_weights))

    def slab(w):
        return pl.BlockSpec((w.shape[0] // steps, w.shape[1]),
                            lambda b, h: (b * MLA_HEADS + h, 0))

    assert all(w.shape[0] % (8 * steps) == 0 for w in f32_weights)
    out = pl.pallas_call(
        kern,
        grid=(bsz, MLA_HEADS),
        in_specs=[pl.BlockSpec((None, s, MLA_QK_PAD), lambda b, h: (b, 0, h)),
                  pl.BlockSpec((None, s, MLA_QK_PAD), lambda b, h: (b, 0, h)),
                  pl.BlockSpec((None, MLA_V, s), lambda b, h: (b, h, 0))]
                 + [slab(w) for w in f32_weights],
        out_specs=[pl.BlockSpec((None, s, MLA_V), lambda b, h: (b, 0, h))]
                  + [slab(w) for w in f32_weights],
        out_shape=[jax.ShapeDtypeStruct((bsz, s, MLA_HEADS * MLA_V), BF16)]
                  + [jax.ShapeDtypeStruct(w.shape, BF16) for w in f32_weights],
        scratch_shapes=[pltpu.VMEM((s, tq), F32), pltpu.VMEM((s, tq), F32)],
        compiler_params=_cparams("parallel", "parallel"),
        name="attention",
    )(q, k, vt, *f32_weights)
    return out[0], out[1:]


def _run_staggered(gens):
    live = list(enumerate(gens))
    tick = 0
    while live:
        for delay, g in list(live):
            if tick >= delay and next(g, StopIteration) is StopIteration:
                live.remove((delay, g))
        tick += 1


def _hgrn_exact_block(q, k, b, v_ref, qs, states, o_ref, k_s, b_s, v_s, *, forward):
    n, c, _ = q.shape
    k_s[...] = k.reshape(n * c, HGRN_DK)
    b_s[...] = b.reshape(n * c, HGRN_DK)
    v_s[...] = v_ref[...].astype(F32)
    t_idx = lax.broadcasted_iota(jnp.int32, (c, 1), 0)
    for j in range(n):
        def key_row(s, acc, j=j):
            row = pl.ds(j * c + s, 1)
            w = jnp.exp2(jnp.minimum(b[j] - b_s[row, :], 0.0))
            seen = (t_idx >= s) if forward else (t_idx <= s)
            a_col = jnp.sum(jnp.where(seen, q[j] * w * k_s[row, :], 0.0),
                            axis=1, keepdims=True)
            return acc + a_col * v_s[row, :]

        intra = lax.fori_loop(0, c, key_row, jnp.zeros((c, HGRN_DV), F32))
        inter = _dot_nt(qs[j], states[j])
        o_ref[j * c:(j + 1) * c, :] = (inter + intra).astype(o_ref.dtype)


def _hgrn_direction(q_ref, z_ref, v_ref, lb, st_ref, o_ref, exact_scratch, *, forward, lag):
    c = HGRN_CHUNK
    n = q_ref.shape[0] // c
    row = lax.broadcasted_iota(jnp.int32, (c, c), 0)
    col = lax.broadcasted_iota(jnp.int32, (c, c), 1)
    causal = (col <= row) if forward else (col >= row)
    tri = causal.astype(BF16)
    edge = c - 1 if forward else 0

    chunks = [slice(j * c, (j + 1) * c) for j in range(n)]
    f = lb + (1.0 - lb) * jax.nn.sigmoid(z_ref[...].astype(F32))
    k = (1.0 - f).reshape(n, c, HGRN_DK)
    log_f = jnp.log2(f).astype(BF16)
    q = q_ref[...].reshape(n, c, HGRN_DK)
    vt = v_ref[...].astype(F32).T.astype(BF16)
    yield
    b = jnp.concatenate([_dot(tri, log_f[r]) for r in chunks], axis=0)
    yield
    b = b.reshape(n, c, HGRN_DK)
    b_edge = b[:, edge:edge + 1]
    kb = k.astype(BF16)
    qs = q * jnp.exp2(b).astype(BF16)
    ki = kb * jnp.exp2(-b).astype(BF16)
    kd = kb * jnp.exp2(b_edge - b).astype(BF16)
    decay = jnp.exp2(b_edge)
    spread = jnp.max(-b_edge)
    yield
    a = [_dot_nt(qs[j], ki[j]) for j in range(n)]
    upd = [_dot(vt[:, chunks[j]], kd[j]) for j in range(n)]
    yield
    a = [jnp.where(causal, x, 0.0).astype(BF16) for x in a]
    st = st_ref[...]
    states = [None] * n
    for j in (range(n) if forward else reversed(range(n))):
        states[j] = st.astype(BF16)
        st = st * decay[j] + upd[j]
    st_ref[...] = st
    yield
    outs = [_dot_nt(jnp.concatenate([a[j], qs[j]], axis=1),
                    jnp.concatenate([vt[:, chunks[j]], states[j]], axis=1))
            for j in range(n)]
    o_ref[...] = jnp.concatenate(outs, axis=0).astype(o_ref.dtype)
    yield
    for _ in range(lag):
        yield

    @pl.when(spread > HGRN_SAFE_LOG2_SPREAD)
    def _():
        _hgrn_exact_block(q.astype(F32), k, b, v_ref, qs, states, o_ref, *exact_scratch,
                          forward=forward)
    yield


def _hgrn_kernel(qf_ref, zf_ref, vf_ref, qb_ref, zb_ref, vb_ref, lbf_ref, lbb_ref,
                 of_ref, ob_ref, stf_ref, stb_ref, k_s, b_s, v_s):
    @pl.when(pl.program_id(2) == 0)
    def _():
        stf_ref[...] = jnp.zeros_like(stf_ref)
        stb_ref[...] = jnp.zeros_like(stb_ref)

    def lower_bound(ref, head):
        l = ref[:, head * LANE:(head + 1) * LANE]
        e = jnp.exp(l - jnp.max(l, axis=0, keepdims=True))
        return e[0:1, :] / jnp.sum(e, axis=0, keepdims=True)

    scans = []
    lag = 2 * qf_ref.shape[0] - 1
    for h in range(qf_ref.shape[0]):
        scans.append(_hgrn_direction(qf_ref.at[h], zf_ref.at[h], vf_ref.at[h],
                                     lower_bound(lbf_ref, h), stf_ref.at[h], of_ref.at[h],
                                     (k_s, b_s, v_s), forward=True, lag=lag))
        scans.append(_hgrn_direction(qb_ref.at[h], zb_ref.at[h], vb_ref.at[h],
                                     lower_bound(lbb_ref, h), stb_ref.at[h], ob_ref.at[h],
                                     (k_s, b_s, v_s), forward=False, lag=lag))
    _run_staggered(scans)


def _hgrn(proj, lbf_logits, lbb_logits, col_q, col_ff, col_fb, col_v, tb=1024):
    bsz, _, s, _ = proj.shape
    nb = s // tb
    hps = HGRN_HEADS_PER_STEP

    def fwd(col):
        return pl.BlockSpec((None, hps, tb, LANE), lambda b, h, i: (b, col // hps + h, i, 0))

    def bwd(col):
        return pl.BlockSpec((None, hps, tb, LANE),
                            lambda b, h, i: (b, col // hps + h, nb - 1 - i, 0))

    assert all(col % hps == 0 for col in (col_q, col_ff, col_fb, col_v))
    lb_spec = pl.BlockSpec((lbf_logits.shape[0], hps * LANE), lambda b, h, i: (0, h))
    out_shape = jax.ShapeDtypeStruct((bsz, HGRN_HEADS, s, HGRN_DV), BF16)
    return pl.pallas_call(
        _hgrn_kernel,
        grid=(bsz, HGRN_HEADS // hps, nb),
        in_specs=[fwd(col_q), fwd(col_ff), fwd(col_v),
                  bwd(col_q), bwd(col_fb), bwd(col_v), lb_spec, lb_spec],
        out_specs=[fwd(0), bwd(0)],
        out_shape=[out_shape, out_shape],
        scratch_shapes=[pltpu.VMEM((hps, HGRN_DV, HGRN_DK), F32),
                        pltpu.VMEM((hps, HGRN_DV, HGRN_DK), F32),
                        pltpu.VMEM((tb, HGRN_DK), F32), pltpu.VMEM((tb, HGRN_DK), F32),
                        pltpu.VMEM((tb, HGRN_DV), F32)],
        compiler_params=_cparams("parallel", "parallel", "arbitrary"),
        name="hgrn",
    )(proj, proj, proj, proj, proj, proj, lbf_logits, lbb_logits)


def _mix_out_kernel(attn_ref, of_ref, ob_ref, hg_ref, hgn_ref, w_ref, x_ref, mod_ref,
                    postg_ref, preg_ref, x1_ref, h2_ref):
    na = attn_ref.shape[-1]
    tm = attn_ref.shape[0]
    tiles = [slice(r, r + MIX_SUB_ROWS) for r in range(0, tm, MIX_SUB_ROWS)]
    recs = []
    for t in tiles:
        rec = []
        for h in range(HGRN_HEADS):
            o = of_ref[h, t, :].astype(F32) + ob_ref[h, t, :].astype(F32)
            hg = hg_ref[h, t, :].astype(F32)
            rec.append((_rms(o, hgn_ref[...]) * (hg * jax.nn.sigmoid(hg))).astype(BF16))
        recs.append(jnp.concatenate(rec, axis=-1))
    mixes = [_dot(attn_ref[t, :], w_ref[0:na, :]) + _dot(rec, w_ref[na:, :])
             for t, rec in zip(tiles, recs)]
    post_gain = mod_ref[2:3, :] * postg_ref[...]
    pre_gain = preg_ref[...] * (1.0 + mod_ref[4:5, :])
    for t, mix in zip(tiles, mixes):
        x1 = x_ref[t, :] + _rms(mix, post_gain)
        x1_ref[t, :] = x1
        h2_ref[t, :] = (_rms(x1, pre_gain) + mod_ref[3:4, :]).astype(BF16)


def _mix_out(attn, o_f, o_b, proj, col_hg, hgn_g, w_out, x, mod3, post_g, pre_g, tm=512):
    bsz, s, d = x.shape
    na = attn.shape[-1]
    heads = o_f.shape[1]

    def row(width):
        return pl.BlockSpec((None, tm, width), lambda b, i: (b, i, 0))

    def per_head(first):
        return pl.BlockSpec((None, heads, tm, LANE), lambda b, i: (b, first // heads, i, 0))

    def const(shape):
        return pl.BlockSpec(shape, lambda b, i: (0,) * len(shape))

    assert col_hg % heads == 0
    return pl.pallas_call(
        _mix_out_kernel,
        grid=(bsz, s // tm),
        in_specs=[row(na), per_head(0), per_head(0), per_head(col_hg),
                  const((1, HGRN_DV)), const(w_out.shape), row(d),
                  pl.BlockSpec((None, N_MOD, d), lambda b, i: (b, 0, 0)),
                  const((1, d)), const((1, d))],
        out_specs=[row(d), row(d)],
        out_shape=[jax.ShapeDtypeStruct((bsz, s, d), F32),
                   jax.ShapeDtypeStruct((bsz, s, d), BF16)],
        compiler_params=_cparams("parallel", "parallel"),
        name="mix_out",
    )(attn, o_f, o_b, proj, hgn_g, w_out, x, mod3, post_g, pre_g)


def _mlp_kernel(h_ref, wu_ref, wd_ref, x1_ref, mod_ref, g_ref, o_ref):
    j = pl.program_id(2)
    last = pl.num_programs(2) - 1

    @pl.when(j == 0)
    def _():
        o_ref[...] = jnp.zeros_like(o_ref)

    def activation():
        u = jnp.maximum(_dot(h_ref[...], wu_ref[...]), 0.0)
        return (u * u).astype(BF16)

    @pl.when(j < last)
    def _():
        o_ref[...] += _dot(activation(), wd_ref[...])

    @pl.when(j == last)
    def _():
        a = activation()
        gain = mod_ref[5:6, :] * g_ref[...]
        for r in range(0, o_ref.shape[0], MLP_SUB_ROWS):
            t = slice(r, r + MLP_SUB_ROWS)
            y = o_ref[t, :] + _dot(a[t, :], wd_ref[...])
            o_ref[t, :] = x1_ref[t, :] + _rms(y, gain)


def _mlp(h2, w_up, w_down, x1, mod3, g, tm=512, tf=1024):
    bsz, s, d = x1.shape
    dff = w_up.shape[1]
    return pl.pallas_call(
        _mlp_kernel,
        grid=(bsz, s // tm, dff // tf),
        in_specs=[pl.BlockSpec((None, tm, d), lambda b, i, j: (b, i, 0)),
                  pl.BlockSpec((d, tf), lambda b, i, j: (0, j)),
                  pl.BlockSpec((tf, d), lambda b, i, j: (j, 0)),
                  pl.BlockSpec((None, tm, d), lambda b, i, j: (b, i, 0)),
                  pl.BlockSpec((None, N_MOD, d), lambda b, i, j: (b, 0, 0)),
                  pl.BlockSpec((1, d), lambda b, i, j: (0, 0))],
        out_specs=pl.BlockSpec((None, tm, d), lambda b, i, j: (b, i, 0)),
        out_shape=jax.ShapeDtypeStruct((bsz, s, d), F32),
        compiler_params=_cparams("parallel", "parallel", "arbitrary"),
        name="mlp",
    )(h2, w_up, w_down, x1, mod3, g)


def _pack_w_uq(w):
    r = w.shape[0]
    w = w.reshape(r, MLA_HEADS, MLA_NOPE + MLA_ROPE)
    pad = jnp.zeros((r, MLA_HEADS, MLA_QK_PAD - MLA_NOPE - MLA_ROPE), w.dtype)
    return jnp.concatenate([w, pad], axis=-1).reshape(r, MLA_HEADS * MLA_QK_PAD).astype(BF16)


def _pack_w_ukv(w):
    r = w.shape[0]
    w = w.reshape(r, MLA_HEADS, MLA_NOPE + MLA_V)
    wk = w[:, :, :MLA_NOPE].reshape(r, -1).astype(BF16)
    wvt = w[:, :, MLA_NOPE:].reshape(r, -1).T.astype(BF16)
    return wk, wvt


def _rope_inv_freq():
    half = MLA_ROPE // 2
    f = ROPE_BASE ** (-jnp.arange(half, dtype=F32) / half)
    return jnp.tile(f, ROPE_PACK).reshape(1, LANE)


def kernel(x, c, positions, w_mod, b_mod, pre_mix_g, post_mix_g, pre_mlp_g, post_mlp_g,
           w_in, q_norm_g, kv_norm_g, w_uq, w_ukv, hgrn_norm_g, hgrn_lb_logits_fwd,
           hgrn_lb_logits_bwd, w_out, w_up, w_down):
    depth = w_mod.shape[0]
    assert depth == 1, "lower-bound slot selection is written for a single layer"
    bsz, s, d = x.shape
    q_rank = q_norm_g.shape[-1]
    kv_rank = kv_norm_g.shape[-1]
    invf = _rope_inv_freq()
    for layer in range(depth):
        mod3 = _mod(c, w_mod[layer], b_mod[layer]).reshape(bsz, N_MOD, d)
        lat, proj = _in_proj(x, mod3, pre_mix_g[layer].reshape(1, d),
                             w_in[layer].T.astype(BF16), q_rank + kv_rank + MLA_ROPE)
        wk, wvt = _pack_w_ukv(w_ukv[layer])
        q, k, vt = _mla_prep(lat, positions, invf, q_norm_g[layer].reshape(1, -1),
                             kv_norm_g[layer].reshape(1, -1), _pack_w_uq(w_uq[layer]),
                             wk, wvt, q_rank, kv_rank)
        attn, (w_out_b, w_up_b, w_down_b) = _attention(
            q, k, vt, (w_out[layer], w_up[layer], w_down[layer]))
        o_f, o_b = _hgrn(proj, hgrn_lb_logits_fwd, hgrn_lb_logits_bwd,
                         col_q=0, col_ff=8, col_fb=16, col_v=24)
        x1, h2 = _mix_out(attn, o_f, o_b, proj, 32, hgrn_norm_g[layer].reshape(1, -1),
                          w_out_b, x, mod3,
                          post_mix_g[layer].reshape(1, d), pre_mlp_g[layer].reshape(1, d))
        x = _mlp(h2, w_up_b, w_down_b, x1, mod3, post_mlp_g[layer].reshape(1, d))
    return x
```

```python
import functools

import jax
import jax.numpy as jnp
import numpy as np
from jax import lax
from jax.experimental import pallas as pl
from jax.experimental.pallas import tpu as pltpu

F32 = jnp.float32
BF16 = jnp.bfloat16

NORM_EPS = 1e-6
ROPE_BASE = 10000.0
N_MOD = 6
LANE = 128
VMEM_LIMIT = 56 * 1024 * 1024

MLA_HEADS = 8
MLA_NOPE = 128
MLA_ROPE = 64
MLA_V = 128
MLA_QK_PAD = 256
ROPE_PACK = LANE // (MLA_ROPE // 2)
ATTN_TILES_PER_TRIP = 4
MLA_LAT_PAD = 1024
HGRN_HEADS = 8
HGRN_DK = 128
HGRN_DV = 128
HGRN_CHUNK = 128
HGRN_HEADS_PER_STEP = 2
HGRN_SAFE_LOG2_SPREAD = 96.0
NORM_SUB_ROWS = 256
MLP_SUB_ROWS = 256
MIX_SUB_ROWS = 128


def _cparams(*sem):
    return pltpu.CompilerParams(dimension_semantics=sem, vmem_limit_bytes=VMEM_LIMIT)


def _dot(a, b):
    return jnp.dot(a, b, preferred_element_type=F32)


def _dot_nt(a, b):
    return lax.dot_general(a, b, (((1,), (1,)), ((), ())), preferred_element_type=F32)


def _rms(x, g):
    return x * lax.rsqrt(jnp.mean(x * x, axis=-1, keepdims=True) + NORM_EPS) * g


def _mod_kernel(c_ref, w_ref, b_ref, o_ref):
    c = c_ref[...]
    cond = c * jax.nn.sigmoid(c)
    hi = cond.astype(BF16)
    lo = (cond - hi.astype(F32)).astype(BF16)
    w = w_ref[...].astype(BF16)
    o_ref[...] = _dot(hi, w) + _dot(lo, w) + b_ref[...]


def _mod(c, w_mod, b_mod, tn=1024):
    bsz, d = c.shape
    n = w_mod.shape[1]
    return pl.pallas_call(
        _mod_kernel,
        grid=(n // tn,),
        in_specs=[pl.BlockSpec((bsz, d), lambda j: (0, 0)),
                  pl.BlockSpec((d, tn), lambda j: (0, j)),
                  pl.BlockSpec((1, tn), lambda j: (0, j))],
        out_specs=pl.BlockSpec((bsz, tn), lambda j: (0, j)),
        out_shape=jax.ShapeDtypeStruct((bsz, n), F32),
        compiler_params=_cparams("parallel"),
        name="mod",
    )(c, w_mod, b_mod.reshape(1, n))


def _in_proj_kernel(x_ref, mod_ref, g_ref, wa_ref, wb_ref, oa_ref, ob_ref, h_ref):
    @pl.when(pl.program_id(2) == 0)
    def _():
        gain = g_ref[...] * (1.0 + mod_ref[1:2, :])
        for r in range(0, x_ref.shape[0], NORM_SUB_ROWS):
            t = slice(r, r + NORM_SUB_ROWS)
            h = (_rms(x_ref[t, :], gain) + mod_ref[0:1, :]).astype(BF16)
            h_ref[t, :] = h
            oa_ref[t, :] = _dot_nt(h, wa_ref[...]).astype(oa_ref.dtype)

    y = _dot_nt(h_ref[...], wb_ref[...]).astype(ob_ref.dtype)
    for c in range(ob_ref.shape[0]):
        ob_ref[c] = y[:, c * LANE:(c + 1) * LANE]


def _in_proj(x, mod3, g, wt, head, tm=1024, tn=1280):
    bsz, s, d = x.shape
    na, nb = MLA_LAT_PAD, wt.shape[0] - head
    row_align = int(np.gcd(head, tn))
    assert row_align % 16 == 0
    return pl.pallas_call(
        _in_proj_kernel,
        grid=(bsz, s // tm, nb // tn),
        in_specs=[pl.BlockSpec((None, tm, d), lambda b, i, j: (b, i, 0)),
                  pl.BlockSpec((None, N_MOD, d), lambda b, i, j: (b, 0, 0)),
                  pl.BlockSpec((1, d), lambda b, i, j: (0, 0)),
                  pl.BlockSpec((pl.Element(na), pl.Element(d)), lambda b, i, j: (0, 0)),
                  pl.BlockSpec((pl.Element(tn), pl.Element(d)),
                               lambda b, i, j: (pl.multiple_of(head + j * tn, row_align), 0))],
        out_specs=[pl.BlockSpec((None, tm, na), lambda b, i, j: (b, i, 0)),
                   pl.BlockSpec((None, tn // LANE, tm, LANE), lambda b, i, j: (b, j, i, 0))],
        out_shape=[jax.ShapeDtypeStruct((bsz, s, na), BF16),
                   jax.ShapeDtypeStruct((bsz, nb // LANE, s, LANE), BF16)],
        scratch_shapes=[pltpu.VMEM((tm, d), BF16)],
        compiler_params=_cparams("parallel", "parallel", "arbitrary"),
        name="in_proj",
    )(x, mod3, g, wt, wt)


def _rope_tile(t, cos_m, sin_lo, sin_hi):
    half = MLA_ROPE // 2
    return (t * cos_m + pltpu.roll(t, LANE - half, 1) * sin_lo
            + pltpu.roll(t, half, 1) * sin_hi)


def _rope_tables(pos_ref, invf_ref, tab_ref):
    half = MLA_ROPE // 2
    rows = pos_ref.shape[0]
    pos = pos_ref[...].astype(F32)
    lane = lax.broadcasted_iota(jnp.int32, (rows, LANE), 1)
    spread = pos[:, ROPE_PACK - 1:ROPE_PACK]
    for g in range(ROPE_PACK - 2, -1, -1):
        spread = jnp.where(lane < (g + 1) * half, pos[:, g:g + 1], spread)
    ang = spread * invf_ref[...]
    cos, sin = jnp.cos(ang), jnp.sin(ang)
    for g in range(ROPE_PACK):
        c = pltpu.roll(cos, LANE - g * half, 1) if g else cos
        s = pltpu.roll(sin, LANE - g * half, 1) if g else sin
        in_lo = lane < half
        in_hi = (lane >= half) & (lane < MLA_ROPE)
        dst = pl.ds(g, rows, stride=ROPE_PACK)
        tab_ref[0, dst, :] = jnp.where(in_lo, c, jnp.where(in_hi, pltpu.roll(c, half, 1), 0.0))
        tab_ref[1, dst, :] = jnp.where(in_lo, -s, 0.0)
        tab_ref[2, dst, :] = jnp.where(in_hi, pltpu.roll(s, half, 1), 0.0)


def _mla_prep_kernel(p_ref, pos_ref, invf_ref, qg_ref, kvg_ref, wq_ref, wk_ref, wvt_ref,
                     q_ref, k_ref, vt_ref, tab_ref, *, q_rank, kv_rank, scale):
    _rope_tables(pos_ref, invf_ref, tab_ref)
    cos_m, sin_lo, sin_hi = tab_ref[0], tab_ref[1], tab_ref[2]

    q_lat = p_ref[:, 0:q_rank].astype(F32)
    q = _dot(_rms(q_lat, qg_ref[...]).astype(BF16), wq_ref[...])
    kv_lat = p_ref[:, q_rank:q_rank + kv_rank].astype(F32)
    kv_n = _rms(kv_lat, kvg_ref[...]).astype(BF16)
    k_nope = _dot(kv_n, wk_ref[...])
    vt_ref[...] = _dot_nt(wvt_ref[...], kv_n).astype(BF16)
    kr_off = q_rank + kv_rank
    k_rope = _rope_tile(p_ref[:, kr_off:kr_off + LANE].astype(F32),
                        cos_m, sin_lo, sin_hi).astype(BF16)
    for h in range(MLA_HEADS):
        o = h * MLA_QK_PAD
        q_ref[:, o:o + MLA_NOPE] = (q[:, o:o + MLA_NOPE] * scale).astype(BF16)
        qr = _rope_tile(q[:, o + MLA_NOPE:o + MLA_QK_PAD], cos_m, sin_lo, sin_hi)
        q_ref[:, o + MLA_NOPE:o + MLA_QK_PAD] = (qr * scale).astype(BF16)
        k_ref[:, o:o + MLA_NOPE] = k_nope[:, h * MLA_NOPE:(h + 1) * MLA_NOPE].astype(BF16)
        k_ref[:, o + MLA_NOPE:o + MLA_QK_PAD] = k_rope


def _mla_prep(proj, positions, invf, qg, kvg, wq, wk, wvt, q_rank, kv_rank, tm=512):
    bsz, s, blk0 = proj.shape
    hq = MLA_HEADS * MLA_QK_PAD
    hv = MLA_HEADS * MLA_V
    scale = float((MLA_NOPE + MLA_ROPE) ** -0.5 * np.log2(np.e))
    kern = functools.partial(_mla_prep_kernel, q_rank=q_rank, kv_rank=kv_rank, scale=scale)

    def const(a):
        return pl.BlockSpec(a.shape, lambda b, i: (0, 0))

    return pl.pallas_call(
        kern,
        grid=(bsz, s // tm),
        in_specs=[pl.BlockSpec((None, tm, blk0), lambda b, i: (b, i, 0)),
                  pl.BlockSpec((None, tm // ROPE_PACK, ROPE_PACK), lambda b, i: (b, i, 0)),
                  const(invf), const(qg), const(kvg), const(wq), const(wk), const(wvt)],
        out_specs=[pl.BlockSpec((None, tm, hq), lambda b, i: (b, i, 0)),
                   pl.BlockSpec((None, tm, hq), lambda b, i: (b, i, 0)),
                   pl.BlockSpec((None, hv, tm), lambda b, i: (b, 0, i))],
        out_shape=[jax.ShapeDtypeStruct((bsz, s, hq), BF16),
                   jax.ShapeDtypeStruct((bsz, s, hq), BF16),
                   jax.ShapeDtypeStruct((bsz, hv, s), BF16)],
        scratch_shapes=[pltpu.VMEM((3, tm, LANE), F32)],
        compiler_params=_cparams("parallel", "parallel"),
        name="mla_prep",
    )(proj, positions.reshape(bsz, s // ROPE_PACK, ROPE_PACK), invf, qg, kvg, wq, wk, wvt)


def _attn_kernel(*refs, tq, tk, n_cast):
    q_ref, k_ref, vt_ref = refs[:3]
    o_ref = refs[3 + n_cast]
    sa_ref, sb_ref = refs[-2:]
    for src, dst in zip(refs[3:3 + n_cast], refs[4 + n_cast:4 + 2 * n_cast]):
        dst[...] = src[...].astype(dst.dtype)
    s_len = k_ref.shape[0]
    n_kv = s_len // tk
    n_q = s_len // tq
    sub = 8

    def scores(i, s_ref):
        q = q_ref[pl.ds(pl.multiple_of(i * tq, tq), tq), :]
        m8 = jnp.full((sub, tq), -jnp.inf, F32)
        for j in range(n_kv):
            st = _dot_nt(k_ref[j * tk:(j + 1) * tk, :], q)
            s_ref[j * tk:(j + 1) * tk, :] = st
            m8 = jnp.maximum(m8, jnp.max(st.reshape(tk // sub, sub, tq), axis=0))
            yield
        return jnp.broadcast_to(jnp.max(m8, axis=0, keepdims=True), (sub, tq))

    def values(i, s_ref, m8):
        l8 = jnp.zeros((sub, tq), F32)
        acc = jnp.zeros((MLA_V, tq), F32)
        for j in range(n_kv):
            st = s_ref[j * tk:(j + 1) * tk, :].reshape(tk // sub, sub, tq)
            p = jnp.exp2(st - m8[None])
            l8 = l8 + jnp.sum(p, axis=0)
            acc = acc + _dot(vt_ref[:, j * tk:(j + 1) * tk],
                             p.reshape(tk, tq).astype(BF16))
            yield
        l = jnp.sum(l8, axis=0, keepdims=True)
        o_ref[pl.ds(pl.multiple_of(i * tq, tq), tq), :] = (acc / l).T.astype(o_ref.dtype)

    def run(*gens):
        result = None
        live = list(gens)
        while live:
            for g in list(live):
                try:
                    next(g)
                except StopIteration as stop:
                    if g is gens[0]:
                        result = stop.value
                    live.remove(g)
        return result

    bufs = (sa_ref, sb_ref)
    group = ATTN_TILES_PER_TRIP

    def tiles(first, count, m):
        for u in range(count):
            m = run(scores(first + u + 1, bufs[(u + 1) % 2]),
                    values(first + u, bufs[u % 2], m))
        return m

    m = lax.fori_loop(0, n_q // group - 1, lambda k, m: tiles(group * k, group, m),
                      run(scores(0, bufs[0])))
    m = tiles(n_q - group, group - 1, m)
    run(values(n_q - 1, bufs[(group - 1) % 2], m))


def _attention(q, k, vt, f32_weights, tq=256, tk=2048):
    bsz, s, _ = q.shape
    steps = bsz * MLA_HEADS
    kern = functools.partial(_attn_kernel, tq=tq, tk=tk, n_cast=len(f32_weights))

    def slab(w):
        return pl.BlockSpec((w.shape[0] // steps, w.shape[1]),
                            lambda b, h: (b * MLA_HEADS + h, 0))

    assert all(w.shape[0] % (8 * steps) == 0 for w in f32_weights)
    out = pl.pallas_call(
        kern,
        grid=(bsz, MLA_HEADS),
        in_specs=[pl.BlockSpec((None, s, MLA_QK_PAD), lambda b, h: (b, 0, h)),
                  pl.BlockSpec((None, s, MLA_QK_PAD), lambda b, h: (b, 0, h)),
                  pl.BlockSpec((None, MLA_V, s), lambda b, h: (b, h, 0))]
                 + [slab(w) for w in f32_weights],
        out_specs=[pl.BlockSpec((None, s, MLA_V), lambda b, h: (b, 0, h))]
                  + [slab(w) for w in f32_weights],
        out_shape=[jax.ShapeDtypeStruct((bsz, s, MLA_HEADS * MLA_V), BF16)]
                  + [jax.ShapeDtypeStruct(w.shape, BF16) for w in f32_weights],
        scratch_shapes=[pltpu.VMEM((s, tq), F32), pltpu.VMEM((s, tq), F32)],
        compiler_params=_cparams("parallel", "parallel"),
        name="attention",
    )(q, k, vt, *f32_weights)
    return out[0], out[1:]


def _run_staggered(gens):
    live = list(enumerate(gens))
    tick = 0
    while live:
        for delay, g in list(live):
            if tick >= delay and next(g, StopIteration) is StopIteration:
                live.remove((delay, g))
        tick += 1


def _hgrn_exact_block(q, k, b, v_ref, qs, states, o_ref, k_s, b_s, v_s, *, forward):
    n, c, _ = q.shape
    k_s[...] = k.reshape(n * c, HGRN_DK)
    b_s[...] = b.reshape(n * c, HGRN_DK)
    v_s[...] = v_ref[...].astype(F32)
    t_idx = lax.broadcasted_iota(jnp.int32, (c, 1), 0)
    for j in range(n):
        def key_row(s, acc, j=j):
            row = pl.ds(j * c + s, 1)
            w = jnp.exp2(jnp.minimum(b[j] - b_s[row, :], 0.0))
            seen = (t_idx >= s) if forward else (t_idx <= s)
            a_col = jnp.sum(jnp.where(seen, q[j] * w * k_s[row, :], 0.0),
                            axis=1, keepdims=True)
            return acc + a_col * v_s[row, :]

        intra = lax.fori_loop(0, c, key_row, jnp.zeros((c, HGRN_DV), F32))
        inter = _dot_nt(qs[j], states[j])
        o_ref[j * c:(j + 1) * c, :] = (inter + intra).astype(o_ref.dtype)


def _hgrn_direction(q_ref, z_ref, v_ref, lb, st_ref, o_ref, exact_scratch, *, forward, lag):
    c = HGRN_CHUNK
    n = q_ref.shape[0] // c
    row = lax.broadcasted_iota(jnp.int32, (c, c), 0)
    col = lax.broadcasted_iota(jnp.int32, (c, c), 1)
    causal = (col <= row) if forward else (col >= row)
    tri = causal.astype(BF16)
    edge = c - 1 if forward else 0

    chunks = [slice(j * c, (j + 1) * c) for j in range(n)]
    f = lb + (1.0 - lb) * jax.nn.sigmoid(z_ref[...].astype(F32))
    k = (1.0 - f).reshape(n, c, HGRN_DK)
    log_f = jnp.log2(f).astype(BF16)
    q = q_ref[...].reshape(n, c, HGRN_DK)
    vt = v_ref[...].astype(F32).T.astype(BF16)
    yield
    b = jnp.concatenate([_dot(tri, log_f[r]) for r in chunks], axis=0)
    yield
    b = b.reshape(n, c, HGRN_DK)
    b_edge = b[:, edge:edge + 1]
    h = c // 2
    near, far = (slice(0, h), slice(h, c)) if forward else (slice(h, c), slice(0, h))
    ref = h - 1 if forward else h
    b_ref = b[:, ref:ref + 1]
    kb = k.astype(BF16)
    q_near = q[:, near] * jnp.exp2(b[:, near]).astype(BF16)
    q_far = q[:, far] * jnp.exp2(b[:, far] - b_ref).astype(BF16)
    k_near = kb[:, near] * jnp.exp2(-b[:, near]).astype(BF16)
    k_ref = kb * jnp.exp2(b_ref - b).astype(BF16)
    kd = kb * jnp.exp2(b_edge - b).astype(BF16)
    qs_far = q_far * jnp.exp2(b_ref).astype(BF16)
    qs = jnp.concatenate([q_near, qs_far] if forward else [qs_far, q_near], axis=1)
    decay = jnp.exp2(b_edge)
    spread = jnp.maximum(jnp.max(-b_ref), jnp.max(b_ref - b_edge))
    yield
    a_near = [_dot_nt(q_near[j], k_near[j]) for j in range(n)]
    a_far = [_dot_nt(q_far[j], k_ref[j]) for j in range(n)]
    upd = [_dot(vt[:, chunks[j]], kd[j]) for j in range(n)]
    yield
    blank = jnp.zeros((h, h), F32)
    if forward:
        a = [jnp.concatenate([jnp.concatenate([an, blank], axis=1), af], axis=0)
             for an, af in zip(a_near, a_far)]
    else:
        a = [jnp.concatenate([af, jnp.concatenate([blank, an], axis=1)], axis=0)
             for an, af in zip(a_near, a_far)]
    a = [jnp.where(causal, x, 0.0).astype(BF16) for x in a]
    st = st_ref[...]
    states = [None] * n
    for j in (range(n) if forward else reversed(range(n))):
        states[j] = st.astype(BF16)
        st = st * decay[j] + upd[j]
    st_ref[...] = st
    yield
    outs = [_dot_nt(jnp.concatenate([a[j], qs[j]], axis=1),
                    jnp.concatenate([vt[:, chunks[j]], states[j]], axis=1))
            for j in range(n)]
    o_ref[...] = jnp.concatenate(outs, axis=0).astype(o_ref.dtype)
    yield
    for _ in range(lag):
        yield

    @pl.when(spread > HGRN_SAFE_LOG2_SPREAD)
    def _():
        _hgrn_exact_block(q.astype(F32), k, b, v_ref, qs, states, o_ref, *exact_scratch,
                          forward=forward)
    yield


def _hgrn_kernel(qf_ref, zf_ref, vf_ref, qb_ref, zb_ref, vb_ref, lbf_ref, lbb_ref,
                 of_ref, ob_ref, stf_ref, stb_ref, k_s, b_s, v_s):
    @pl.when(pl.program_id(2) == 0)
    def _():
        stf_ref[...] = jnp.zeros_like(stf_ref)
        stb_ref[...] = jnp.zeros_like(stb_ref)

    def lower_bound(ref, head):
        l = ref[:, head * LANE:(head + 1) * LANE]
        e = jnp.exp(l - jnp.max(l, axis=0, keepdims=True))
        return e[0:1, :] / jnp.sum(e, axis=0, keepdims=True)

    scans = []
    lag = 2 * qf_ref.shape[0] - 1
    for h in range(qf_ref.shape[0]):
        scans.append(_hgrn_direction(qf_ref.at[h], zf_ref.at[h], vf_ref.at[h],
                                     lower_bound(lbf_ref, h), stf_ref.at[h], of_ref.at[h],
                                     (k_s, b_s, v_s), forward=True, lag=lag))
        scans.append(_hgrn_direction(qb_ref.at[h], zb_ref.at[h], vb_ref.at[h],
                                     lower_bound(lbb_ref, h), stb_ref.at[h], ob_ref.at[h],
                                     (k_s, b_s, v_s), forward=False, lag=lag))
    _run_staggered(scans)


def _hgrn(proj, lbf_logits, lbb_logits, col_q, col_ff, col_fb, col_v, tb=1024):
    bsz, _, s, _ = proj.shape
    nb = s // tb
    hps = HGRN_HEADS_PER_STEP

    def fwd(col):
        return pl.BlockSpec((None, hps, tb, LANE), lambda b, h, i: (b, col // hps + h, i, 0))

    def bwd(col):
        return pl.BlockSpec((None, hps, tb, LANE),
                            lambda b, h, i: (b, col // hps + h, nb - 1 - i, 0))

    assert all(col % hps == 0 for col in (col_q, col_ff, col_fb, col_v))
    lb_spec = pl.BlockSpec((lbf_logits.shape[0], hps * LANE), lambda b, h, i: (0, h))
    out_shape = jax.ShapeDtypeStruct((bsz, HGRN_HEADS, s, HGRN_DV), BF16)
    return pl.pallas_call(
        _hgrn_kernel,
        grid=(bsz, HGRN_HEADS // hps, nb),
        in_specs=[fwd(col_q), fwd(col_ff), fwd(col_v),
                  bwd(col_q), bwd(col_fb), bwd(col_v), lb_spec, lb_spec],
        out_specs=[fwd(0), bwd(0)],
        out_shape=[out_shape, out_shape],
        scratch_shapes=[pltpu.VMEM((hps, HGRN_DV, HGRN_DK), F32),
                        pltpu.VMEM((hps, HGRN_DV, HGRN_DK), F32),
                        pltpu.VMEM((tb, HGRN_DK), F32), pltpu.VMEM((tb, HGRN_DK), F32),
                        pltpu.VMEM((tb, HGRN_DV), F32)],
        compiler_params=_cparams("parallel", "parallel", "arbitrary"),
        name="hgrn",
    )(proj, proj, proj, proj, proj, proj, lbf_logits, lbb_logits)


def _mix_out_kernel(attn_ref, of_ref, ob_ref, hg_ref, hgn_ref, w_ref, x_ref, mod_ref,
                    postg_ref, preg_ref, x1_ref, h2_ref):
    na = attn_ref.shape[-1]
    tm = attn_ref.shape[0]
    tiles = [slice(r, r + MIX_SUB_ROWS) for r in range(0, tm, MIX_SUB_ROWS)]
    recs = []
    for t in tiles:
        rec = []
        for h in range(HGRN_HEADS):
            o = of_ref[h, t, :].astype(F32) + ob_ref[h, t, :].astype(F32)
            hg = hg_ref[h, t, :].astype(F32)
            rec.append((_rms(o, hgn_ref[...]) * (hg * jax.nn.sigmoid(hg))).astype(BF16))
        recs.append(jnp.concatenate(rec, axis=-1))
    mixes = [_dot(attn_ref[t, :], w_ref[0:na, :]) + _dot(rec, w_ref[na:, :])
             for t, rec in zip(tiles, recs)]
    post_gain = mod_ref[2:3, :] * postg_ref[...]
    pre_gain = preg_ref[...] * (1.0 + mod_ref[4:5, :])
    for t, mix in zip(tiles, mixes):
        x1 = x_ref[t, :] + _rms(mix, post_gain)
        x1_ref[t, :] = x1
        h2_ref[t, :] = (_rms(x1, pre_gain) + mod_ref[3:4, :]).astype(BF16)


def _mix_out(attn, o_f, o_b, proj, col_hg, hgn_g, w_out, x, mod3, post_g, pre_g, tm=512):
    bsz, s, d = x.shape
    na = attn.shape[-1]
    heads = o_f.shape[1]

    def row(width):
        return pl.BlockSpec((None, tm, width), lambda b, i: (b, i, 0))

    def per_head(first):
        return pl.BlockSpec((None, heads, tm, LANE), lambda b, i: (b, first // heads, i, 0))

    def const(shape):
        return pl.BlockSpec(shape, lambda b, i: (0,) * len(shape))

    assert col_hg % heads == 0
    return pl.pallas_call(
        _mix_out_kernel,
        grid=(bsz, s // tm),
        in_specs=[row(na), per_head(0), per_head(0), per_head(col_hg),
                  const((1, HGRN_DV)), const(w_out.shape), row(d),
                  pl.BlockSpec((None, N_MOD, d), lambda b, i: (b, 0, 0)),
                  const((1, d)), const((1, d))],
        out_specs=[row(d), row(d)],
        out_shape=[jax.ShapeDtypeStruct((bsz, s, d), F32),
                   jax.ShapeDtypeStruct((bsz, s, d), BF16)],
        compiler_params=_cparams("parallel", "parallel"),
        name="mix_out",
    )(attn, o_f, o_b, proj, hgn_g, w_out, x, mod3, post_g, pre_g)


def _mlp_kernel(h_ref, wu_ref, wd_ref, x1_ref, mod_ref, g_ref, o_ref):
    j = pl.program_id(2)
    last = pl.num_programs(2) - 1

    @pl.when(j == 0)
    def _():
        o_ref[...] = jnp.zeros_like(o_ref)

    def activation():
        u = jnp.maximum(_dot(h_ref[...], wu_ref[...]), 0.0)
        return (u * u).astype(BF16)

    @pl.when(j < last)
    def _():
        o_ref[...] += _dot(activation(), wd_ref[...])

    @pl.when(j == last)
    def _():
        a = activation()
        gain = mod_ref[5:6, :] * g_ref[...]
        for r in range(0, o_ref.shape[0], MLP_SUB_ROWS):
            t = slice(r, r + MLP_SUB_ROWS)
            y = o_ref[t, :] + _dot(a[t, :], wd_ref[...])
            o_ref[t, :] = x1_ref[t, :] + _rms(y, gain)


def _mlp(h2, w_up, w_down, x1, mod3, g, tm=512, tf=1024):
    bsz, s, d = x1.shape
    dff = w_up.shape[1]
    return pl.pallas_call(
        _mlp_kernel,
        grid=(bsz, s // tm, dff // tf),
        in_specs=[pl.BlockSpec((None, tm, d), lambda b, i, j: (b, i, 0)),
                  pl.BlockSpec((d, tf), lambda b, i, j: (0, j)),
                  pl.BlockSpec((tf, d), lambda b, i, j: (j, 0)),
                  pl.BlockSpec((None, tm, d), lambda b, i, j: (b, i, 0)),
                  pl.BlockSpec((None, N_MOD, d), lambda b, i, j: (b, 0, 0)),
                  pl.BlockSpec((1, d), lambda b, i, j: (0, 0))],
        out_specs=pl.BlockSpec((None, tm, d), lambda b, i, j: (b, i, 0)),
        out_shape=jax.ShapeDtypeStruct((bsz, s, d), F32),
        compiler_params=_cparams("parallel", "parallel", "arbitrary"),
        name="mlp",
    )(h2, w_up, w_down, x1, mod3, g)


def _pack_w_uq(w):
    r = w.shape[0]
    w = w.reshape(r, MLA_HEADS, MLA_NOPE + MLA_ROPE)
    pad = jnp.zeros((r, MLA_HEADS, MLA_QK_PAD - MLA_NOPE - MLA_ROPE), w.dtype)
    return jnp.concatenate([w, pad], axis=-1).reshape(r, MLA_HEADS * MLA_QK_PAD).astype(BF16)


def _pack_w_ukv(w):
    r = w.shape[0]
    w = w.reshape(r, MLA_HEADS, MLA_NOPE + MLA_V)
    wk = w[:, :, :MLA_NOPE].reshape(r, -1).astype(BF16)
    wvt = w[:, :, MLA_NOPE:].reshape(r, -1).T.astype(BF16)
    return wk, wvt


def _rope_inv_freq():
    half = MLA_ROPE // 2
    f = ROPE_BASE ** (-jnp.arange(half, dtype=F32) / half)
    return jnp.tile(f, ROPE_PACK).reshape(1, LANE)


def kernel(x, c, positions, w_mod, b_mod, pre_mix_g, post_mix_g, pre_mlp_g, post_mlp_g,
           w_in, q_norm_g, kv_norm_g, w_uq, w_ukv, hgrn_norm_g, hgrn_lb_logits_fwd,
           hgrn_lb_logits_bwd, w_out, w_up, w_down):
    depth = w_mod.shape[0]
    assert depth == 1, "lower-bound slot selection is written for a single layer"
    bsz, s, d = x.shape
    q_rank = q_norm_g.shape[-1]
    kv_rank = kv_norm_g.shape[-1]
    invf = _rope_inv_freq()
    for layer in range(depth):
        mod3 = _mod(c, w_mod[layer], b_mod[layer]).reshape(bsz, N_MOD, d)
        lat, proj = _in_proj(x, mod3, pre_mix_g[layer].reshape(1, d),
                             w_in[layer].T.astype(BF16), q_rank + kv_rank + MLA_ROPE)
        wk, wvt = _pack_w_ukv(w_ukv[layer])
        q, k, vt = _mla_prep(lat, positions, invf, q_norm_g[layer].reshape(1, -1),
                             kv_norm_g[layer].reshape(1, -1), _pack_w_uq(w_uq[layer]),
                             wk, wvt, q_rank, kv_rank)
        attn, (w_out_b, w_up_b, w_down_b) = _attention(
            q, k, vt, (w_out[layer], w_up[layer], w_down[layer]))
        o_f, o_b = _hgrn(proj, hgrn_lb_logits_fwd, hgrn_lb_logits_bwd,
                         col_q=0, col_ff=8, col_fb=16, col_v=24)
        x1, h2 = _mix_out(attn, o_f, o_b, proj, 32, hgrn_norm_g[layer].reshape(1, -1),
                          w_out_b, x, mod3,
                          post_mix_g[layer].reshape(1, d), pre_mlp_g[layer].reshape(1, d))
        x = _mlp(h2, w_up_b, w_down_b, x1, mod3, post_mlp_g[layer].reshape(1, d))
    return x
```

```python
import functools

import jax
import jax.numpy as jnp
import numpy as np
from jax import lax
from jax.experimental import pallas as pl
from jax.experimental.pallas import tpu as pltpu

F32 = jnp.float32
BF16 = jnp.bfloat16

NORM_EPS = 1e-6
ROPE_BASE = 10000.0
N_MOD = 6
LANE = 128
VMEM_LIMIT = 56 * 1024 * 1024

MOD_BUFFERS = 4
MLA_HEADS = 8
MLA_NOPE = 128
MLA_ROPE = 64
MLA_V = 128
MLA_QK_PAD = 256
ROPE_PACK = LANE // (MLA_ROPE // 2)
ATTN_TILES_PER_TRIP = 4
MLA_LAT_PAD = 1024
HGRN_HEADS = 8
HGRN_DK = 128
HGRN_DV = 128
HGRN_CHUNK = 128
HGRN_HEADS_PER_STEP = 2
HGRN_SAFE_LOG2_SPREAD = 96.0
NORM_SUB_ROWS = 256
MLP_SUB_ROWS = 256
MIX_SUB_ROWS = 128


def _cparams(*sem):
    return pltpu.CompilerParams(dimension_semantics=sem, vmem_limit_bytes=VMEM_LIMIT)


def _dot(a, b):
    return jnp.dot(a, b, preferred_element_type=F32)


def _dot_nt(a, b):
    return lax.dot_general(a, b, (((1,), (1,)), ((), ())), preferred_element_type=F32)


def _rms(x, g):
    return x * lax.rsqrt(jnp.mean(x * x, axis=-1, keepdims=True) + NORM_EPS) * g


def _mod_kernel(c_ref, w_hbm, b_ref, o_ref, buf, sem, *, tn):
    n_tiles = o_ref.shape[1] // tn

    def tile_copy(j):
        slot = j % MOD_BUFFERS
        return pltpu.make_async_copy(w_hbm.at[:, pl.ds(j * tn, tn)], buf.at[slot],
                                     sem.at[slot])

    for j in range(min(MOD_BUFFERS, n_tiles)):
        tile_copy(j).start()
    c = c_ref[...]
    cond = c * jax.nn.sigmoid(c)
    hi = cond.astype(BF16)
    lo = (cond - hi.astype(F32)).astype(BF16)
    for j in range(n_tiles):
        cols = slice(j * tn, (j + 1) * tn)
        tile_copy(j).wait()
        w = buf[j % MOD_BUFFERS].astype(BF16)
        o_ref[:, cols] = _dot(hi, w) + _dot(lo, w) + b_ref[:, cols]
        if j + MOD_BUFFERS < n_tiles:
            tile_copy(j + MOD_BUFFERS).start()


def _mod(c, w_mod, b_mod, tn=1024):
    bsz, d = c.shape
    n = w_mod.shape[1]
    assert n % tn == 0
    return pl.pallas_call(
        functools.partial(_mod_kernel, tn=tn),
        in_specs=[pl.BlockSpec(memory_space=pltpu.VMEM),
                  pl.BlockSpec(memory_space=pl.ANY),
                  pl.BlockSpec(memory_space=pltpu.VMEM)],
        out_specs=pl.BlockSpec(memory_space=pltpu.VMEM),
        out_shape=jax.ShapeDtypeStruct((bsz, n), F32),
        scratch_shapes=[pltpu.VMEM((MOD_BUFFERS, d, tn), F32),
                        pltpu.SemaphoreType.DMA((MOD_BUFFERS,))],
        compiler_params=pltpu.CompilerParams(vmem_limit_bytes=VMEM_LIMIT),
        name="mod",
    )(c, w_mod, b_mod.reshape(1, n))


def _in_proj_kernel(x_ref, mod_ref, g_ref, wa_ref, wb_ref, oa_ref, ob_ref, h_ref):
    @pl.when(pl.program_id(2) == 0)
    def _():
        gain = g_ref[...] * (1.0 + mod_ref[1:2, :])
        for r in range(0, x_ref.shape[0], NORM_SUB_ROWS):
            t = slice(r, r + NORM_SUB_ROWS)
            h = (_rms(x_ref[t, :], gain) + mod_ref[0:1, :]).astype(BF16)
            h_ref[t, :] = h
            oa_ref[t, :] = _dot_nt(h, wa_ref[...]).astype(oa_ref.dtype)

    y = _dot_nt(h_ref[...], wb_ref[...]).astype(ob_ref.dtype)
    for c in range(ob_ref.shape[0]):
        ob_ref[c] = y[:, c * LANE:(c + 1) * LANE]


def _in_proj(x, mod3, g, wt, head, tm=1024, tn=1280):
    bsz, s, d = x.shape
    na, nb = MLA_LAT_PAD, wt.shape[0] - head
    row_align = int(np.gcd(head, tn))
    assert row_align % 16 == 0
    return pl.pallas_call(
        _in_proj_kernel,
        grid=(bsz, s // tm, nb // tn),
        in_specs=[pl.BlockSpec((None, tm, d), lambda b, i, j: (b, i, 0)),
                  pl.BlockSpec((None, N_MOD, d), lambda b, i, j: (b, 0, 0)),
                  pl.BlockSpec((1, d), lambda b, i, j: (0, 0)),
                  pl.BlockSpec((pl.Element(na), pl.Element(d)), lambda b, i, j: (0, 0)),
                  pl.BlockSpec((pl.Element(tn), pl.Element(d)),
                               lambda b, i, j: (pl.multiple_of(head + j * tn, row_align), 0))],
        out_specs=[pl.BlockSpec((None, tm, na), lambda b, i, j: (b, i, 0)),
                   pl.BlockSpec((None, tn // LANE, tm, LANE), lambda b, i, j: (b, j, i, 0))],
        out_shape=[jax.ShapeDtypeStruct((bsz, s, na), BF16),
                   jax.ShapeDtypeStruct((bsz, nb // LANE, s, LANE), BF16)],
        scratch_shapes=[pltpu.VMEM((tm, d), BF16)],
        compiler_params=_cparams("parallel", "parallel", "arbitrary"),
        name="in_proj",
    )(x, mod3, g, wt, wt)


def _rope_tile(t, cos_m, sin_lo, sin_hi):
    half = MLA_ROPE // 2
    return (t * cos_m + pltpu.roll(t, LANE - half, 1) * sin_lo
            + pltpu.roll(t, half, 1) * sin_hi)


def _rope_tables(pos_ref, invf_ref, tab_ref):
    half = MLA_ROPE // 2
    rows = pos_ref.shape[0]
    pos = pos_ref[...].astype(F32)
    lane = lax.broadcasted_iota(jnp.int32, (rows, LANE), 1)
    spread = pos[:, ROPE_PACK - 1:ROPE_PACK]
    for g in range(ROPE_PACK - 2, -1, -1):
        spread = jnp.where(lane < (g + 1) * half, pos[:, g:g + 1], spread)
    ang = spread * invf_ref[...]
    cos, sin = jnp.cos(ang), jnp.sin(ang)
    for g in range(ROPE_PACK):
        c = pltpu.roll(cos, LANE - g * half, 1) if g else cos
        s = pltpu.roll(sin, LANE - g * half, 1) if g else sin
        in_lo = lane < half
        in_hi = (lane >= half) & (lane < MLA_ROPE)
        dst = pl.ds(g, rows, stride=ROPE_PACK)
        tab_ref[0, dst, :] = jnp.where(in_lo, c, jnp.where(in_hi, pltpu.roll(c, half, 1), 0.0))
        tab_ref[1, dst, :] = jnp.where(in_lo, -s, 0.0)
        tab_ref[2, dst, :] = jnp.where(in_hi, pltpu.roll(s, half, 1), 0.0)


def _mla_prep_kernel(p_ref, pos_ref, invf_ref, qg_ref, kvg_ref, wq_ref, wk_ref, wvt_ref,
                     q_ref, k_ref, vt_ref, tab_ref, *, q_rank, kv_rank, scale):
    _rope_tables(pos_ref, invf_ref, tab_ref)
    cos_m, sin_lo, sin_hi = tab_ref[0], tab_ref[1], tab_ref[2]

    q_lat = p_ref[:, 0:q_rank].astype(F32)
    q = _dot(_rms(q_lat, qg_ref[...]).astype(BF16), wq_ref[...])
    kv_lat = p_ref[:, q_rank:q_rank + kv_rank].astype(F32)
    kv_n = _rms(kv_lat, kvg_ref[...]).astype(BF16)
    k_nope = _dot(kv_n, wk_ref[...])
    vt_ref[...] = _dot_nt(wvt_ref[...], kv_n).astype(BF16)
    kr_off = q_rank + kv_rank
    k_rope = _rope_tile(p_ref[:, kr_off:kr_off + LANE].astype(F32),
                        cos_m, sin_lo, sin_hi).astype(BF16)
    for h in range(MLA_HEADS):
        o = h * MLA_QK_PAD
        q_ref[:, o:o + MLA_NOPE] = (q[:, o:o + MLA_NOPE] * scale).astype(BF16)
        qr = _rope_tile(q[:, o + MLA_NOPE:o + MLA_QK_PAD], cos_m, sin_lo, sin_hi)
        q_ref[:, o + MLA_NOPE:o + MLA_QK_PAD] = (qr * scale).astype(BF16)
        k_ref[:, o:o + MLA_NOPE] = k_nope[:, h * MLA_NOPE:(h + 1) * MLA_NOPE].astype(BF16)
        k_ref[:, o + MLA_NOPE:o + MLA_QK_PAD] = k_rope


def _mla_prep(proj, positions, invf, qg, kvg, wq, wk, wvt, q_rank, kv_rank, tm=512):
    bsz, s, blk0 = proj.shape
    hq = MLA_HEADS * MLA_QK_PAD
    hv = MLA_HEADS * MLA_V
    scale = float((MLA_NOPE + MLA_ROPE) ** -0.5 * np.log2(np.e))
    kern = functools.partial(_mla_prep_kernel, q_rank=q_rank, kv_rank=kv_rank, scale=scale)

    def const(a):
        return pl.BlockSpec(a.shape, lambda b, i: (0, 0))

    return pl.pallas_call(
        kern,
        grid=(bsz, s // tm),
        in_specs=[pl.BlockSpec((None, tm, blk0), lambda b, i: (b, i, 0)),
                  pl.BlockSpec((None, tm // ROPE_PACK, ROPE_PACK), lambda b, i: (b, i, 0)),
                  const(invf), const(qg), const(kvg), const(wq), const(wk), const(wvt)],
        out_specs=[pl.BlockSpec((None, tm, hq), lambda b, i: (b, i, 0)),
                   pl.BlockSpec((None, tm, hq), lambda b, i: (b, i, 0)),
                   pl.BlockSpec((None, hv, tm), lambda b, i: (b, 0, i))],
        out_shape=[jax.ShapeDtypeStruct((bsz, s, hq), BF16),
                   jax.ShapeDtypeStruct((bsz, s, hq), BF16),
                   jax.ShapeDtypeStruct((bsz, hv, s), BF16)],
        scratch_shapes=[pltpu.VMEM((3, tm, LANE), F32)],
        compiler_params=_cparams("parallel", "parallel"),
        name="mla_prep",
    )(proj, positions.reshape(bsz, s // ROPE_PACK, ROPE_PACK), invf, qg, kvg, wq, wk, wvt)


def _attn_kernel(*refs, tq, tk, n_cast):
    q_ref, k_ref, vt_ref = refs[:3]
    o_ref = refs[3 + n_cast]
    sa_ref, sb_ref = refs[-2:]
    for src, dst in zip(refs[3:3 + n_cast], refs[4 + n_cast:4 + 2 * n_cast]):
        dst[...] = src[...].astype(dst.dtype)
    s_len = k_ref.shape[0]
    n_kv = s_len // tk
    n_q = s_len // tq
    sub = 8

    def scores(i, s_ref):
        q = q_ref[pl.ds(pl.multiple_of(i * tq, tq), tq), :]
        m8 = jnp.full((sub, tq), -jnp.inf, F32)
        for j in range(n_kv):
            st = _dot_nt(k_ref[j * tk:(j + 1) * tk, :], q)
            s_ref[j * tk:(j + 1) * tk, :] = st
            m8 = jnp.maximum(m8, jnp.max(st.reshape(tk // sub, sub, tq), axis=0))
            yield
        return jnp.broadcast_to(jnp.max(m8, axis=0, keepdims=True), (sub, tq))

    def values(i, s_ref, m8):
        l8 = jnp.zeros((sub, tq), F32)
        acc = jnp.zeros((MLA_V, tq), F32)
        for j in range(n_kv):
            st = s_ref[j * tk:(j + 1) * tk, :].reshape(tk // sub, sub, tq)
            p = jnp.exp2(st - m8[None])
            l8 = l8 + jnp.sum(p, axis=0)
            acc = acc + _dot(vt_ref[:, j * tk:(j + 1) * tk],
                             p.reshape(tk, tq).astype(BF16))
            yield
        l = jnp.sum(l8, axis=0, keepdims=True)
        o_ref[pl.ds(pl.multiple_of(i * tq, tq), tq), :] = (acc / l).T.astype(o_ref.dtype)

    def run(*gens):
        result = None
        live = list(gens)
        while live:
            for g in list(live):
                try:
                    next(g)
                except StopIteration as stop:
                    if g is gens[0]:
                        result = stop.value
                    live.remove(g)
        return result

    bufs = (sa_ref, sb_ref)
    group = ATTN_TILES_PER_TRIP

    def tiles(first, count, m):
        for u in range(count):
            m = run(scores(first + u + 1, bufs[(u + 1) % 2]),
                    values(first + u, bufs[u % 2], m))
        return m

    m = lax.fori_loop(0, n_q // group - 1, lambda k, m: tiles(group * k, group, m),
                      run(scores(0, bufs[0])))
    m = tiles(n_q - group, group - 1, m)
    run(values(n_q - 1, bufs[(group - 1) % 2], m))


def _attention(q, k, vt, f32_weights, tq=256, tk=2048):
    bsz, s, _ = q.shape
    steps = bsz * MLA_HEADS
    kern = functools.partial(_attn_kernel, tq=tq, tk=tk, n_cast=len(f32_weights))

    def slab(w):
        return pl.BlockSpec((w.shape[0] // steps, w.shape[1]),
                            lambda b, h: (b * MLA_HEADS + h, 0))

    assert all(w.shape[0] % (8 * steps) == 0 for w in f32_weights)
    out = pl.pallas_call(
        kern,
        grid=(bsz, MLA_HEADS),
        in_specs=[pl.BlockSpec((None, s, MLA_QK_PAD), lambda b, h: (b, 0, h)),
                  pl.BlockSpec((None, s, MLA_QK_PAD), lambda b, h: (b, 0, h)),
                  pl.BlockSpec((None, MLA_V, s), lambda b, h: (b, h, 0))]
                 + [slab(w) for w in f32_weights],
        out_specs=[pl.BlockSpec((None, s, MLA_V), lambda b, h: (b, 0, h))]
                  + [slab(w) for w in f32_weights],
        out_shape=[jax.ShapeDtypeStruct((bsz, s, MLA_HEADS * MLA_V), BF16)]
                  + [jax.ShapeDtypeStruct(w.shape, BF16) for w in f32_weights],
        scratch_shapes=[pltpu.VMEM((s, tq), F32), pltpu.VMEM((s, tq), F32)],
        compiler_params=_cparams("parallel", "parallel"),
        name="attention",
    )(q, k, vt, *f32_weights)
    return out[0], out[1:]


def _run_staggered(gens):
    live = list(enumerate(gens))
    tick = 0
    while live:
        for delay, g in list(live):
            if tick >= delay and next(g, StopIteration) is StopIteration:
                live.remove((delay, g))
        tick += 1


def _hgrn_exact_block(q, k, b, v_ref, qs, states, o_ref, k_s, b_s, v_s, *, forward):
    n, c, _ = q.shape
    k_s[...] = k.reshape(n * c, HGRN_DK)
    b_s[...] = b.reshape(n * c, HGRN_DK)
    v_s[...] = v_ref[...].astype(F32)
    t_idx = lax.broadcasted_iota(jnp.int32, (c, 1), 0)
    for j in range(n):
        def key_row(s, acc, j=j):
            row = pl.ds(j * c + s, 1)
            w = jnp.exp2(jnp.minimum(b[j] - b_s[row, :], 0.0))
            seen = (t_idx >= s) if forward else (t_idx <= s)
            a_col = jnp.sum(jnp.where(seen, q[j] * w * k_s[row, :], 0.0),
                            axis=1, keepdims=True)
            return acc + a_col * v_s[row, :]

        intra = lax.fori_loop(0, c, key_row, jnp.zeros((c, HGRN_DV), F32))
        inter = _dot_nt(qs[j], states[j])
        o_ref[j * c:(j + 1) * c, :] = (inter + intra).astype(o_ref.dtype)


def _hgrn_direction(q_ref, z_ref, v_ref, lb, st_ref, o_ref, exact_scratch, *, forward, lag):
    c = HGRN_CHUNK
    n = q_ref.shape[0] // c
    row = lax.broadcasted_iota(jnp.int32, (c, c), 0)
    col = lax.broadcasted_iota(jnp.int32, (c, c), 1)
    causal = (col <= row) if forward else (col >= row)
    tri = causal.astype(BF16)
    edge = c - 1 if forward else 0

    chunks = [slice(j * c, (j + 1) * c) for j in range(n)]
    f = lb + (1.0 - lb) * jax.nn.sigmoid(z_ref[...].astype(F32))
    k = (1.0 - f).reshape(n, c, HGRN_DK)
    log_f = jnp.log2(f).astype(BF16)
    q = q_ref[...].reshape(n, c, HGRN_DK)
    vt = v_ref[...].astype(F32).T.astype(BF16)
    yield
    b = jnp.concatenate([_dot(tri, log_f[r]) for r in chunks], axis=0)
    yield
    b = b.reshape(n, c, HGRN_DK)
    b_edge = b[:, edge:edge + 1]
    h = c // 2
    near, far = (slice(0, h), slice(h, c)) if forward else (slice(h, c), slice(0, h))
    ref = h - 1 if forward else h
    b_ref = b[:, ref:ref + 1]
    kb = k.astype(BF16)
    q_near = q[:, near] * jnp.exp2(b[:, near]).astype(BF16)
    q_far = q[:, far] * jnp.exp2(b[:, far] - b_ref).astype(BF16)
    k_near = kb[:, near] * jnp.exp2(-b[:, near]).astype(BF16)
    k_ref = kb * jnp.exp2(b_ref - b).astype(BF16)
    kd = kb * jnp.exp2(b_edge - b).astype(BF16)
    qs_far = q_far * jnp.exp2(b_ref).astype(BF16)
    qs = jnp.concatenate([q_near, qs_far] if forward else [qs_far, q_near], axis=1)
    decay = jnp.exp2(b_edge)
    spread = jnp.maximum(jnp.max(-b_ref), jnp.max(b_ref - b_edge))
    yield
    a_near = [_dot_nt(q_near[j], k_near[j]) for j in range(n)]
    a_far = [_dot_nt(q_far[j], k_ref[j]) for j in range(n)]
    upd = [_dot(vt[:, chunks[j]], kd[j]) for j in range(n)]
    yield
    blank = jnp.zeros((h, h), F32)
    if forward:
        a = [jnp.concatenate([jnp.concatenate([an, blank], axis=1), af], axis=0)
             for an, af in zip(a_near, a_far)]
    else:
        a = [jnp.concatenate([af, jnp.concatenate([blank, an], axis=1)], axis=0)
             for an, af in zip(a_near, a_far)]
    a = [jnp.where(causal, x, 0.0).astype(BF16) for x in a]
    st = st_ref[...]
    states = [None] * n
    for j in (range(n) if forward else reversed(range(n))):
        states[j] = st.astype(BF16)
        st = st * decay[j] + upd[j]
    st_ref[...] = st
    yield
    outs = [_dot_nt(jnp.concatenate([a[j], qs[j]], axis=1),
                    jnp.concatenate([vt[:, chunks[j]], states[j]], axis=1))
            for j in range(n)]
    o_ref[...] = jnp.concatenate(outs, axis=0).astype(o_ref.dtype)
    yield
    for _ in range(lag):
        yield

    @pl.when(spread > HGRN_SAFE_LOG2_SPREAD)
    def _():
        _hgrn_exact_block(q.astype(F32), k, b, v_ref, qs, states, o_ref, *exact_scratch,
                          forward=forward)
    yield


def _hgrn_kernel(qf_ref, zf_ref, vf_ref, qb_ref, zb_ref, vb_ref, lbf_ref, lbb_ref,
                 of_ref, ob_ref, stf_ref, stb_ref, k_s, b_s, v_s):
    @pl.when(pl.program_id(2) == 0)
    def _():
        stf_ref[...] = jnp.zeros_like(stf_ref)
        stb_ref[...] = jnp.zeros_like(stb_ref)

    def lower_bound(ref, head):
        l = ref[:, head * LANE:(head + 1) * LANE]
        e = jnp.exp(l - jnp.max(l, axis=0, keepdims=True))
        return e[0:1, :] / jnp.sum(e, axis=0, keepdims=True)

    scans = []
    lag = 2 * qf_ref.shape[0] - 1
    for h in range(qf_ref.shape[0]):
        scans.append(_hgrn_direction(qf_ref.at[h], zf_ref.at[h], vf_ref.at[h],
                                     lower_bound(lbf_ref, h), stf_ref.at[h], of_ref.at[h],
                                     (k_s, b_s, v_s), forward=True, lag=lag))
        scans.append(_hgrn_direction(qb_ref.at[h], zb_ref.at[h], vb_ref.at[h],
                                     lower_bound(lbb_ref, h), stb_ref.at[h], ob_ref.at[h],
                                     (k_s, b_s, v_s), forward=False, lag=lag))
    _run_staggered(scans)


def _hgrn(proj, lbf_logits, lbb_logits, col_q, col_ff, col_fb, col_v, tb=1024):
    bsz, _, s, _ = proj.shape
    nb = s // tb
    hps = HGRN_HEADS_PER_STEP

    def fwd(col):
        return pl.BlockSpec((None, hps, tb, LANE), lambda b, h, i: (b, col // hps + h, i, 0))

    def bwd(col):
        return pl.BlockSpec((None, hps, tb, LANE),
                            lambda b, h, i: (b, col // hps + h, nb - 1 - i, 0))

    assert all(col % hps == 0 for col in (col_q, col_ff, col_fb, col_v))
    lb_spec = pl.BlockSpec((lbf_logits.shape[0], hps * LANE), lambda b, h, i: (0, h))
    out_shape = jax.ShapeDtypeStruct((bsz, HGRN_HEADS, s, HGRN_DV), BF16)
    return pl.pallas_call(
        _hgrn_kernel,
        grid=(bsz, HGRN_HEADS // hps, nb),
        in_specs=[fwd(col_q), fwd(col_ff), fwd(col_v),
                  bwd(col_q), bwd(col_fb), bwd(col_v), lb_spec, lb_spec],
        out_specs=[fwd(0), bwd(0)],
        out_shape=[out_shape, out_shape],
        scratch_shapes=[pltpu.VMEM((hps, HGRN_DV, HGRN_DK), F32),
                        pltpu.VMEM((hps, HGRN_DV, HGRN_DK), F32),
                        pltpu.VMEM((tb, HGRN_DK), F32), pltpu.VMEM((tb, HGRN_DK), F32),
                        pltpu.VMEM((tb, HGRN_DV), F32)],
        compiler_params=_cparams("parallel", "parallel", "arbitrary"),
        name="hgrn",
    )(proj, proj, proj, proj, proj, proj, lbf_logits, lbb_logits)


def _mix_out_kernel(attn_ref, of_ref, ob_ref, hg_ref, hgn_ref, w_ref, x_ref, mod_ref,
                    postg_ref, preg_ref, x1_ref, h2_ref):
    na = attn_ref.shape[-1]
    tm = attn_ref.shape[0]
    tiles = [slice(r, r + MIX_SUB_ROWS) for r in range(0, tm, MIX_SUB_ROWS)]
    recs = []
    for t in tiles:
        rec = []
        for h in range(HGRN_HEADS):
            o = of_ref[h, t, :].astype(F32) + ob_ref[h, t, :].astype(F32)
            hg = hg_ref[h, t, :].astype(F32)
            rec.append((_rms(o, hgn_ref[...]) * (hg * jax.nn.sigmoid(hg))).astype(BF16))
        recs.append(jnp.concatenate(rec, axis=-1))
    mixes = [_dot(attn_ref[t, :], w_ref[0:na, :]) + _dot(rec, w_ref[na:, :])
             for t, rec in zip(tiles, recs)]
    post_gain = mod_ref[2:3, :] * postg_ref[...]
    pre_gain = preg_ref[...] * (1.0 + mod_ref[4:5, :])
    for t, mix in zip(tiles, mixes):
        x1 = x_ref[t, :] + _rms(mix, post_gain)
        x1_ref[t, :] = x1
        h2_ref[t, :] = (_rms(x1, pre_gain) + mod_ref[3:4, :]).astype(BF16)


def _mix_out(attn, o_f, o_b, proj, col_hg, hgn_g, w_out, x, mod3, post_g, pre_g, tm=512):
    bsz, s, d = x.shape
    na = attn.shape[-1]
    heads = o_f.shape[1]

    def row(width):
        return pl.BlockSpec((None, tm, width), lambda b, i: (b, i, 0))

    def per_head(first):
        return pl.BlockSpec((None, heads, tm, LANE), lambda b, i: (b, first // heads, i, 0))

    def const(shape):
        return pl.BlockSpec(shape, lambda b, i: (0,) * len(shape))

    assert col_hg % heads == 0
    return pl.pallas_call(
        _mix_out_kernel,
        grid=(bsz, s // tm),
        in_specs=[row(na), per_head(0), per_head(0), per_head(col_hg),
                  const((1, HGRN_DV)), const(w_out.shape), row(d),
                  pl.BlockSpec((None, N_MOD, d), lambda b, i: (b, 0, 0)),
                  const((1, d)), const((1, d))],
        out_specs=[row(d), row(d)],
        out_shape=[jax.ShapeDtypeStruct((bsz, s, d), F32),
                   jax.ShapeDtypeStruct((bsz, s, d), BF16)],
        compiler_params=_cparams("parallel", "parallel"),
        name="mix_out",
    )(attn, o_f, o_b, proj, hgn_g, w_out, x, mod3, post_g, pre_g)


def _mlp_kernel(h_ref, wu_ref, wd_ref, x1_ref, mod_ref, g_ref, o_ref):
    j = pl.program_id(2)
    last = pl.num_programs(2) - 1

    @pl.when(j == 0)
    def _():
        o_ref[...] = jnp.zeros_like(o_ref)

    def activation():
        u = jnp.maximum(_dot(h_ref[...], wu_ref[...]), 0.0)
        return (u * u).astype(BF16)

    @pl.when(j < last)
    def _():
        o_ref[...] += _dot(activation(), wd_ref[...])

    @pl.when(j == last)
    def _():
        a = activation()
        gain = mod_ref[5:6, :] * g_ref[...]
        for r in range(0, o_ref.shape[0], MLP_SUB_ROWS):
            t = slice(r, r + MLP_SUB_ROWS)
            y = o_ref[t, :] + _dot(a[t, :], wd_ref[...])
            o_ref[t, :] = x1_ref[t, :] + _rms(y, gain)


def _mlp(h2, w_up, w_down, x1, mod3, g, tm=512, tf=1024):
    bsz, s, d = x1.shape
    dff = w_up.shape[1]
    return pl.pallas_call(
        _mlp_kernel,
        grid=(bsz, s // tm, dff // tf),
        in_specs=[pl.BlockSpec((None, tm, d), lambda b, i, j: (b, i, 0)),
                  pl.BlockSpec((d, tf), lambda b, i, j: (0, j)),
                  pl.BlockSpec((tf, d), lambda b, i, j: (j, 0)),
                  pl.BlockSpec((None, tm, d), lambda b, i, j: (b, i, 0)),
                  pl.BlockSpec((None, N_MOD, d), lambda b, i, j: (b, 0, 0)),
                  pl.BlockSpec((1, d), lambda b, i, j: (0, 0))],
        out_specs=pl.BlockSpec((None, tm, d), lambda b, i, j: (b, i, 0)),
        out_shape=jax.ShapeDtypeStruct((bsz, s, d), F32),
        compiler_params=_cparams("parallel", "parallel", "arbitrary"),
        name="mlp",
    )(h2, w_up, w_down, x1, mod3, g)


def _pack_w_uq(w):
    r = w.shape[0]
    w = w.reshape(r, MLA_HEADS, MLA_NOPE + MLA_ROPE)
    pad = jnp.zeros((r, MLA_HEADS, MLA_QK_PAD - MLA_NOPE - MLA_ROPE), w.dtype)
    return jnp.concatenate([w, pad], axis=-1).reshape(r, MLA_HEADS * MLA_QK_PAD).astype(BF16)


def _pack_w_ukv(w):
    r = w.shape[0]
    w = w.reshape(r, MLA_HEADS, MLA_NOPE + MLA_V)
    wk = w[:, :, :MLA_NOPE].reshape(r, -1).astype(BF16)
    wvt = w[:, :, MLA_NOPE:].reshape(r, -1).T.astype(BF16)
    return wk, wvt


def _rope_inv_freq():
    half = MLA_ROPE // 2
    f = ROPE_BASE ** (-jnp.arange(half, dtype=F32) / half)
    return jnp.tile(f, ROPE_PACK).reshape(1, LANE)


def kernel(x, c, positions, w_mod, b_mod, pre_mix_g, post_mix_g, pre_mlp_g, post_mlp_g,
           w_in, q_norm_g, kv_norm_g, w_uq, w_ukv, hgrn_norm_g, hgrn_lb_logits_fwd,
           hgrn_lb_logits_bwd, w_out, w_up, w_down):
    depth = w_mod.shape[0]
    assert depth == 1, "lower-bound slot selection is written for a single layer"
    bsz, s, d = x.shape
    q_rank = q_norm_g.shape[-1]
    kv_rank = kv_norm_g.shape[-1]
    invf = _rope_inv_freq()
    for layer in range(depth):
        mod3 = _mod(c, w_mod[layer], b_mod[layer]).reshape(bsz, N_MOD, d)
        lat, proj = _in_proj(x, mod3, pre_mix_g[layer].reshape(1, d),
                             w_in[layer].T.astype(BF16), q_rank + kv_rank + MLA_ROPE)
        wk, wvt = _pack_w_ukv(w_ukv[layer])
        q, k, vt = _mla_prep(lat, positions, invf, q_norm_g[layer].reshape(1, -1),
                             kv_norm_g[layer].reshape(1, -1), _pack_w_uq(w_uq[layer]),
                             wk, wvt, q_rank, kv_rank)
        attn, (w_out_b, w_up_b, w_down_b) = _attention(
            q, k, vt, (w_out[layer], w_up[layer], w_down[layer]))
        o_f, o_b = _hgrn(proj, hgrn_lb_logits_fwd, hgrn_lb_logits_bwd,
                         col_q=0, col_ff=8, col_fb=16, col_v=24)
        x1, h2 = _mix_out(attn, o_f, o_b, proj, 32, hgrn_norm_g[layer].reshape(1, -1),
                          w_out_b, x, mod3,
                          post_mix_g[layer].reshape(1, d), pre_mlp_g[layer].reshape(1, d))
        x = _mlp(h2, w_up_b, w_down_b, x1, mod3, post_mlp_g[layer].reshape(1, d))
    return x
```
